```python
import jax, jax.numpy as jnp
from jax import lax
import numpy as np

D_MODEL = 1024
BATCH = 2
SEQ = 8192
DEPTH = 1
DEC_BATCH = 128
DEC_SEQ = 1
PAST_LEN = 16384
PAGE_SIZE = 128

MIX_WIDTH = D_MODEL
MLA_WIDTH = D_MODEL // 2
GLA_WIDTH = MIX_WIDTH - MLA_WIDTH
MLA_HEADS = 4
MLA_V = MLA_WIDTH // MLA_HEADS
MLA_NOPE = 128
MLA_ROPE = 64
Q_RANK = (3 * D_MODEL) // 8
KV_RANK = D_MODEL // 4
MLA_SCALE = (MLA_NOPE + MLA_ROPE) ** -0.5
ROPE_THETA = 10000.0
Q_BLOCK = 128
GLA_HEADS = 4
GLA_DV = GLA_WIDTH // GLA_HEADS
GLA_DK = GLA_DV // 2
GATE_RANK = 16
GATE_TAU = 16.0
GLA_CHUNK = 64
D_FF = 4 * D_MODEL
ALPHA = (2.0 * DEPTH) ** 0.25
BETA = (8.0 * DEPTH) ** -0.25
EPS = 1e-5
IN_SIZES = (Q_RANK, KV_RANK, MLA_ROPE, GLA_HEADS * GLA_DK, GLA_HEADS * GLA_DK,
            GLA_WIDTH, GATE_RANK, GLA_WIDTH)
IN_COLS = sum(IN_SIZES)

kernel_name = 'hymba_mla_gla_deepnorm_decode_step'


def rms_norm(x, g):
    xf = x.astype(jnp.float32)
    return (xf * lax.rsqrt(jnp.mean(xf * xf, -1, keepdims=True) + EPS) * g).astype(x.dtype)


def layer_norm(x, g, b):
    xf = x.astype(jnp.float32)
    mu = jnp.mean(xf, -1, keepdims=True)
    var = jnp.mean(jnp.square(xf - mu), -1, keepdims=True)
    return ((xf - mu) * lax.rsqrt(var + EPS) * g + b).astype(x.dtype)


def rope_tables(pos):
    inv = ROPE_THETA ** (-jnp.arange(0, MLA_ROPE, 2, dtype=jnp.float32) / MLA_ROPE)
    ang = pos.astype(jnp.float32)[:, None] * inv[None, :]
    return jnp.cos(ang), jnp.sin(ang)


def apply_rope(x, cos, sin):
    half = MLA_ROPE // 2
    x1, x2 = x[..., :half], x[..., half:]
    return jnp.concatenate([x1 * cos - x2 * sin, x2 * cos + x1 * sin], -1).astype(x.dtype)


def mixer_inputs(x, pos, w_in, q_norm, w_uq, kv_norm, w_uk, w_gate2, b_gate):
    B, T, _ = x.shape
    idx = np.cumsum(IN_SIZES)[:-1].tolist()
    cq, ckv, kr, gq, gk, gv, gr, gg = jnp.split(x @ w_in, idx, axis=-1)
    cos, sin = rope_tables(pos)
    q = (rms_norm(cq, q_norm) @ w_uq).reshape(B, T, MLA_HEADS, MLA_NOPE + MLA_ROPE)
    q_nope, q_rope = q[..., :MLA_NOPE], q[..., MLA_NOPE:]
    q_rope = apply_rope(q_rope, cos[:, None, :], sin[:, None, :])
    k_rope = apply_rope(kr, cos, sin)
    ckv = rms_norm(ckv, kv_norm)
    q_lat = jnp.einsum('bthn,rhn->bthr', q_nope, w_uk)
    gla_q = gq.reshape(B, T, GLA_HEADS, GLA_DK) * (GLA_DK ** -0.5)
    gla_k = gk.reshape(B, T, GLA_HEADS, GLA_DK)
    gla_v = gv.reshape(B, T, GLA_HEADS, GLA_DV)
    loga = jax.nn.log_sigmoid((gr @ w_gate2 + b_gate).astype(jnp.float32)) / GATE_TAU
    loga = loga.reshape(B, T, GLA_HEADS, GLA_DK)
    return q_lat, q_rope, ckv, k_rope, gla_q, gla_k, gla_v, loga, gg


def mla_prompt(q_lat, q_rope, ckv, krope):
    B, T = ckv.shape[:2]
    key_pos = jnp.arange(T)

    def block(i):
        start = i * Q_BLOCK
        qs = lax.dynamic_slice_in_dim(q_lat, start, Q_BLOCK, axis=1)
        qr = lax.dynamic_slice_in_dim(q_rope, start, Q_BLOCK, axis=1)
        s = (jnp.einsum('bqhr,bkr->bhqk', qs, ckv)
             + jnp.einsum('bqhe,bke->bhqk', qr, krope)).astype(jnp.float32) * MLA_SCALE
        qpos = start + jnp.arange(Q_BLOCK)
        s = jnp.where(key_pos[None, :] <= qpos[:, None], s, -jnp.inf)
        p = jax.nn.softmax(s, axis=-1).astype(ckv.dtype)
        return jnp.einsum('bhqk,bkr->bqhr', p, ckv)

    o = lax.map(block, jnp.arange(T // Q_BLOCK))
    return jnp.moveaxis(o, 0, 1).reshape(B, T, MLA_HEADS, KV_RANK)


def mla_sample(q_lat, q_rope, c_past, kr_past, c_new, kr_new):
    T = c_new.shape[1]
    s_past = jnp.einsum('bthr,bsr->bhts', q_lat, c_past) + jnp.einsum('bthe,bse->bhts', q_rope, kr_past)
    s_new = jnp.einsum('bthr,bur->bhtu', q_lat, c_new) + jnp.einsum('bthe,bue->bhtu', q_rope, kr_new)
    tri = jnp.arange(T)[:, None] >= jnp.arange(T)[None, :]
    s_new = jnp.where(tri, s_new.astype(jnp.float32), -jnp.inf)
    s = jnp.concatenate([s_past.astype(jnp.float32), s_new], -1) * MLA_SCALE
    p = jax.nn.softmax(s, axis=-1).astype(c_new.dtype)
    P = c_past.shape[1]
    return (jnp.einsum('bhts,bsr->bthr', p[..., :P], c_past)
            + jnp.einsum('bhtu,bur->bthr', p[..., P:], c_new))


def gla_prompt(q, k, v, loga):
    B, T = q.shape[:2]
    N = T // GLA_CHUNK

    def chunks(a):
        return a.reshape(B, N, GLA_CHUNK, a.shape[2], a.shape[3]).transpose(0, 3, 1, 2, 4).astype(jnp.float32)

    qf, kf, vf, g = chunks(q), chunks(k), chunks(v), chunks(loga)
    b = jnp.cumsum(g, axis=3)
    b_last = b[..., -1:, :]
    q_t = qf * jnp.exp(b)
    k_t = kf * jnp.exp(-b)
    k_h = kf * jnp.exp(b_last - b)
    tri = jnp.arange(GLA_CHUNK)[:, None] >= jnp.arange(GLA_CHUNK)[None, :]
    A = jnp.where(tri, jnp.einsum('bhnid,bhnjd->bhnij', q_t, k_t), 0.0)
    o_intra = jnp.einsum('bhnij,bhnjv->bhniv', A, vf)
    kv_chunk = jnp.einsum('bhnjd,bhnjv->bhndv', k_h, vf)
    decay = jnp.exp(b_last[..., 0, :])

    def step(S, inp):
        dec, kv = inp
        return dec[..., None] * S + kv, S

    S0 = jnp.zeros((B, GLA_HEADS, GLA_DK, GLA_DV), jnp.float32)
    S_fin, S_prev = lax.scan(step, S0, (jnp.moveaxis(decay, 2, 0), jnp.moveaxis(kv_chunk, 2, 0)))
    S_prev = jnp.moveaxis(S_prev, 0, 2)
    o = o_intra + jnp.einsum('bhnid,bhndv->bhniv', q_t, S_prev)
    o = o.transpose(0, 2, 3, 1, 4).reshape(B, T, GLA_HEADS, GLA_DV)
    return o, S_fin


def gla_recurrent(S0, q, k, v, loga):
    xs = tuple(jnp.moveaxis(a.astype(jnp.float32), 1, 0) for a in (q, k, v, loga))

    def step(S, inp):
        qt, kt, vt, gt = inp
        S = jnp.exp(gt)[..., None] * S + kt[..., :, None] * vt[..., None, :]
        return S, jnp.einsum('bhd,bhdv->bhv', qt, S)

    S_fin, o = lax.scan(step, S0.astype(jnp.float32), xs)
    return jnp.moveaxis(o, 0, 1), S_fin


def mixer_output(o_lat, gla_o, gg, w_uv, gla_norm, w_out, dtype):
    B, T = gg.shape[:2]
    mla = jnp.einsum('bthr,rhv->bthv', o_lat, w_uv).reshape(B, T, MLA_WIDTH)
    gla = rms_norm(gla_o.astype(dtype), gla_norm) * jax.nn.silu(gg.reshape(B, T, GLA_HEADS, GLA_DV))
    return jnp.concatenate([mla, gla.reshape(B, T, GLA_WIDTH)], -1) @ w_out


def post_block(x, mix, ln1_g, ln1_b, w1, w2, ln2_g, ln2_b):
    x = layer_norm(ALPHA * x + mix, ln1_g, ln1_b)
    h = jnp.square(jax.nn.relu(x @ w1)) @ w2
    return layer_norm(ALPHA * x + h, ln2_g, ln2_b)


def setup_inputs(seed: int = 0) -> dict:
    key = jax.random.key(seed)
    ks = jax.random.split(key, 24)
    n_pages = PAST_LEN // PAGE_SIZE
    n_used = DEC_BATCH * n_pages
    n_phys = n_used + n_used // 4
    nrm = jax.random.normal
    f32 = jnp.float32
    L = DEPTH
    page_table = jax.random.permutation(ks[5], n_phys)[:n_used].reshape(DEC_BATCH, n_pages).astype(jnp.int32)
    return {
        'x_prompt': nrm(ks[0], (BATCH, SEQ, D_MODEL), f32),
        'x_sample': nrm(ks[1], (DEC_BATCH, DEC_SEQ, D_MODEL), f32),
        'cache_ckv': nrm(ks[2], (L, n_phys, PAGE_SIZE, KV_RANK), f32),
        'cache_krope': nrm(ks[3], (L, n_phys, PAGE_SIZE, MLA_ROPE), f32),
        'state_gla': 0.5 * nrm(ks[4], (L, DEC_BATCH, GLA_HEADS, GLA_DK, GLA_DV), f32),
        'page_table': page_table,
        'w_in': nrm(ks[6], (L, D_MODEL, IN_COLS), f32) * D_MODEL ** -0.5,
        'mla_q_norm': 1.0 + 0.02 * nrm(ks[7], (L, Q_RANK), f32),
        'mla_w_uq': nrm(ks[8], (L, Q_RANK, MLA_HEADS * (MLA_NOPE + MLA_ROPE)), f32) * Q_RANK ** -0.5,
        'mla_kv_norm': 1.0 + 0.02 * nrm(ks[9], (L, KV_RANK), f32),
        'mla_w_uk': nrm(ks[10], (L, KV_RANK, MLA_HEADS, MLA_NOPE), f32) * KV_RANK ** -0.5,
        'mla_w_uv': nrm(ks[11], (L, KV_RANK, MLA_HEADS, MLA_V), f32) * (BETA * KV_RANK ** -0.5),
        'gla_w_gate2': nrm(ks[12], (L, GATE_RANK, GLA_HEADS * GLA_DK), f32) * GATE_RANK ** -0.5,
        'gla_b_gate': 0.1 * nrm(ks[13], (L, GLA_HEADS * GLA_DK), f32),
        'gla_norm': 1.0 + 0.02 * nrm(ks[14], (L, GLA_DV), f32),
        'w_out': nrm(ks[15], (L, MIX_WIDTH, D_MODEL), f32) * (BETA * MIX_WIDTH ** -0.5),
        'ln1_g': 1.0 + 0.02 * nrm(ks[16], (L, D_MODEL), f32),
        'ln1_b': 0.02 * nrm(ks[17], (L, D_MODEL), f32),
        'mlp_w1': nrm(ks[18], (L, D_MODEL, D_FF), f32) * (BETA * D_MODEL ** -0.5),
        'mlp_w2': nrm(ks[19], (L, D_FF, D_MODEL), f32) * (BETA * D_FF ** -0.5),
        'ln2_g': 1.0 + 0.02 * nrm(ks[20], (L, D_MODEL), f32),
        'ln2_b': 0.02 * nrm(ks[21], (L, D_MODEL), f32),
    }


def reference(x_prompt, x_sample, cache_ckv, cache_krope, state_gla, page_table,
              w_in, mla_q_norm, mla_w_uq, mla_kv_norm, mla_w_uk, mla_w_uv,
              gla_w_gate2, gla_b_gate, gla_norm, w_out, ln1_g, ln1_b,
              mlp_w1, mlp_w2, ln2_g, ln2_b):
    DB, T_new = x_sample.shape[:2]
    pos_p = jnp.arange(x_prompt.shape[1], dtype=jnp.int32)
    pos_s = PAST_LEN + jnp.arange(T_new, dtype=jnp.int32)
    xp, xs = x_prompt, x_sample
    ckv_p_l, kr_p_l, S_p_l, ckv_s_l, kr_s_l, S_s_l = [], [], [], [], [], []
    for l in range(DEPTH):
        proj = (w_in[l], mla_q_norm[l], mla_w_uq[l], mla_kv_norm[l], mla_w_uk[l],
                gla_w_gate2[l], gla_b_gate[l])
        q_lat, q_rope, ckv, krope, gq, gk, gv, ga, gg = mixer_inputs(xp, pos_p, *proj)
        o_lat = mla_prompt(q_lat, q_rope, ckv, krope)
        gla_o, S_p = gla_prompt(gq, gk, gv, ga)
        mix = mixer_output(o_lat, gla_o, gg, mla_w_uv[l], gla_norm[l], w_out[l], xp.dtype)
        xp = post_block(xp, mix, ln1_g[l], ln1_b[l], mlp_w1[l], mlp_w2[l], ln2_g[l], ln2_b[l])
        ckv_p_l.append(ckv)
        kr_p_l.append(krope)
        S_p_l.append(S_p.astype(state_gla.dtype))
        q_lat, q_rope, ckv, krope, gq, gk, gv, ga, gg = mixer_inputs(xs, pos_s, *proj)
        c_past = cache_ckv[l][page_table].reshape(DB, -1, KV_RANK)
        kr_past = cache_krope[l][page_table].reshape(DB, -1, MLA_ROPE)
        o_lat = mla_sample(q_lat, q_rope, c_past, kr_past, ckv, krope)
        gla_o, S_s = gla_recurrent(state_gla[l], gq, gk, gv, ga)
        mix = mixer_output(o_lat, gla_o, gg, mla_w_uv[l], gla_norm[l], w_out[l], xs.dtype)
        xs = post_block(xs, mix, ln1_g[l], ln1_b[l], mlp_w1[l], mlp_w2[l], ln2_g[l], ln2_b[l])
        ckv_s_l.append(ckv)
        kr_s_l.append(krope)
        S_s_l.append(S_s.astype(state_gla.dtype))
    return (xp, xs, jnp.stack(ckv_p_l), jnp.stack(kr_p_l), jnp.stack(S_p_l),
            jnp.stack(ckv_s_l), jnp.stack(kr_s_l), jnp.stack(S_s_l))
```

```python
import functools

import numpy as np
import jax
import jax.numpy as jnp
from jax import lax
from jax.experimental import pallas as pl
from jax.experimental.pallas import tpu as pltpu

D_MODEL = 1024
PAST_LEN = 16384
PAGE_SIZE = 128
MLA_HEADS = 4
MLA_V = 128
MLA_NOPE = 128
MLA_ROPE = 64
Q_RANK = 384
KV_RANK = 256
MLA_SCALE = (MLA_NOPE + MLA_ROPE) ** -0.5
ROPE_THETA = 10000.0
GLA_HEADS = 4
GLA_DV = 128
GLA_DK = 64
GLA_WIDTH = GLA_HEADS * GLA_DV
GLA_KW = GLA_HEADS * GLA_DK
GATE_RANK = 16
GATE_TAU = 16.0
GLA_CHUNK = 64
D_FF = 4 * D_MODEL
DEPTH = 1
ALPHA = (2.0 * DEPTH) ** 0.25
EPS = 1e-5

LANES = 128
QK_WIDTH = KV_RANK + LANES
VMEM_LIMIT = 56 * 1024 * 1024

_C_CQ = 0
_C_CKV = _C_CQ + Q_RANK
_C_GQ = _C_CKV + KV_RANK
_C_GK = _C_GQ + GLA_KW
_C_GV = _C_GK + GLA_KW
_C_GG = _C_GV + GLA_WIDTH
_C_KR = _C_GG + GLA_WIDTH
_C_GR = _C_KR + 2 * MLA_ROPE
IN_COLS_P = _C_GR + LANES

BF16 = jnp.bfloat16
F32 = jnp.float32


def _dot(a, b):
    return jnp.dot(a, b, preferred_element_type=F32)


def _dot_nt(a, b):
    return lax.dot_general(a, b, (((1,), (1,)), ((), ())), preferred_element_type=F32)


def _dot_tn(a, b):
    return lax.dot_general(a, b, (((0,), (0,)), ((), ())), preferred_element_type=F32)


def _rms(x, g):
    return x * lax.rsqrt(jnp.mean(x * x, axis=-1, keepdims=True) + EPS) * g


def _layer_norm(x, g, b):
    mu = jnp.mean(x, axis=-1, keepdims=True)
    xc = x - mu
    var = jnp.mean(xc * xc, axis=-1, keepdims=True)
    return xc * lax.rsqrt(var + EPS) * g + b


def _proj_kernel(x_ref, cos_ref, sin_ref, win_ref, qn_ref, wuq_ref, kvn_ref, wuk_ref,
                 wg2_ref, bg_ref,
                 q_ref, kcat_ref, ckv_ref, kr_ref, gq_ref, gk_ref, gv_ref, la_ref, gg_ref):
    xb = x_ref[...].astype(BF16)
    cos = cos_ref[...]
    sin = sin_ref[...]

    cq = _dot(xb, win_ref[:, _C_CQ:_C_CQ + Q_RANK])
    cqn = _rms(cq, qn_ref[...]).astype(BF16)
    q = _dot(cqn, wuq_ref[...])
    nh = MLA_HEADS * MLA_NOPE
    for h in range(MLA_HEADS):
        nope = q[:, h * MLA_NOPE:(h + 1) * MLA_NOPE].astype(BF16)
        q_lat = _dot(nope, wuk_ref[h])
        r = q[:, nh + h * LANES: nh + (h + 1) * LANES]
        r_sw = q[:, nh + (MLA_HEADS + h) * LANES: nh + (MLA_HEADS + h + 1) * LANES]
        q_rope = r * cos + r_sw * sin
        q_ref[h, :, 0:KV_RANK] = (q_lat * MLA_SCALE).astype(BF16)
        q_ref[h, :, KV_RANK:QK_WIDTH] = (q_rope * MLA_SCALE).astype(BF16)

    ckv = _rms(_dot(xb, win_ref[:, _C_CKV:_C_CKV + KV_RANK]), kvn_ref[...])
    ckv_ref[...] = ckv
    krr = _dot(xb, win_ref[:, _C_KR:_C_KR + 2 * MLA_ROPE])
    k_rope = (krr[:, :MLA_ROPE] * cos[:, :MLA_ROPE]
              + krr[:, MLA_ROPE:] * sin[:, :MLA_ROPE])
    kr_ref[...] = k_rope
    kcat_ref[:, 0:KV_RANK] = ckv.astype(BF16)
    kcat_ref[:, KV_RANK:QK_WIDTH] = jnp.concatenate(
        [k_rope, jnp.zeros_like(k_rope)], axis=-1).astype(BF16)

    gq_ref[...] = _dot(xb, win_ref[:, _C_GQ:_C_GQ + GLA_KW]) * (GLA_DK ** -0.5)
    gk_ref[...] = _dot(xb, win_ref[:, _C_GK:_C_GK + GLA_KW])
    gv_ref[...] = _dot(xb, win_ref[:, _C_GV:_C_GV + GLA_WIDTH]).astype(gv_ref.dtype)
    gg_ref[...] = _dot(xb, win_ref[:, _C_GG:_C_GG + GLA_WIDTH])
    gr = _dot(xb, win_ref[:, _C_GR:_C_GR + LANES]).astype(BF16)
    z = _dot(gr, wg2_ref[...]) + bg_ref[...]
    la_ref[...] = (jnp.minimum(z, 0.0) - jnp.log1p(jnp.exp(-jnp.abs(z)))) / GATE_TAU


def _proj(x2d, cos_t, sin_t, w, tm, gv_dtype):
    m = x2d.shape[0]
    grid = (m // tm,)
    row = lambda i: (i, 0)
    const2 = lambda i: (0, 0)
    const3 = lambda i: (0, 0, 0)
    in_specs = [
        pl.BlockSpec((tm, D_MODEL), row),
        pl.BlockSpec((tm, LANES), row),
        pl.BlockSpec((tm, LANES), row),
        pl.BlockSpec((D_MODEL, IN_COLS_P), const2),
        pl.BlockSpec((1, Q_RANK), const2),
        pl.BlockSpec(w['w_uq'].shape, const2),
        pl.BlockSpec((1, KV_RANK), const2),
        pl.BlockSpec((MLA_HEADS, MLA_NOPE, KV_RANK), const3),
        pl.BlockSpec((LANES, GLA_KW), const2),
        pl.BlockSpec((1, GLA_KW), const2),
    ]
    out_shape = [
        jax.ShapeDtypeStruct((MLA_HEADS, m, QK_WIDTH), BF16),
        jax.ShapeDtypeStruct((m, QK_WIDTH), BF16),
        jax.ShapeDtypeStruct((m, KV_RANK), F32),
        jax.ShapeDtypeStruct((m, MLA_ROPE), F32),
        jax.ShapeDtypeStruct((m, GLA_KW), F32),
        jax.ShapeDtypeStruct((m, GLA_KW), F32),
        jax.ShapeDtypeStruct((m, GLA_WIDTH), gv_dtype),
        jax.ShapeDtypeStruct((m, GLA_KW), F32),
        jax.ShapeDtypeStruct((m, GLA_WIDTH), F32),
    ]
    out_specs = [
        pl.BlockSpec((MLA_HEADS, tm, QK_WIDTH), lambda i: (0, i, 0)),
        pl.BlockSpec((tm, QK_WIDTH), row),
        pl.BlockSpec((tm, KV_RANK), row),
        pl.BlockSpec((tm, MLA_ROPE), row),
        pl.BlockSpec((tm, GLA_KW), row),
        pl.BlockSpec((tm, GLA_KW), row),
        pl.BlockSpec((tm, GLA_WIDTH), row),
        pl.BlockSpec((tm, GLA_KW), row),
        pl.BlockSpec((tm, GLA_WIDTH), row),
    ]
    return pl.pallas_call(
        _proj_kernel, grid=grid, in_specs=in_specs, out_specs=out_specs, out_shape=out_shape,
        compiler_params=pltpu.CompilerParams(
            dimension_semantics=("arbitrary",), vmem_limit_bytes=VMEM_LIMIT),
        name="proj",
    )(x2d, cos_t, sin_t, w['w_in'], w['q_norm'], w['w_uq'], w['kv_norm'], w['w_uk'],
      w['w_gate2'], w['b_gate'])


def _mla_prompt_kernel(q_ref, k_ref, wuv_ref, o_ref, m_scr, l_scr, acc_scr, *, tq, bk):
    i = pl.program_id(1)
    rows = MLA_HEADS * tq
    q = q_ref[...].reshape(rows, QK_WIDTH)
    m_scr[...] = jnp.full(m_scr.shape, -jnp.inf, F32)
    l_scr[...] = jnp.zeros(l_scr.shape, F32)
    acc_scr[...] = jnp.zeros(acc_scr.shape, F32)
    tok = i * tq + (lax.broadcasted_iota(jnp.int32, (rows, bk), 0) & (tq - 1))
    col = lax.broadcasted_iota(jnp.int32, (rows, bk), 1)
    n_blocks = (i * tq + tq - 1) // bk + 1

    def body(j, carry):
        start = pl.multiple_of(j * bk, bk)
        kblk = k_ref[pl.ds(start, bk), :]
        s = _dot_nt(q, kblk)
        s = jnp.where(col + j * bk <= tok, s, -jnp.inf)
        m_old = m_scr[...]
        m_new = jnp.maximum(m_old, jnp.max(s, axis=-1, keepdims=True))
        p = jnp.exp(s - m_new)
        alpha = jnp.exp(m_old - m_new)
        l_scr[...] = alpha * l_scr[...] + jnp.sum(p, axis=-1, keepdims=True)
        acc_scr[...] = alpha * acc_scr[...] + _dot(p.astype(BF16), kblk[:, :KV_RANK])
        m_scr[...] = m_new
        return carry

    lax.fori_loop(0, n_blocks, body, 0)
    o_lat = (acc_scr[...] / l_scr[...]).astype(BF16)
    for h in range(MLA_HEADS):
        o_ref[:, h * MLA_V:(h + 1) * MLA_V] = _dot(
            o_lat[h * tq:(h + 1) * tq], wuv_ref[h]).astype(o_ref.dtype)


def _mla_prompt(q, kcat, w_uv, batch, seq, tq=256, bk=512):
    nq = seq // tq
    kern = functools.partial(_mla_prompt_kernel, tq=tq, bk=bk)
    rows = MLA_HEADS * tq
    return pl.pallas_call(
        kern, grid=(batch, nq),
        in_specs=[
            pl.BlockSpec((MLA_HEADS, tq, QK_WIDTH), lambda b, i: (0, b * nq + i, 0)),
            pl.BlockSpec((seq, QK_WIDTH), lambda b, i: (b, 0)),
            pl.BlockSpec((MLA_HEADS, KV_RANK, MLA_V), lambda b, i: (0, 0, 0)),
        ],
        out_specs=pl.BlockSpec((tq, MLA_HEADS * MLA_V), lambda b, i: (b * nq + i, 0)),
        out_shape=jax.ShapeDtypeStruct((batch * seq, MLA_HEADS * MLA_V), BF16),
        scratch_shapes=[pltpu.VMEM((rows, 1), F32), pltpu.VMEM((rows, 1), F32),
                        pltpu.VMEM((rows, KV_RANK), F32)],
        compiler_params=pltpu.CompilerParams(
            dimension_semantics=("arbitrary", "arbitrary"), vmem_limit_bytes=VMEM_LIMIT),
        name="mla_prompt",
    )(q, kcat, w_uv)


SAMPLE_Q_ROWS = 16
PAGES_PER_CHUNK = 16


def _mla_sample_kernel(pt_ref, q_ref, knew_ref, wuv_ref, ckv_hbm, kr_hbm, o_ref,
                       cbuf, kbuf, sem_c, sem_k, *, n_batch, n_chunks):
    b = pl.program_id(0)
    g_pages = PAGES_PER_CHUNK
    rows = g_pages * PAGE_SIZE

    def page_copies(bb, c, slot, g):
        page = pt_ref[bb, c * g_pages + g]
        return (pltpu.make_async_copy(ckv_hbm.at[page], cbuf.at[slot, g], sem_c.at[slot]),
                pltpu.make_async_copy(kr_hbm.at[page], kbuf.at[slot, g], sem_k.at[slot]))

    def start_chunk(bb, c, slot):
        for g in range(g_pages):
            cc, ck = page_copies(bb, c, slot, g)
            cc.start()
            ck.start()

    def wait_chunk(bb, c, slot):
        for g in range(g_pages):
            cc, ck = page_copies(bb, c, slot, g)
            cc.wait()
            ck.wait()

    @pl.when(b == 0)
    def _():
        start_chunk(0, 0, 0)

    q = q_ref[0]
    q_lat = q[:, :KV_RANK]
    q_rope = q[:, KV_RANK:KV_RANK + MLA_ROPE]
    m = jnp.full((SAMPLE_Q_ROWS, 1), -jnp.inf, F32)
    l = jnp.zeros((SAMPLE_Q_ROWS, 1), F32)
    acc = jnp.zeros((SAMPLE_Q_ROWS, KV_RANK), F32)

    for c in range(n_chunks):
        slot = c % 2
        if c + 1 < n_chunks:
            start_chunk(b, c + 1, 1 - slot)
        else:
            @pl.when(b + 1 < n_batch)
            def _():
                start_chunk(b + 1, 0, 1 - slot)
        wait_chunk(b, c, slot)
        cb = cbuf[slot].reshape(rows, KV_RANK).astype(BF16)
        kb = kbuf[slot].reshape(rows, MLA_ROPE).astype(BF16)
        s = _dot_nt(q_lat, cb) + _dot_nt(q_rope, kb)
        m_new = jnp.maximum(m, jnp.max(s, axis=-1, keepdims=True))
        p = jnp.exp(s - m_new)
        alpha = jnp.exp(m - m_new)
        l = alpha * l + jnp.sum(p, axis=-1, keepdims=True)
        acc = alpha * acc + _dot(p.astype(BF16), cb)
        m = m_new

    knew = knew_ref[0].astype(F32)
    s_self = jnp.sum(q.astype(F32) * knew, axis=-1, keepdims=True)
    m_new = jnp.maximum(m, s_self)
    p_self = jnp.exp(s_self - m_new)
    alpha = jnp.exp(m - m_new)
    l = alpha * l + p_self
    acc = alpha * acc + p_self.astype(BF16).astype(F32) * knew[:, :KV_RANK]
    o_lat = (acc / l).astype(BF16)
    for h in range(MLA_HEADS):
        res = _dot(o_lat, wuv_ref[h])
        o_ref[0, :, h * MLA_V:(h + 1) * MLA_V] = res[h:h + 1, :]


def _mla_sample(page_table, q_s, knew, w_uv, cache_ckv, cache_krope):
    n_batch, n_pages = page_table.shape
    n_chunks = n_pages // PAGES_PER_CHUNK
    assert n_chunks * PAGES_PER_CHUNK == n_pages and n_chunks % 2 == 0
    kern = functools.partial(_mla_sample_kernel, n_batch=n_batch, n_chunks=n_chunks)
    grid_spec = pltpu.PrefetchScalarGridSpec(
        num_scalar_prefetch=1,
        grid=(n_batch,),
        in_specs=[
            pl.BlockSpec((1, SAMPLE_Q_ROWS, QK_WIDTH), lambda b, pt: (b, 0, 0)),
            pl.BlockSpec((1, 1, QK_WIDTH), lambda b, pt: (b, 0, 0)),
            pl.BlockSpec((MLA_HEADS, KV_RANK, MLA_V), lambda b, pt: (0, 0, 0)),
            pl.BlockSpec(memory_space=pl.ANY),
            pl.BlockSpec(memory_space=pl.ANY),
        ],
        out_specs=pl.BlockSpec((1, 1, MLA_HEADS * MLA_V), lambda b, pt: (b, 0, 0)),
        scratch_shapes=[
            pltpu.VMEM((2, PAGES_PER_CHUNK, PAGE_SIZE, KV_RANK), F32),
            pltpu.VMEM((2, PAGES_PER_CHUNK, PAGE_SIZE, MLA_ROPE), F32),
            pltpu.SemaphoreType.DMA((2,)),
            pltpu.SemaphoreType.DMA((2,)),
        ],
    )
    return pl.pallas_call(
        kern, grid_spec=grid_spec,
        out_shape=jax.ShapeDtypeStruct((n_batch, 1, MLA_HEADS * MLA_V), F32),
        compiler_params=pltpu.CompilerParams(
            dimension_semantics=("arbitrary",), vmem_limit_bytes=VMEM_LIMIT),
        name="mla_sample",
    )(page_table, q_s, knew, w_uv, cache_ckv, cache_krope)


def _split3(x):
    hi = x.astype(BF16)
    r1 = x - hi.astype(F32)
    mid = r1.astype(BF16)
    lo = (r1 - mid.astype(F32)).astype(BF16)
    return hi, mid, lo


def _gla_gate_out(o, gg, gnorm):
    return _rms(o, gnorm) * (gg * jax.nn.sigmoid(gg))


def _gla_prompt_kernel(gq_ref, gk_ref, gv_ref, la_ref, gg_ref, gn_ref, o_ref, s_out_ref,
                       s_scr, *, n_sub):
    t = pl.program_id(1)
    cs = GLA_CHUNK

    @pl.when(t == 0)
    def _():
        s_scr[...] = jnp.zeros(s_scr.shape, F32)

    ri = lax.broadcasted_iota(jnp.int32, (cs, cs), 0)
    ci = lax.broadcasted_iota(jnp.int32, (cs, cs), 1)
    tri = ri >= ci
    tri_b = tri.astype(BF16)
    lane_head = lax.broadcasted_iota(jnp.int32, (cs, GLA_KW), 1) // GLA_DK
    gnorm = gn_ref[...]

    def chunk(c, carry):
        r0 = pl.multiple_of(c * cs, cs)
        g = la_ref[pl.ds(r0, cs), :]
        g_hi, g_mid, g_lo = _split3(g)
        bcum = _dot(tri_b, g_hi) + _dot(tri_b, g_mid) + _dot(tri_b, g_lo)
        b_last = bcum[cs - 1:cs, :]
        qf = gq_ref[pl.ds(r0, cs), :]
        kf = gk_ref[pl.ds(r0, cs), :]
        v = gv_ref[pl.ds(r0, cs), :]
        q_t = qf * jnp.exp(bcum)
        k_t = (kf * jnp.exp(-bcum)).astype(BF16)
        k_h = (kf * jnp.exp(b_last - bcum)).astype(BF16)
        decay = jnp.exp(b_last)
        q_stack = jnp.concatenate(
            [jnp.where(lane_head == h, q_t, 0.0) for h in range(GLA_HEADS)],
            axis=0).astype(BF16)
        s_prev = s_scr[...]
        a_stack = _dot_nt(q_stack, k_t)
        o_inter = _dot(q_stack, s_prev.astype(BF16))
        kv = _dot_tn(k_h, v)
        decay_col = jnp.transpose(jnp.broadcast_to(decay, (GLA_DV, GLA_KW)))
        kv_diag = jnp.concatenate(
            [kv[h * GLA_DK:(h + 1) * GLA_DK, h * GLA_DV:(h + 1) * GLA_DV]
             for h in range(GLA_HEADS)], axis=0)
        s_scr[...] = decay_col * s_prev + kv_diag
        for h in range(GLA_HEADS):
            a_h = jnp.where(tri, a_stack[h * cs:(h + 1) * cs], 0.0).astype(BF16)
            o_h = _dot(a_h, v[:, h * GLA_DV:(h + 1) * GLA_DV]) + o_inter[h * cs:(h + 1) * cs]
            gg_h = gg_ref[pl.ds(r0, cs), h * GLA_DV:(h + 1) * GLA_DV]
            o_ref[pl.ds(r0, cs), h * GLA_DV:(h + 1) * GLA_DV] = _gla_gate_out(
                o_h, gg_h, gnorm).astype(o_ref.dtype)
        return carry

    lax.fori_loop(0, n_sub, chunk, 0)
    s_out_ref[0] = s_scr[...]


def _gla_prompt(gq, gk, gv, la, gg, gnorm, batch, seq, tc=1024):
    nt = seq // tc
    n_sub = tc // GLA_CHUNK
    row = lambda b, t: (b * nt + t, 0)
    kern = functools.partial(_gla_prompt_kernel, n_sub=n_sub)
    return pl.pallas_call(
        kern, grid=(batch, nt),
        in_specs=[
            pl.BlockSpec((tc, GLA_KW), row),
            pl.BlockSpec((tc, GLA_KW), row),
            pl.BlockSpec((tc, GLA_WIDTH), row),
            pl.BlockSpec((tc, GLA_KW), row),
            pl.BlockSpec((tc, GLA_WIDTH), row),
            pl.BlockSpec((1, GLA_DV), lambda b, t: (0, 0)),
        ],
        out_specs=[
            pl.BlockSpec((tc, GLA_WIDTH), row),
            pl.BlockSpec((1, GLA_KW, GLA_DV), lambda b, t: (b, 0, 0)),
        ],
        out_shape=[
            jax.ShapeDtypeStruct((batch * seq, GLA_WIDTH), BF16),
            jax.ShapeDtypeStruct((batch, GLA_KW, GLA_DV), F32),
        ],
        scratch_shapes=[pltpu.VMEM((GLA_KW, GLA_DV), F32)],
        compiler_params=pltpu.CompilerParams(
            dimension_semantics=("arbitrary", "arbitrary"), vmem_limit_bytes=VMEM_LIMIT),
        name="gla_prompt",
    )(gq, gk, gv, la, gg, gnorm)


def _gla_step_kernel(s_ref, gq_ref, gk_ref, gv_ref, la_ref, gg_ref, gn_ref, o_ref, s_out_ref,
                     *, bb):
    gnorm = gn_ref[...]

    def col(x_row):
        return jnp.transpose(jnp.broadcast_to(x_row, (GLA_DV, GLA_KW)))

    for i in range(bb):
        s = s_ref[i]
        e_col = col(jnp.exp(la_ref[i:i + 1, :]))
        k_col = col(gk_ref[i:i + 1, :])
        q_col = col(gq_ref[i:i + 1, :].astype(BF16).astype(F32))
        v_row = gv_ref[i:i + 1, :].astype(F32)
        v_rows = jnp.concatenate(
            [jnp.broadcast_to(v_row[:, h * GLA_DV:(h + 1) * GLA_DV], (GLA_DK, GLA_DV))
             for h in range(GLA_HEADS)], axis=0)
        s_new = e_col * s + k_col * v_rows
        s_out_ref[i] = s_new
        prod = q_col * s_new.astype(BF16).astype(F32)
        for h in range(GLA_HEADS):
            o_h = jnp.sum(prod[h * GLA_DK:(h + 1) * GLA_DK], axis=0, keepdims=True)
            gg_h = gg_ref[i:i + 1, h * GLA_DV:(h + 1) * GLA_DV]
            o_ref[i:i + 1, h * GLA_DV:(h + 1) * GLA_DV] = _gla_gate_out(o_h, gg_h, gnorm)


def _gla_step(state, gq, gk, gv, la, gg, gnorm, bb=8):
    n = state.shape[0]
    row = lambda i: (i, 0)
    kern = functools.partial(_gla_step_kernel, bb=bb)
    return pl.pallas_call(
        kern, grid=(n // bb,),
        in_specs=[
            pl.BlockSpec((bb, GLA_KW, GLA_DV), lambda i: (i, 0, 0)),
            pl.BlockSpec((bb, GLA_KW), row),
            pl.BlockSpec((bb, GLA_KW), row),
            pl.BlockSpec((bb, GLA_WIDTH), row),
            pl.BlockSpec((bb, GLA_KW), row),
            pl.BlockSpec((bb, GLA_WIDTH), row),
            pl.BlockSpec((1, GLA_DV), lambda i: (0, 0)),
        ],
        out_specs=[
            pl.BlockSpec((bb, GLA_WIDTH), row),
            pl.BlockSpec((bb, GLA_KW, GLA_DV), lambda i: (i, 0, 0)),
        ],
        out_shape=[
            jax.ShapeDtypeStruct((n, GLA_WIDTH), F32),
            jax.ShapeDtypeStruct((n, GLA_KW, GLA_DV), F32),
        ],
        compiler_params=pltpu.CompilerParams(
            dimension_semantics=("arbitrary",), vmem_limit_bytes=VMEM_LIMIT),
        name="gla_step",
    )(state, gq, gk, gv, la, gg, gnorm)


FF_CHUNK = 1024


def _post_kernel(x_ref, mla_ref, gla_ref, wout_ref, g1_ref, b1_ref, w1_ref, w2_ref,
                 g2_ref, b2_ref, y_ref):
    half = MLA_HEADS * MLA_V
    mix = (_dot(mla_ref[...].astype(BF16), wout_ref[0:half, :])
           + _dot(gla_ref[...].astype(BF16), wout_ref[half:, :]))
    x1 = _layer_norm(ALPHA * x_ref[...] + mix, g1_ref[...], b1_ref[...])
    x1b = x1.astype(BF16)
    acc = jnp.zeros(x1.shape, F32)
    for c in range(D_FF // FF_CHUNK):
        hmid = _dot(x1b, w1_ref[:, c * FF_CHUNK:(c + 1) * FF_CHUNK])
        hmid = jnp.square(jnp.maximum(hmid, 0.0)).astype(BF16)
        acc = acc + _dot(hmid, w2_ref[c * FF_CHUNK:(c + 1) * FF_CHUNK, :])
    y_ref[...] = _layer_norm(ALPHA * x1 + acc, g2_ref[...], b2_ref[...])


def _post(x2d, mla, gla, w, tm):
    m = x2d.shape[0]
    row = lambda i: (i, 0)
    const = lambda i: (0, 0)
    resident = dict(pipeline_mode=pl.Buffered(1))
    return pl.pallas_call(
        _post_kernel, grid=(m // tm,),
        in_specs=[
            pl.BlockSpec((tm, D_MODEL), row),
            pl.BlockSpec((tm, MLA_HEADS * MLA_V), row),
            pl.BlockSpec((tm, GLA_WIDTH), row),
            pl.BlockSpec((D_MODEL, D_MODEL), const, **resident),
            pl.BlockSpec((1, D_MODEL), const),
            pl.BlockSpec((1, D_MODEL), const),
            pl.BlockSpec((D_MODEL, D_FF), const, **resident),
            pl.BlockSpec((D_FF, D_MODEL), const, **resident),
            pl.BlockSpec((1, D_MODEL), const),
            pl.BlockSpec((1, D_MODEL), const),
        ],
        out_specs=pl.BlockSpec((tm, D_MODEL), row),
        out_shape=jax.ShapeDtypeStruct((m, D_MODEL), F32),
        compiler_params=pltpu.CompilerParams(
            dimension_semantics=("arbitrary",), vmem_limit_bytes=VMEM_LIMIT),
        name="post",
    )(x2d, mla, gla, w['w_out'], w['ln1_g'], w['ln1_b'], w['w1'], w['w2'],
      w['ln2_g'], w['ln2_b'])


def _permute_w_in(w):
    sizes = (Q_RANK, KV_RANK, MLA_ROPE, GLA_KW, GLA_KW, GLA_WIDTH, GATE_RANK, GLA_WIDTH)
    off = np.concatenate([[0], np.cumsum(sizes)]).tolist()
    cq, ckv, kr, gq, gk, gv, gr, gg = [w[:, off[i]:off[i + 1]] for i in range(8)]
    half = MLA_ROPE // 2
    pad = jnp.zeros((w.shape[0], LANES - GATE_RANK), w.dtype)
    return jnp.concatenate(
        [cq, ckv, gq, gk, gv, gg, kr, kr[:, half:], kr[:, :half], gr, pad], axis=1).astype(BF16)


def _permute_w_uq(w):
    per_head = MLA_NOPE + MLA_ROPE
    half = MLA_ROPE // 2
    pad = jnp.zeros((w.shape[0], LANES - MLA_ROPE), w.dtype)
    nope, rope, rope_sw = [], [], []
    for h in range(MLA_HEADS):
        base = h * per_head
        nope.append(w[:, base:base + MLA_NOPE])
        r = w[:, base + MLA_NOPE:base + per_head]
        rope += [r, pad]
        rope_sw += [r[:, half:], r[:, :half], pad]
    return jnp.concatenate(nope + rope + rope_sw, axis=1).astype(BF16)


def _prep_weights(w_in, mla_q_norm, mla_w_uq, mla_kv_norm, mla_w_uk, mla_w_uv,
                  gla_w_gate2, gla_b_gate, gla_norm, w_out, ln1_g, ln1_b,
                  mlp_w1, mlp_w2, ln2_g, ln2_b, l):
    w_in_p = _permute_w_in(w_in[l])
    w_uq_p = _permute_w_uq(mla_w_uq[l])
    return dict(
        w_in=w_in_p,
        q_norm=mla_q_norm[l][None, :],
        w_uq=w_uq_p,
        kv_norm=mla_kv_norm[l][None, :],
        w_uk=jnp.transpose(mla_w_uk[l], (1, 2, 0)).astype(BF16),
        w_uv=jnp.transpose(mla_w_uv[l], (1, 0, 2)).astype(BF16),
        w_gate2=jnp.pad(gla_w_gate2[l], ((0, LANES - GATE_RANK), (0, 0))).astype(BF16),
        b_gate=gla_b_gate[l][None, :],
        gla_norm=gla_norm[l][None, :],
        w_out=w_out[l].astype(BF16),
        ln1_g=ln1_g[l][None, :], ln1_b=ln1_b[l][None, :],
        w1=mlp_w1[l].astype(BF16), w2=mlp_w2[l].astype(BF16),
        ln2_g=ln2_g[l][None, :], ln2_b=ln2_b[l][None, :],
    )


def _rope_tables(pos):
    inv = ROPE_THETA ** (-jnp.arange(0, MLA_ROPE, 2, dtype=F32) / MLA_ROPE)
    ang = pos.astype(F32)[:, None] * inv[None, :]
    cos, sin = jnp.cos(ang), jnp.sin(ang)
    zeros = jnp.zeros((pos.shape[0], LANES - MLA_ROPE), F32)
    return (jnp.concatenate([cos, cos, zeros], axis=-1),
            jnp.concatenate([-sin, sin, zeros], axis=-1))


def kernel(x_prompt, x_sample, cache_ckv, cache_krope, state_gla, page_table, w_in,
           mla_q_norm, mla_w_uq, mla_kv_norm, mla_w_uk, mla_w_uv, gla_w_gate2, gla_b_gate,
           gla_norm, w_out, ln1_g, ln1_b, mlp_w1, mlp_w2, ln2_g, ln2_b):
    assert w_in.shape[0] == DEPTH == 1
    batch, seq, _ = x_prompt.shape
    n_dec, t_new, _ = x_sample.shape
    assert t_new == 1
    l = 0
    w = _prep_weights(w_in, mla_q_norm, mla_w_uq, mla_kv_norm, mla_w_uk, mla_w_uv,
                      gla_w_gate2, gla_b_gate, gla_norm, w_out, ln1_g, ln1_b,
                      mlp_w1, mlp_w2, ln2_g, ln2_b, l)

    xp = x_prompt.reshape(batch * seq, D_MODEL)
    cos_p, sin_p = _rope_tables(jnp.tile(jnp.arange(seq, dtype=jnp.int32), batch))
    q, kcat, ckv_p, kr_p, gq, gk, gv, la, gg = _proj(xp, cos_p, sin_p, w, 512, BF16)
    mla_p = _mla_prompt(q, kcat, w['w_uv'], batch, seq)
    gla_p, s_p = _gla_prompt(gq, gk, gv, la, gg, w['gla_norm'], batch, seq)
    y_p = _post(xp, mla_p, gla_p, w, tm=512)

    xs = x_sample.reshape(n_dec, D_MODEL)
    cos_s, sin_s = _rope_tables(jnp.full((n_dec,), PAST_LEN, dtype=jnp.int32))
    q, kcat, ckv_s, kr_s, gq, gk, gv, la, gg = _proj(xs, cos_s, sin_s, w, n_dec, F32)
    q_s = jnp.pad(jnp.transpose(q, (1, 0, 2)),
                  ((0, 0), (0, SAMPLE_Q_ROWS - MLA_HEADS), (0, 0)))
    mla_s = _mla_sample(page_table, q_s, kcat[:, None, :], w['w_uv'],
                        cache_ckv[l], cache_krope[l])
    gla_s, s_s = _gla_step(state_gla[l].reshape(n_dec, GLA_KW, GLA_DV),
                           gq, gk, gv, la, gg, w['gla_norm'])
    y_s = _post(xs, mla_s.reshape(n_dec, MLA_HEADS * MLA_V), gla_s, w, tm=n_dec)

    return (y_p.reshape(batch, seq, D_MODEL),
            y_s.reshape(n_dec, 1, D_MODEL),
            ckv_p.reshape(1, batch, seq, KV_RANK),
            kr_p.reshape(1, batch, seq, MLA_ROPE),
            s_p.reshape(1, batch, GLA_HEADS, GLA_DK, GLA_DV),
            ckv_s.reshape(1, n_dec, 1, KV_RANK),
            kr_s.reshape(1, n_dec, 1, MLA_ROPE),
            s_s.reshape(1, n_dec, GLA_HEADS, GLA_DK, GLA_DV))
```

```python
import functools

import numpy as np
import jax
import jax.numpy as jnp
from jax import lax
from jax.experimental import pallas as pl
from jax.experimental.pallas import tpu as pltpu

D_MODEL = 1024
PAST_LEN = 16384
PAGE_SIZE = 128
MLA_HEADS = 4
MLA_V = 128
MLA_NOPE = 128
MLA_ROPE = 64
Q_RANK = 384
KV_RANK = 256
MLA_SCALE = (MLA_NOPE + MLA_ROPE) ** -0.5
ROPE_THETA = 10000.0
GLA_HEADS = 4
GLA_DV = 128
GLA_DK = 64
GLA_WIDTH = GLA_HEADS * GLA_DV
GLA_KW = GLA_HEADS * GLA_DK
GATE_RANK = 16
GATE_TAU = 16.0
GLA_CHUNK = 64
D_FF = 4 * D_MODEL
DEPTH = 1
ALPHA = (2.0 * DEPTH) ** 0.25
EPS = 1e-5

LANES = 128
QK_WIDTH = KV_RANK + LANES
VMEM_LIMIT = 56 * 1024 * 1024
KEY_TILE = 256

_C_CQ = 0
_C_CKV = _C_CQ + Q_RANK
_C_GQ = _C_CKV + KV_RANK
_C_GK = _C_GQ + GLA_KW
_C_GV = _C_GK + GLA_KW
_C_GG = _C_GV + GLA_WIDTH
_C_KR = _C_GG + GLA_WIDTH
_C_GR = _C_KR + 2 * MLA_ROPE
IN_COLS_P = _C_GR + LANES

BF16 = jnp.bfloat16
F32 = jnp.float32


def _dot(a, b):
    return jnp.dot(a, b, preferred_element_type=F32)


def _dot_nt(a, b):
    return lax.dot_general(a, b, (((1,), (1,)), ((), ())), preferred_element_type=F32)


def _dot_tn(a, b):
    return lax.dot_general(a, b, (((0,), (0,)), ((), ())), preferred_element_type=F32)


def _rms(x, g):
    return x * lax.rsqrt(jnp.mean(x * x, axis=-1, keepdims=True) + EPS) * g


def _layer_norm(x, g, b):
    mu = jnp.mean(x, axis=-1, keepdims=True)
    xc = x - mu
    var = jnp.mean(xc * xc, axis=-1, keepdims=True)
    return xc * lax.rsqrt(var + EPS) * g + b


def _proj_kernel(x_ref, cos_ref, sin_ref, win_ref, qn_ref, wuq_ref, kvn_ref, wuk_ref,
                 wg2_ref, bg_ref,
                 q_ref, kcat_ref, ckv_ref, kr_ref, gq_ref, gk_ref, gv_ref, la_ref, gg_ref,
                 ckvt_ref=None):
    xb = x_ref[...].astype(BF16)
    cos = cos_ref[...]
    sin = sin_ref[...]

    cq = _dot(xb, win_ref[:, _C_CQ:_C_CQ + Q_RANK])
    cqn = _rms(cq, qn_ref[...]).astype(BF16)
    q = _dot(cqn, wuq_ref[...])
    nh = MLA_HEADS * MLA_NOPE
    for h in range(MLA_HEADS):
        nope = q[:, h * MLA_NOPE:(h + 1) * MLA_NOPE].astype(BF16)
        q_lat = _dot(nope, wuk_ref[h])
        r = q[:, nh + h * LANES: nh + (h + 1) * LANES]
        r_sw = q[:, nh + (MLA_HEADS + h) * LANES: nh + (MLA_HEADS + h + 1) * LANES]
        q_rope = r * cos + r_sw * sin
        q_ref[h, :, 0:KV_RANK] = (q_lat * MLA_SCALE).astype(BF16)
        q_ref[h, :, KV_RANK:QK_WIDTH] = (q_rope * MLA_SCALE).astype(BF16)

    ckv = _rms(_dot(xb, win_ref[:, _C_CKV:_C_CKV + KV_RANK]), kvn_ref[...])
    ckv_ref[...] = ckv
    krr = _dot(xb, win_ref[:, _C_KR:_C_KR + 2 * MLA_ROPE])
    k_rope = (krr[:, :MLA_ROPE] * cos[:, :MLA_ROPE]
              + krr[:, MLA_ROPE:] * sin[:, :MLA_ROPE])
    kr_ref[...] = k_rope
    kcat_ref[:, 0:KV_RANK] = ckv.astype(BF16)
    kcat_ref[:, KV_RANK:QK_WIDTH] = jnp.concatenate(
        [k_rope, jnp.zeros_like(k_rope)], axis=-1).astype(BF16)
    if ckvt_ref is not None:
        for n in range(ckvt_ref.shape[0]):
            ckvt_ref[n] = jnp.transpose(ckv[n * KEY_TILE:(n + 1) * KEY_TILE, :]).astype(BF16)

    gq_ref[...] = _dot(xb, win_ref[:, _C_GQ:_C_GQ + GLA_KW]) * (GLA_DK ** -0.5)
    gk_ref[...] = _dot(xb, win_ref[:, _C_GK:_C_GK + GLA_KW])
    gv_ref[...] = _dot(xb, win_ref[:, _C_GV:_C_GV + GLA_WIDTH]).astype(gv_ref.dtype)
    gg_ref[...] = _dot(xb, win_ref[:, _C_GG:_C_GG + GLA_WIDTH])
    gr = _dot(xb, win_ref[:, _C_GR:_C_GR + LANES]).astype(BF16)
    z = _dot(gr, wg2_ref[...]) + bg_ref[...]
    la_ref[...] = (jnp.minimum(z, 0.0) - jnp.log1p(jnp.exp(-jnp.abs(z)))) / GATE_TAU


def _proj(x2d, cos_t, sin_t, w, tm, gv_dtype, with_ckv_t):
    m = x2d.shape[0]
    grid = (m // tm,)
    row = lambda i: (i, 0)
    const2 = lambda i: (0, 0)
    const3 = lambda i: (0, 0, 0)
    const3_row = lambda i: (i, 0, 0)
    in_specs = [
        pl.BlockSpec((tm, D_MODEL), row),
        pl.BlockSpec((tm, LANES), row),
        pl.BlockSpec((tm, LANES), row),
        pl.BlockSpec((D_MODEL, IN_COLS_P), const2),
        pl.BlockSpec((1, Q_RANK), const2),
        pl.BlockSpec(w['w_uq'].shape, const2),
        pl.BlockSpec((1, KV_RANK), const2),
        pl.BlockSpec((MLA_HEADS, MLA_NOPE, KV_RANK), const3),
        pl.BlockSpec((LANES, GLA_KW), const2),
        pl.BlockSpec((1, GLA_KW), const2),
    ]
    out_shape = [
        jax.ShapeDtypeStruct((MLA_HEADS, m, QK_WIDTH), BF16),
        jax.ShapeDtypeStruct((m, QK_WIDTH), BF16),
        jax.ShapeDtypeStruct((m, KV_RANK), F32),
        jax.ShapeDtypeStruct((m, MLA_ROPE), F32),
        jax.ShapeDtypeStruct((m, GLA_KW), F32),
        jax.ShapeDtypeStruct((m, GLA_KW), F32),
        jax.ShapeDtypeStruct((m, GLA_WIDTH), gv_dtype),
        jax.ShapeDtypeStruct((m, GLA_KW), F32),
        jax.ShapeDtypeStruct((m, GLA_WIDTH), F32),
    ]
    out_specs = [
        pl.BlockSpec((MLA_HEADS, tm, QK_WIDTH), lambda i: (0, i, 0)),
        pl.BlockSpec((tm, QK_WIDTH), row),
        pl.BlockSpec((tm, KV_RANK), row),
        pl.BlockSpec((tm, MLA_ROPE), row),
        pl.BlockSpec((tm, GLA_KW), row),
        pl.BlockSpec((tm, GLA_KW), row),
        pl.BlockSpec((tm, GLA_WIDTH), row),
        pl.BlockSpec((tm, GLA_KW), row),
        pl.BlockSpec((tm, GLA_WIDTH), row),
    ]
    if with_ckv_t:
        out_shape.append(jax.ShapeDtypeStruct((m // KEY_TILE, KV_RANK, KEY_TILE), BF16))
        out_specs.append(pl.BlockSpec((tm // KEY_TILE, KV_RANK, KEY_TILE), const3_row))
    return pl.pallas_call(
        _proj_kernel, grid=grid, in_specs=in_specs, out_specs=out_specs, out_shape=out_shape,
        compiler_params=pltpu.CompilerParams(
            dimension_semantics=("arbitrary",), vmem_limit_bytes=VMEM_LIMIT),
        name="proj",
    )(x2d, cos_t, sin_t, w['w_in'], w['q_norm'], w['w_uq'], w['kv_norm'], w['w_uk'],
      w['w_gate2'], w['b_gate'])


def _mla_prompt_kernel(q_ref, k_ref, vt_ref, wuv_ref, o_ref, m_scr, l_scr, acc_scr, *, tq):
    i = pl.program_id(1)
    rows = MLA_HEADS * tq
    q = q_ref[...].reshape(rows, QK_WIDTH)
    m_scr[...] = jnp.full(m_scr.shape, -jnp.inf, F32)
    l_scr[...] = jnp.zeros(l_scr.shape, F32)
    acc_scr[...] = jnp.zeros(acc_scr.shape, F32)

    def step(tile0, n_tiles, masked):
        size = n_tiles * KEY_TILE
        start = pl.multiple_of(tile0 * KEY_TILE, KEY_TILE)
        st = _dot_nt(k_ref[pl.ds(start, size), :], q)
        if masked:
            key = start + lax.broadcasted_iota(jnp.int32, (size, rows), 0)
            tok = i * tq + (lax.broadcasted_iota(jnp.int32, (size, rows), 1) & (tq - 1))
            st = jnp.where(key <= tok, st, -jnp.inf)
        m_old = m_scr[...]
        m_new = jnp.maximum(m_old, jnp.max(st, axis=0, keepdims=True))
        p = jnp.exp(st - m_new)
        alpha = jnp.exp(m_old - m_new)
        l_scr[...] = alpha * l_scr[...] + jnp.sum(p, axis=0, keepdims=True)
        pb = p.astype(BF16)
        pv = _dot(vt_ref[tile0], pb[0:KEY_TILE])
        for t in range(1, n_tiles):
            pv = pv + _dot(vt_ref[tile0 + t], pb[t * KEY_TILE:(t + 1) * KEY_TILE])
        acc_scr[...] = alpha * acc_scr[...] + pv
        m_scr[...] = m_new

    def pair(j, carry):
        step(2 * j, 2, masked=False)
        return carry

    lax.fori_loop(0, i // 2, pair, 0)

    @pl.when(i % 2 == 1)
    def _():
        step(i - 1, 1, masked=False)

    step(i, 1, masked=True)

    o_lat_t = (acc_scr[...] / l_scr[...]).astype(BF16)
    for h in range(MLA_HEADS):
        o_ref[:, h * MLA_V:(h + 1) * MLA_V] = _dot_tn(
            o_lat_t[:, h * tq:(h + 1) * tq], wuv_ref[h]).astype(o_ref.dtype)


def _mla_prompt(q, kcat, ckv_t, w_uv, batch, seq):
    tq = KEY_TILE
    nq = seq // tq
    kern = functools.partial(_mla_prompt_kernel, tq=tq)
    rows = MLA_HEADS * tq
    return pl.pallas_call(
        kern, grid=(batch, nq),
        in_specs=[
            pl.BlockSpec((MLA_HEADS, tq, QK_WIDTH), lambda b, i: (0, b * nq + i, 0)),
            pl.BlockSpec((seq, QK_WIDTH), lambda b, i: (b, 0)),
            pl.BlockSpec((seq // KEY_TILE, KV_RANK, KEY_TILE), lambda b, i: (b, 0, 0)),
            pl.BlockSpec((MLA_HEADS, KV_RANK, MLA_V), lambda b, i: (0, 0, 0)),
        ],
        out_specs=pl.BlockSpec((tq, MLA_HEADS * MLA_V), lambda b, i: (b * nq + i, 0)),
        out_shape=jax.ShapeDtypeStruct((batch * seq, MLA_HEADS * MLA_V), BF16),
        scratch_shapes=[pltpu.VMEM((1, rows), F32), pltpu.VMEM((1, rows), F32),
                        pltpu.VMEM((KV_RANK, rows), F32)],
        compiler_params=pltpu.CompilerParams(
            dimension_semantics=("arbitrary", "arbitrary"), vmem_limit_bytes=VMEM_LIMIT),
        name="mla_prompt",
    )(q, kcat, ckv_t, w_uv)


SAMPLE_Q_ROWS = 16
PAGES_PER_CHUNK = 16


def _mla_sample_kernel(pt_ref, q_ref, knew_ref, wuv_ref, ckv_hbm, krt_hbm, o_ref,
                       cbuf, kbuf, cbf, sem_c, sem_k, *, n_batch, n_chunks):
    b = pl.program_id(0)
    g_pages = PAGES_PER_CHUNK

    def page_copies(bb, c, slot, g):
        page = pt_ref[bb, c * g_pages + g]
        tok = pl.ds(g * PAGE_SIZE, PAGE_SIZE)
        return (pltpu.make_async_copy(ckv_hbm.at[page], cbuf.at[slot, tok, :], sem_c.at[slot]),
                pltpu.make_async_copy(krt_hbm.at[page], kbuf.at[slot, :, tok], sem_k.at[slot]))

    def start_chunk(bb, c, slot):
        for g in range(g_pages):
            cc, ck = page_copies(bb, c, slot, g)
            cc.start()
            ck.start()

    def wait_chunk(bb, c, slot):
        for g in range(g_pages):
            cc, ck = page_copies(bb, c, slot, g)
            cc.wait()
            ck.wait()

    @pl.when(b == 0)
    def _():
        start_chunk(0, 0, 0)

    q = q_ref[0]
    q_lat = q[:, :KV_RANK]
    q_rope = q[:, KV_RANK:KV_RANK + MLA_ROPE]
    m = jnp.full((SAMPLE_Q_ROWS, 1), -jnp.inf, F32)
    l = jnp.zeros((SAMPLE_Q_ROWS, 1), F32)
    acc = jnp.zeros((SAMPLE_Q_ROWS, KV_RANK), F32)

    pending = None
    for c in range(n_chunks):
        slot = c % 2
        if c + 1 < n_chunks:
            start_chunk(b, c + 1, 1 - slot)
        else:
            @pl.when(b + 1 < n_batch)
            def _():
                start_chunk(b + 1, 0, 1 - slot)
        wait_chunk(b, c, slot)
        cb = cbuf[slot].astype(BF16)
        cbf[slot] = cb
        kb = kbuf[slot].astype(BF16)
        s = _dot_nt(q_lat, cb) + _dot(q_rope, kb)
        m_new = jnp.maximum(m, jnp.max(s, axis=-1, keepdims=True))
        p = jnp.exp(s - m_new)
        alpha = jnp.exp(m - m_new)
        l = alpha * l + jnp.sum(p, axis=-1, keepdims=True)
        m = m_new
        if pending is not None:
            alpha_p, p_p, slot_p = pending
            acc = alpha_p * acc + _dot(p_p, cbf[slot_p])
        pending = (alpha, p.astype(BF16), slot)
    alpha_p, p_p, slot_p = pending
    acc = alpha_p * acc + _dot(p_p, cbf[slot_p])

    knew = knew_ref[0].astype(F32)
    s_self = jnp.sum(q.astype(F32) * knew, axis=-1, keepdims=True)
    m_new = jnp.maximum(m, s_self)
    p_self = jnp.exp(s_self - m_new)
    alpha = jnp.exp(m - m_new)
    l = alpha * l + p_self
    acc = alpha * acc + p_self.astype(BF16).astype(F32) * knew[:, :KV_RANK]
    o_lat = (acc / l).astype(BF16)
    for h in range(MLA_HEADS):
        res = _dot(o_lat, wuv_ref[h])
        o_ref[0, :, h * MLA_V:(h + 1) * MLA_V] = res[h:h + 1, :]


def _mla_sample(page_table, q_s, knew, w_uv, cache_ckv, cache_krope_t):
    n_batch, n_pages = page_table.shape
    n_chunks = n_pages // PAGES_PER_CHUNK
    chunk_rows = PAGES_PER_CHUNK * PAGE_SIZE
    assert n_chunks * PAGES_PER_CHUNK == n_pages and n_chunks % 2 == 0
    kern = functools.partial(_mla_sample_kernel, n_batch=n_batch, n_chunks=n_chunks)
    grid_spec = pltpu.PrefetchScalarGridSpec(
        num_scalar_prefetch=1,
        grid=(n_batch,),
        in_specs=[
            pl.BlockSpec((1, SAMPLE_Q_ROWS, QK_WIDTH), lambda b, pt: (b, 0, 0)),
            pl.BlockSpec((1, 1, QK_WIDTH), lambda b, pt: (b, 0, 0)),
            pl.BlockSpec((MLA_HEADS, KV_RANK, MLA_V), lambda b, pt: (0, 0, 0)),
            pl.BlockSpec(memory_space=pl.ANY),
            pl.BlockSpec(memory_space=pl.ANY),
        ],
        out_specs=pl.BlockSpec((1, 1, MLA_HEADS * MLA_V), lambda b, pt: (b, 0, 0)),
        scratch_shapes=[
            pltpu.VMEM((2, chunk_rows, KV_RANK), F32),
            pltpu.VMEM((2, MLA_ROPE, chunk_rows), F32),
            pltpu.VMEM((2, chunk_rows, KV_RANK), BF16),
            pltpu.SemaphoreType.DMA((2,)),
            pltpu.SemaphoreType.DMA((2,)),
        ],
    )
    return pl.pallas_call(
        kern, grid_spec=grid_spec,
        out_shape=jax.ShapeDtypeStruct((n_batch, 1, MLA_HEADS * MLA_V), F32),
        compiler_params=pltpu.CompilerParams(
            dimension_semantics=("arbitrary",), vmem_limit_bytes=VMEM_LIMIT),
        name="mla_sample",
    )(page_table, q_s, knew, w_uv, cache_ckv, cache_krope_t)


def _split3(x):
    hi = x.astype(BF16)
    r1 = x - hi.astype(F32)
    mid = r1.astype(BF16)
    lo = (r1 - mid.astype(F32)).astype(BF16)
    return hi, mid, lo


def _gla_gate_out(o, gg, gnorm):
    return _rms(o, gnorm) * (gg * jax.nn.sigmoid(gg))


def _gla_prompt_kernel(gq_ref, gk_ref, gv_ref, la_ref, gg_ref, gn_ref, o_ref, s_out_ref,
                       s_scr, *, n_sub):
    t = pl.program_id(1)
    cs = GLA_CHUNK

    @pl.when(t == 0)
    def _():
        s_scr[...] = jnp.zeros(s_scr.shape, F32)

    ri = lax.broadcasted_iota(jnp.int32, (cs, cs), 0)
    ci = lax.broadcasted_iota(jnp.int32, (cs, cs), 1)
    tri = ri >= ci
    tri_b = tri.astype(BF16)
    lane_head = lax.broadcasted_iota(jnp.int32, (cs, GLA_KW), 1) // GLA_DK
    gnorm = gn_ref[...]

    def chunk(c, carry):
        r0 = pl.multiple_of(c * cs, cs)
        g = la_ref[pl.ds(r0, cs), :]
        g_hi, g_mid, g_lo = _split3(g)
        bcum = _dot(tri_b, g_hi) + _dot(tri_b, g_mid) + _dot(tri_b, g_lo)
        b_last = bcum[cs - 1:cs, :]
        qf = gq_ref[pl.ds(r0, cs), :]
        kf = gk_ref[pl.ds(r0, cs), :]
        v = gv_ref[pl.ds(r0, cs), :]
        q_t = qf * jnp.exp(bcum)
        k_t = (kf * jnp.exp(-bcum)).astype(BF16)
        k_h = (kf * jnp.exp(b_last - bcum)).astype(BF16)
        decay = jnp.exp(b_last)
        q_stack = jnp.concatenate(
            [jnp.where(lane_head == h, q_t, 0.0) for h in range(GLA_HEADS)],
            axis=0).astype(BF16)
        s_prev = s_scr[...]
        a_stack = _dot_nt(q_stack, k_t)
        o_inter = _dot(q_stack, s_prev.astype(BF16))
        kv = _dot_tn(k_h, v)
        decay_col = jnp.transpose(jnp.broadcast_to(decay, (GLA_DV, GLA_KW)))
        kv_diag = jnp.concatenate(
            [kv[h * GLA_DK:(h + 1) * GLA_DK, h * GLA_DV:(h + 1) * GLA_DV]
             for h in range(GLA_HEADS)], axis=0)
        s_scr[...] = decay_col * s_prev + kv_diag
        for h in range(GLA_HEADS):
            a_h = jnp.where(tri, a_stack[h * cs:(h + 1) * cs], 0.0).astype(BF16)
            o_h = _dot(a_h, v[:, h * GLA_DV:(h + 1) * GLA_DV]) + o_inter[h * cs:(h + 1) * cs]
            gg_h = gg_ref[pl.ds(r0, cs), h * GLA_DV:(h + 1) * GLA_DV]
            o_ref[pl.ds(r0, cs), h * GLA_DV:(h + 1) * GLA_DV] = _gla_gate_out(
                o_h, gg_h, gnorm).astype(o_ref.dtype)
        return carry

    lax.fori_loop(0, n_sub, chunk, 0)
    s_out_ref[0] = s_scr[...]


def _gla_prompt(gq, gk, gv, la, gg, gnorm, batch, seq, tc=1024):
    nt = seq // tc
    n_sub = tc // GLA_CHUNK
    row = lambda b, t: (b * nt + t, 0)
    kern = functools.partial(_gla_prompt_kernel, n_sub=n_sub)
    return pl.pallas_call(
        kern, grid=(batch, nt),
        in_specs=[
            pl.BlockSpec((tc, GLA_KW), row),
            pl.BlockSpec((tc, GLA_KW), row),
            pl.BlockSpec((tc, GLA_WIDTH), row),
            pl.BlockSpec((tc, GLA_KW), row),
            pl.BlockSpec((tc, GLA_WIDTH), row),
            pl.BlockSpec((1, GLA_DV), lambda b, t: (0, 0)),
        ],
        out_specs=[
            pl.BlockSpec((tc, GLA_WIDTH), row),
            pl.BlockSpec((1, GLA_KW, GLA_DV), lambda b, t: (b, 0, 0)),
        ],
        out_shape=[
            jax.ShapeDtypeStruct((batch * seq, GLA_WIDTH), BF16),
            jax.ShapeDtypeStruct((batch, GLA_KW, GLA_DV), F32),
        ],
        scratch_shapes=[pltpu.VMEM((GLA_KW, GLA_DV), F32)],
        compiler_params=pltpu.CompilerParams(
            dimension_semantics=("arbitrary", "arbitrary"), vmem_limit_bytes=VMEM_LIMIT),
        name="gla_prompt",
    )(gq, gk, gv, la, gg, gnorm)


def _gla_step_kernel(s_ref, gq_ref, gk_ref, gv_ref, la_ref, gg_ref, gn_ref, o_ref, s_out_ref,
                     *, bb):
    gnorm = gn_ref[...]

    def col(x_row):
        return jnp.transpose(jnp.broadcast_to(x_row, (GLA_DV, GLA_KW)))

    for i in range(bb):
        s = s_ref[i]
        e_col = col(jnp.exp(la_ref[i:i + 1, :]))
        k_col = col(gk_ref[i:i + 1, :])
        q_col = col(gq_ref[i:i + 1, :].astype(BF16).astype(F32))
        v_row = gv_ref[i:i + 1, :].astype(F32)
        v_rows = jnp.concatenate(
            [jnp.broadcast_to(v_row[:, h * GLA_DV:(h + 1) * GLA_DV], (GLA_DK, GLA_DV))
             for h in range(GLA_HEADS)], axis=0)
        s_new = e_col * s + k_col * v_rows
        s_out_ref[i] = s_new
        prod = q_col * s_new.astype(BF16).astype(F32)
        for h in range(GLA_HEADS):
            o_h = jnp.sum(prod[h * GLA_DK:(h + 1) * GLA_DK], axis=0, keepdims=True)
            gg_h = gg_ref[i:i + 1, h * GLA_DV:(h + 1) * GLA_DV]
            o_ref[i:i + 1, h * GLA_DV:(h + 1) * GLA_DV] = _gla_gate_out(o_h, gg_h, gnorm)


def _gla_step(state, gq, gk, gv, la, gg, gnorm, bb=8):
    n = state.shape[0]
    row = lambda i: (i, 0)
    kern = functools.partial(_gla_step_kernel, bb=bb)
    return pl.pallas_call(
        kern, grid=(n // bb,),
        in_specs=[
            pl.BlockSpec((bb, GLA_KW, GLA_DV), lambda i: (i, 0, 0)),
            pl.BlockSpec((bb, GLA_KW), row),
            pl.BlockSpec((bb, GLA_KW), row),
            pl.BlockSpec((bb, GLA_WIDTH), row),
            pl.BlockSpec((bb, GLA_KW), row),
            pl.BlockSpec((bb, GLA_WIDTH), row),
            pl.BlockSpec((1, GLA_DV), lambda i: (0, 0)),
        ],
        out_specs=[
            pl.BlockSpec((bb, GLA_WIDTH), row),
            pl.BlockSpec((bb, GLA_KW, GLA_DV), lambda i: (i, 0, 0)),
        ],
        out_shape=[
            jax.ShapeDtypeStruct((n, GLA_WIDTH), F32),
            jax.ShapeDtypeStruct((n, GLA_KW, GLA_DV), F32),
        ],
        compiler_params=pltpu.CompilerParams(
            dimension_semantics=("arbitrary",), vmem_limit_bytes=VMEM_LIMIT),
        name="gla_step",
    )(state, gq, gk, gv, la, gg, gnorm)


FF_CHUNK = 1024


def _post_kernel(x_ref, mla_ref, gla_ref, wout_ref, g1_ref, b1_ref, w1_ref, w2_ref,
                 g2_ref, b2_ref, y_ref):
    half = MLA_HEADS * MLA_V
    mix = (_dot(mla_ref[...].astype(BF16), wout_ref[0:half, :])
           + _dot(gla_ref[...].astype(BF16), wout_ref[half:, :]))
    x1 = _layer_norm(ALPHA * x_ref[...] + mix, g1_ref[...], b1_ref[...])
    x1b = x1.astype(BF16)
    acc = jnp.zeros(x1.shape, F32)
    for c in range(D_FF // FF_CHUNK):
        hmid = _dot(x1b, w1_ref[:, c * FF_CHUNK:(c + 1) * FF_CHUNK])
        hmid = jnp.square(jnp.maximum(hmid, 0.0)).astype(BF16)
        acc = acc + _dot(hmid, w2_ref[c * FF_CHUNK:(c + 1) * FF_CHUNK, :])
    y_ref[...] = _layer_norm(ALPHA * x1 + acc, g2_ref[...], b2_ref[...])


def _post(x2d, mla, gla, w, tm):
    m = x2d.shape[0]
    row = lambda i: (i, 0)
    const = lambda i: (0, 0)
    resident = dict(pipeline_mode=pl.Buffered(1))
    return pl.pallas_call(
        _post_kernel, grid=(m // tm,),
        in_specs=[
            pl.BlockSpec((tm, D_MODEL), row),
            pl.BlockSpec((tm, MLA_HEADS * MLA_V), row),
            pl.BlockSpec((tm, GLA_WIDTH), row),
            pl.BlockSpec((D_MODEL, D_MODEL), const, **resident),
            pl.BlockSpec((1, D_MODEL), const),
            pl.BlockSpec((1, D_MODEL), const),
            pl.BlockSpec((D_MODEL, D_FF), const, **resident),
            pl.BlockSpec((D_FF, D_MODEL), const, **resident),
            pl.BlockSpec((1, D_MODEL), const),
            pl.BlockSpec((1, D_MODEL), const),
        ],
        out_specs=pl.BlockSpec((tm, D_MODEL), row),
        out_shape=jax.ShapeDtypeStruct((m, D_MODEL), F32),
        compiler_params=pltpu.CompilerParams(
            dimension_semantics=("arbitrary",), vmem_limit_bytes=VMEM_LIMIT),
        name="post",
    )(x2d, mla, gla, w['w_out'], w['ln1_g'], w['ln1_b'], w['w1'], w['w2'],
      w['ln2_g'], w['ln2_b'])


def _permute_w_in(w):
    sizes = (Q_RANK, KV_RANK, MLA_ROPE, GLA_KW, GLA_KW, GLA_WIDTH, GATE_RANK, GLA_WIDTH)
    off = np.concatenate([[0], np.cumsum(sizes)]).tolist()
    cq, ckv, kr, gq, gk, gv, gr, gg = [w[:, off[i]:off[i + 1]] for i in range(8)]
    half = MLA_ROPE // 2
    pad = jnp.zeros((w.shape[0], LANES - GATE_RANK), w.dtype)
    return jnp.concatenate(
        [cq, ckv, gq, gk, gv, gg, kr, kr[:, half:], kr[:, :half], gr, pad], axis=1).astype(BF16)


def _permute_w_uq(w):
    per_head = MLA_NOPE + MLA_ROPE
    half = MLA_ROPE // 2
    pad = jnp.zeros((w.shape[0], LANES - MLA_ROPE), w.dtype)
    nope, rope, rope_sw = [], [], []
    for h in range(MLA_HEADS):
        base = h * per_head
        nope.append(w[:, base:base + MLA_NOPE])
        r = w[:, base + MLA_NOPE:base + per_head]
        rope += [r, pad]
        rope_sw += [r[:, half:], r[:, :half], pad]
    return jnp.concatenate(nope + rope + rope_sw, axis=1).astype(BF16)


def _prep_weights(w_in, mla_q_norm, mla_w_uq, mla_kv_norm, mla_w_uk, mla_w_uv,
                  gla_w_gate2, gla_b_gate, gla_norm, w_out, ln1_g, ln1_b,
                  mlp_w1, mlp_w2, ln2_g, ln2_b, l):
    w_in_p = _permute_w_in(w_in[l])
    w_uq_p = _permute_w_uq(mla_w_uq[l])
    return dict(
        w_in=w_in_p,
        q_norm=mla_q_norm[l][None, :],
        w_uq=w_uq_p,
        kv_norm=mla_kv_norm[l][None, :],
        w_uk=jnp.transpose(mla_w_uk[l], (1, 2, 0)).astype(BF16),
        w_uv=jnp.transpose(mla_w_uv[l], (1, 0, 2)).astype(BF16),
        w_gate2=jnp.pad(gla_w_gate2[l], ((0, LANES - GATE_RANK), (0, 0))).astype(BF16),
        b_gate=gla_b_gate[l][None, :],
        gla_norm=gla_norm[l][None, :],
        w_out=w_out[l].astype(BF16),
        ln1_g=ln1_g[l][None, :], ln1_b=ln1_b[l][None, :],
        w1=mlp_w1[l].astype(BF16), w2=mlp_w2[l].astype(BF16),
        ln2_g=ln2_g[l][None, :], ln2_b=ln2_b[l][None, :],
    )


def _rope_tables(pos):
    inv = ROPE_THETA ** (-jnp.arange(0, MLA_ROPE, 2, dtype=F32) / MLA_ROPE)
    ang = pos.astype(F32)[:, None] * inv[None, :]
    cos, sin = jnp.cos(ang), jnp.sin(ang)
    zeros = jnp.zeros((pos.shape[0], LANES - MLA_ROPE), F32)
    return (jnp.concatenate([cos, cos, zeros], axis=-1),
            jnp.concatenate([-sin, sin, zeros], axis=-1))


def kernel(x_prompt, x_sample, cache_ckv, cache_krope, state_gla, page_table, w_in,
           mla_q_norm, mla_w_uq, mla_kv_norm, mla_w_uk, mla_w_uv, gla_w_gate2, gla_b_gate,
           gla_norm, w_out, ln1_g, ln1_b, mlp_w1, mlp_w2, ln2_g, ln2_b):
    assert w_in.shape[0] == DEPTH == 1
    batch, seq, _ = x_prompt.shape
    n_dec, t_new, _ = x_sample.shape
    assert t_new == 1
    l = 0
    w = _prep_weights(w_in, mla_q_norm, mla_w_uq, mla_kv_norm, mla_w_uk, mla_w_uv,
                      gla_w_gate2, gla_b_gate, gla_norm, w_out, ln1_g, ln1_b,
                      mlp_w1, mlp_w2, ln2_g, ln2_b, l)

    xp = x_prompt.reshape(batch * seq, D_MODEL)
    cos_p, sin_p = _rope_tables(jnp.tile(jnp.arange(seq, dtype=jnp.int32), batch))
    q, kcat, ckv_p, kr_p, gq, gk, gv, la, gg, ckv_t = _proj(
        xp, cos_p, sin_p, w, 512, BF16, True)
    mla_p = _mla_prompt(q, kcat, ckv_t, w['w_uv'], batch, seq)
    gla_p, s_p = _gla_prompt(gq, gk, gv, la, gg, w['gla_norm'], batch, seq)
    y_p = _post(xp, mla_p, gla_p, w, tm=512)

    xs = x_sample.reshape(n_dec, D_MODEL)
    cos_s, sin_s = _rope_tables(jnp.full((n_dec,), PAST_LEN, dtype=jnp.int32))
    q, kcat, ckv_s, kr_s, gq, gk, gv, la, gg = _proj(xs, cos_s, sin_s, w, n_dec, F32, False)
    q_s = jnp.pad(jnp.transpose(q, (1, 0, 2)),
                  ((0, 0), (0, SAMPLE_Q_ROWS - MLA_HEADS), (0, 0)))
    krope_t = jnp.swapaxes(cache_krope[l], 1, 2)
    mla_s = _mla_sample(page_table, q_s, kcat[:, None, :], w['w_uv'],
                        cache_ckv[l], krope_t)
    gla_s, s_s = _gla_step(state_gla[l].reshape(n_dec, GLA_KW, GLA_DV),
                           gq, gk, gv, la, gg, w['gla_norm'])
    y_s = _post(xs, mla_s.reshape(n_dec, MLA_HEADS * MLA_V), gla_s, w, tm=n_dec)

    return (y_p.reshape(batch, seq, D_MODEL),
            y_s.reshape(n_dec, 1, D_MODEL),
            ckv_p.reshape(1, batch, seq, KV_RANK),
            kr_p.reshape(1, batch, seq, MLA_ROPE),
            s_p.reshape(1, batch, GLA_HEADS, GLA_DK, GLA_DV),
            ckv_s.reshape(1, n_dec, 1, KV_RANK),
            kr_s.reshape(1, n_dec, 1, MLA_ROPE),
            s_s.reshape(1, n_dec, GLA_HEADS, GLA_DK, GLA_DV))
```

```python
import functools

import numpy as np
import jax
import jax.numpy as jnp
from jax import lax
from jax.experimental import pallas as pl
from jax.experimental.pallas import tpu as pltpu

D_MODEL = 1024
PAST_LEN = 16384
PAGE_SIZE = 128
MLA_HEADS = 4
MLA_V = 128
MLA_NOPE = 128
MLA_ROPE = 64
Q_RANK = 384
KV_RANK = 256
MLA_SCALE = (MLA_NOPE + MLA_ROPE) ** -0.5
ROPE_THETA = 10000.0
GLA_HEADS = 4
GLA_DV = 128
GLA_DK = 64
GLA_WIDTH = GLA_HEADS * GLA_DV
GLA_KW = GLA_HEADS * GLA_DK
GATE_RANK = 16
GATE_TAU = 16.0
GLA_CHUNK = 64
D_FF = 4 * D_MODEL
DEPTH = 1
ALPHA = (2.0 * DEPTH) ** 0.25
EPS = 1e-5

LANES = 128
QK_WIDTH = KV_RANK + LANES
VMEM_LIMIT = 56 * 1024 * 1024
KEY_TILE = 256
STREAM_HEADS = 2

_C_CQ = 0
_C_CKV = _C_CQ + Q_RANK
_C_GQ = _C_CKV + KV_RANK
_C_GK = _C_GQ + GLA_KW
_C_GV = _C_GK + GLA_KW
_C_GG = _C_GV + GLA_WIDTH
_C_KR = _C_GG + GLA_WIDTH
_C_GR = _C_KR + 2 * MLA_ROPE
IN_COLS_P = _C_GR + LANES

BF16 = jnp.bfloat16
F32 = jnp.float32


def _dot(a, b):
    return jnp.dot(a, b, preferred_element_type=F32)


def _dot_nt(a, b):
    return lax.dot_general(a, b, (((1,), (1,)), ((), ())), preferred_element_type=F32)


def _dot_tn(a, b):
    return lax.dot_general(a, b, (((0,), (0,)), ((), ())), preferred_element_type=F32)


def _rms(x, g):
    return x * lax.rsqrt(jnp.mean(x * x, axis=-1, keepdims=True) + EPS) * g


def _layer_norm(x, g, b):
    mu = jnp.mean(x, axis=-1, keepdims=True)
    xc = x - mu
    var = jnp.mean(xc * xc, axis=-1, keepdims=True)
    return xc * lax.rsqrt(var + EPS) * g + b


def _proj_kernel(x_ref, cos_ref, sin_ref, win_ref, qn_ref, wuq_ref, kvn_ref, wuk_ref,
                 wg2_ref, bg_ref,
                 q_ref, kcat_ref, ckv_ref, kr_ref, gq_ref, gk_ref, gv_ref, la_ref, gg_ref,
                 ckvt_ref=None):
    xb = x_ref[...].astype(BF16)
    cos = cos_ref[...]
    sin = sin_ref[...]

    cq = _dot(xb, win_ref[:, _C_CQ:_C_CQ + Q_RANK])
    cqn = _rms(cq, qn_ref[...]).astype(BF16)
    q = _dot(cqn, wuq_ref[...])
    nh = MLA_HEADS * MLA_NOPE
    for h in range(MLA_HEADS):
        nope = q[:, h * MLA_NOPE:(h + 1) * MLA_NOPE].astype(BF16)
        q_lat = _dot(nope, wuk_ref[h])
        r = q[:, nh + h * LANES: nh + (h + 1) * LANES]
        r_sw = q[:, nh + (MLA_HEADS + h) * LANES: nh + (MLA_HEADS + h + 1) * LANES]
        q_rope = r * cos + r_sw * sin
        q_ref[h, :, 0:KV_RANK] = (q_lat * MLA_SCALE).astype(BF16)
        q_ref[h, :, KV_RANK:QK_WIDTH] = (q_rope * MLA_SCALE).astype(BF16)

    ckv = _rms(_dot(xb, win_ref[:, _C_CKV:_C_CKV + KV_RANK]), kvn_ref[...])
    ckv_ref[...] = ckv
    krr = _dot(xb, win_ref[:, _C_KR:_C_KR + 2 * MLA_ROPE])
    k_rope = (krr[:, :MLA_ROPE] * cos[:, :MLA_ROPE]
              + krr[:, MLA_ROPE:] * sin[:, :MLA_ROPE])
    kr_ref[...] = k_rope
    kcat_ref[:, 0:KV_RANK] = ckv.astype(BF16)
    kcat_ref[:, KV_RANK:QK_WIDTH] = jnp.concatenate(
        [k_rope, jnp.zeros_like(k_rope)], axis=-1).astype(BF16)
    if ckvt_ref is not None:
        for n in range(ckvt_ref.shape[0]):
            ckvt_ref[n] = jnp.transpose(ckv[n * KEY_TILE:(n + 1) * KEY_TILE, :]).astype(BF16)

    gq_ref[...] = _dot(xb, win_ref[:, _C_GQ:_C_GQ + GLA_KW]) * (GLA_DK ** -0.5)
    gk_ref[...] = _dot(xb, win_ref[:, _C_GK:_C_GK + GLA_KW])
    gv_ref[...] = _dot(xb, win_ref[:, _C_GV:_C_GV + GLA_WIDTH]).astype(gv_ref.dtype)
    gg_ref[...] = _dot(xb, win_ref[:, _C_GG:_C_GG + GLA_WIDTH])
    gr = _dot(xb, win_ref[:, _C_GR:_C_GR + LANES]).astype(BF16)
    z = _dot(gr, wg2_ref[...]) + bg_ref[...]
    la_ref[...] = (jnp.minimum(z, 0.0) - jnp.log1p(jnp.exp(-jnp.abs(z)))) / GATE_TAU


def _proj(x2d, cos_t, sin_t, w, tm, gv_dtype, with_ckv_t):
    m = x2d.shape[0]
    grid = (m // tm,)
    row = lambda i: (i, 0)
    const2 = lambda i: (0, 0)
    const3 = lambda i: (0, 0, 0)
    const3_row = lambda i: (i, 0, 0)
    pos_blocks = cos_t.shape[0] // tm
    pos_row = lambda i: (i % pos_blocks, 0)
    in_specs = [
        pl.BlockSpec((tm, D_MODEL), row),
        pl.BlockSpec((tm, LANES), pos_row),
        pl.BlockSpec((tm, LANES), pos_row),
        pl.BlockSpec((D_MODEL, IN_COLS_P), const2),
        pl.BlockSpec((1, Q_RANK), const2),
        pl.BlockSpec(w['w_uq'].shape, const2),
        pl.BlockSpec((1, KV_RANK), const2),
        pl.BlockSpec((MLA_HEADS, MLA_NOPE, KV_RANK), const3),
        pl.BlockSpec((LANES, GLA_KW), const2),
        pl.BlockSpec((1, GLA_KW), const2),
    ]
    out_shape = [
        jax.ShapeDtypeStruct((MLA_HEADS, m, QK_WIDTH), BF16),
        jax.ShapeDtypeStruct((m, QK_WIDTH), BF16),
        jax.ShapeDtypeStruct((m, KV_RANK), F32),
        jax.ShapeDtypeStruct((m, MLA_ROPE), F32),
        jax.ShapeDtypeStruct((m, GLA_KW), F32),
        jax.ShapeDtypeStruct((m, GLA_KW), F32),
        jax.ShapeDtypeStruct((m, GLA_WIDTH), gv_dtype),
        jax.ShapeDtypeStruct((m, GLA_KW), F32),
        jax.ShapeDtypeStruct((m, GLA_WIDTH), F32),
    ]
    out_specs = [
        pl.BlockSpec((MLA_HEADS, tm, QK_WIDTH), lambda i: (0, i, 0)),
        pl.BlockSpec((tm, QK_WIDTH), row),
        pl.BlockSpec((tm, KV_RANK), row),
        pl.BlockSpec((tm, MLA_ROPE), row),
        pl.BlockSpec((tm, GLA_KW), row),
        pl.BlockSpec((tm, GLA_KW), row),
        pl.BlockSpec((tm, GLA_WIDTH), row),
        pl.BlockSpec((tm, GLA_KW), row),
        pl.BlockSpec((tm, GLA_WIDTH), row),
    ]
    if with_ckv_t:
        out_shape.append(jax.ShapeDtypeStruct((m // KEY_TILE, KV_RANK, KEY_TILE), BF16))
        out_specs.append(pl.BlockSpec((tm // KEY_TILE, KV_RANK, KEY_TILE), const3_row))
    return pl.pallas_call(
        _proj_kernel, grid=grid, in_specs=in_specs, out_specs=out_specs, out_shape=out_shape,
        compiler_params=pltpu.CompilerParams(
            dimension_semantics=("arbitrary",), vmem_limit_bytes=VMEM_LIMIT),
        name="proj",
    )(x2d, cos_t, sin_t, w['w_in'], w['q_norm'], w['w_uq'], w['kv_norm'], w['w_uk'],
      w['w_gate2'], w['b_gate'])


def _mla_prompt_kernel(q_ref, k_ref, vt_ref, wuv_ref, o_ref, m_scr, l_scr, acc_scr, *, tq):
    i = pl.program_id(1)
    m_scr[...] = jnp.full(m_scr.shape, -jnp.inf, F32)
    l_scr[...] = jnp.zeros(l_scr.shape, F32)
    acc_scr[...] = jnp.zeros(acc_scr.shape, F32)

    def step(tile0, n_tiles, masked):
        size = n_tiles * KEY_TILE
        hw = STREAM_HEADS * tq
        start = pl.multiple_of(tile0 * KEY_TILE, KEY_TILE)
        kblk = k_ref[pl.ds(start, size), :]
        if masked:
            key = start + lax.broadcasted_iota(jnp.int32, (size, hw), 0)
            tok = i * tq + (lax.broadcasted_iota(jnp.int32, (size, hw), 1) & (tq - 1))
            visible = key <= tok
        m_all = m_scr[...]
        l_all = l_scr[...]
        m_out, l_out, updates = [], [], []
        n_streams = MLA_HEADS // STREAM_HEADS

        def scores(g):
            qg = q_ref[g * STREAM_HEADS:(g + 1) * STREAM_HEADS].reshape(hw, QK_WIDTH)
            return _dot_nt(kblk, qg)

        ahead = [scores(g) for g in range(n_streams)]
        for g in range(n_streams):
            lanes = slice(g * hw, (g + 1) * hw)
            st = ahead[g]
            if masked:
                st = jnp.where(visible, st, -jnp.inf)
            m_old = m_all[:, lanes]
            m_new = jnp.maximum(m_old, jnp.max(st, axis=0, keepdims=True))
            p = jnp.exp(st - m_new)
            alpha = jnp.exp(m_old - m_new)
            l_out.append(alpha * l_all[:, lanes] + jnp.sum(p, axis=0, keepdims=True))
            m_out.append(m_new)
            pb = p.astype(BF16)
            pv = _dot(vt_ref[tile0], pb[0:KEY_TILE])
            for t in range(1, n_tiles):
                pv = pv + _dot(vt_ref[tile0 + t], pb[t * KEY_TILE:(t + 1) * KEY_TILE])
            updates.append((lanes, alpha, pv))
        for lanes, alpha, pv in updates:
            acc_scr[:, lanes] = alpha * acc_scr[:, lanes] + pv
        m_scr[...] = jnp.concatenate(m_out, axis=1)
        l_scr[...] = jnp.concatenate(l_out, axis=1)

    def pair(j, carry):
        step(2 * j, 2, masked=False)
        return carry

    lax.fori_loop(0, i // 2, pair, 0)

    @pl.when(i % 2 == 1)
    def _():
        step(i - 1, 1, masked=False)

    step(i, 1, masked=True)

    o_lat_t = (acc_scr[...] / l_scr[...]).astype(BF16)
    for h in range(MLA_HEADS):
        o_ref[:, h * MLA_V:(h + 1) * MLA_V] = _dot_tn(
            o_lat_t[:, h * tq:(h + 1) * tq], wuv_ref[h]).astype(o_ref.dtype)


def _mla_prompt(q, kcat, ckv_t, w_uv, batch, seq):
    tq = KEY_TILE
    nq = seq // tq
    kern = functools.partial(_mla_prompt_kernel, tq=tq)
    rows = MLA_HEADS * tq
    return pl.pallas_call(
        kern, grid=(batch, nq),
        in_specs=[
            pl.BlockSpec((MLA_HEADS, tq, QK_WIDTH), lambda b, i: (0, b * nq + i, 0)),
            pl.BlockSpec((seq, QK_WIDTH), lambda b, i: (b, 0)),
            pl.BlockSpec((seq // KEY_TILE, KV_RANK, KEY_TILE), lambda b, i: (b, 0, 0)),
            pl.BlockSpec((MLA_HEADS, KV_RANK, MLA_V), lambda b, i: (0, 0, 0)),
        ],
        out_specs=pl.BlockSpec((tq, MLA_HEADS * MLA_V), lambda b, i: (b * nq + i, 0)),
        out_shape=jax.ShapeDtypeStruct((batch * seq, MLA_HEADS * MLA_V), BF16),
        scratch_shapes=[pltpu.VMEM((1, rows), F32), pltpu.VMEM((1, rows), F32),
                        pltpu.VMEM((KV_RANK, rows), F32)],
        compiler_params=pltpu.CompilerParams(
            dimension_semantics=("arbitrary", "arbitrary"), vmem_limit_bytes=VMEM_LIMIT),
        name="mla_prompt",
    )(q, kcat, ckv_t, w_uv)


SAMPLE_Q_ROWS = 16
PAGES_PER_CHUNK = 32
SAMPLE_SLOTS = 4
SAMPLE_PREFETCH = SAMPLE_SLOTS - 1


def _mla_sample_kernel(pt_ref, q_ref, knew_ref, wuv_ref, ckv_hbm, krt_hbm, o_ref,
                       cbuf, kbuf, cbf, sem_c, sem_k, *, n_batch, n_chunks):
    b = pl.program_id(0)
    g_pages = PAGES_PER_CHUNK

    def page_copies(bb, c, slot, g):
        page = pt_ref[bb, c * g_pages + g]
        tok = pl.ds(g * PAGE_SIZE, PAGE_SIZE)
        return (pltpu.make_async_copy(ckv_hbm.at[page], cbuf.at[slot, tok, :], sem_c.at[slot]),
                pltpu.make_async_copy(krt_hbm.at[page], kbuf.at[slot, :, tok], sem_k.at[slot]))

    def start_chunk(bb, c, slot):
        for g in range(g_pages):
            cc, ck = page_copies(bb, c, slot, g)
            cc.start()
            ck.start()

    def wait_chunk(bb, c, slot):
        for g in range(g_pages):
            cc, ck = page_copies(bb, c, slot, g)
            cc.wait()
            ck.wait()

    def start_ahead(c):
        cn = c + SAMPLE_PREFETCH
        if cn < n_chunks:
            start_chunk(b, cn, cn % SAMPLE_SLOTS)
        else:
            @pl.when(b + 1 < n_batch)
            def _():
                start_chunk(b + 1, cn - n_chunks, (cn - n_chunks) % SAMPLE_SLOTS)

    @pl.when(b == 0)
    def _():
        for c0 in range(SAMPLE_PREFETCH):
            start_chunk(0, c0, c0)

    q = q_ref[0]
    q_lat = q[:, :KV_RANK]
    q_rope = q[:, KV_RANK:KV_RANK + MLA_ROPE]
    m = jnp.full((SAMPLE_Q_ROWS, 1), -jnp.inf, F32)
    l = jnp.zeros((SAMPLE_Q_ROWS, 1), F32)
    acc = jnp.zeros((SAMPLE_Q_ROWS, KV_RANK), F32)

    n_tiles = g_pages * PAGE_SIZE // KEY_TILE

    def tiled_pv(p_b, cbf_slot):
        out = _dot(p_b[:, 0:KEY_TILE], cbf[cbf_slot, 0:KEY_TILE, :])
        for t in range(1, n_tiles):
            tok = slice(t * KEY_TILE, (t + 1) * KEY_TILE)
            out = out + _dot(p_b[:, tok], cbf[cbf_slot, tok, :])
        return out

    pending = None
    for c in range(n_chunks):
        slot = c % SAMPLE_SLOTS
        wait_chunk(b, c, slot)
        if pending is not None:
            alpha_p, p_p, slot_p = pending
            acc = alpha_p * acc + tiled_pv(p_p, slot_p)
        s_tiles = []
        for t in range(n_tiles):
            tok = slice(t * KEY_TILE, (t + 1) * KEY_TILE)
            cb = cbuf[slot, tok, :].astype(BF16)
            cbf[c % 2, tok, :] = cb
            kb = kbuf[slot, :, tok].astype(BF16)
            s_tiles.append(_dot_nt(q_lat, cb) + _dot(q_rope, kb))
        s = jnp.concatenate(s_tiles, axis=1)
        m_new = jnp.maximum(m, jnp.max(s, axis=-1, keepdims=True))
        p = jnp.exp(s - m_new)
        alpha = jnp.exp(m - m_new)
        l = alpha * l + jnp.sum(p, axis=-1, keepdims=True)
        m = m_new
        pending = (alpha, p.astype(BF16), c % 2)
        start_ahead(c)
    alpha_p, p_p, slot_p = pending
    acc = alpha_p * acc + tiled_pv(p_p, slot_p)

    knew = knew_ref[0].astype(F32)
    s_self = jnp.sum(q.astype(F32) * knew, axis=-1, keepdims=True)
    m_new = jnp.maximum(m, s_self)
    p_self = jnp.exp(s_self - m_new)
    alpha = jnp.exp(m - m_new)
    l = alpha * l + p_self
    acc = alpha * acc + p_self.astype(BF16).astype(F32) * knew[:, :KV_RANK]
    o_lat = (acc / l).astype(BF16)
    for h in range(MLA_HEADS):
        res = _dot(o_lat, wuv_ref[h])
        o_ref[0, :, h * MLA_V:(h + 1) * MLA_V] = res[h:h + 1, :]


def _mla_sample(page_table, q_s, knew, w_uv, cache_ckv, cache_krope_t):
    n_batch, n_pages = page_table.shape
    n_chunks = n_pages // PAGES_PER_CHUNK
    chunk_rows = PAGES_PER_CHUNK * PAGE_SIZE
    assert n_chunks * PAGES_PER_CHUNK == n_pages and n_chunks % SAMPLE_SLOTS == 0
    assert SAMPLE_PREFETCH <= n_chunks
    kern = functools.partial(_mla_sample_kernel, n_batch=n_batch, n_chunks=n_chunks)
    grid_spec = pltpu.PrefetchScalarGridSpec(
        num_scalar_prefetch=1,
        grid=(n_batch,),
        in_specs=[
            pl.BlockSpec((1, SAMPLE_Q_ROWS, QK_WIDTH), lambda b, pt: (b, 0, 0)),
            pl.BlockSpec((1, 1, QK_WIDTH), lambda b, pt: (b, 0, 0)),
            pl.BlockSpec((MLA_HEADS, KV_RANK, MLA_V), lambda b, pt: (0, 0, 0)),
            pl.BlockSpec(memory_space=pl.ANY),
            pl.BlockSpec(memory_space=pl.ANY),
        ],
        out_specs=pl.BlockSpec((1, 1, MLA_HEADS * MLA_V), lambda b, pt: (b, 0, 0)),
        scratch_shapes=[
            pltpu.VMEM((SAMPLE_SLOTS, chunk_rows, KV_RANK), F32),
            pltpu.VMEM((SAMPLE_SLOTS, MLA_ROPE, chunk_rows), F32),
            pltpu.VMEM((2, chunk_rows, KV_RANK), BF16),
            pltpu.SemaphoreType.DMA((SAMPLE_SLOTS,)),
            pltpu.SemaphoreType.DMA((SAMPLE_SLOTS,)),
        ],
    )
    return pl.pallas_call(
        kern, grid_spec=grid_spec,
        out_shape=jax.ShapeDtypeStruct((n_batch, 1, MLA_HEADS * MLA_V), F32),
        compiler_params=pltpu.CompilerParams(
            dimension_semantics=("arbitrary",), vmem_limit_bytes=VMEM_LIMIT),
        name="mla_sample",
    )(page_table, q_s, knew, w_uv, cache_ckv, cache_krope_t)


GLA_UNROLL = 4


def _split3(x):
    hi = x.astype(BF16)
    r1 = x - hi.astype(F32)
    mid = r1.astype(BF16)
    lo = (r1 - mid.astype(F32)).astype(BF16)
    return hi, mid, lo


def _gla_gate_out(o, gg, gnorm):
    return _rms(o, gnorm) * (gg * jax.nn.sigmoid(gg))


def _gla_prompt_kernel(gq_ref, gk_ref, gv_ref, la_ref, gg_ref, gn_ref, o_ref, s_out_ref,
                       s_scr, *, n_sub):
    t = pl.program_id(1)
    cs = GLA_CHUNK

    @pl.when(t == 0)
    def _():
        s_scr[...] = jnp.zeros(s_scr.shape, F32)

    ri = lax.broadcasted_iota(jnp.int32, (cs, cs), 0)
    ci = lax.broadcasted_iota(jnp.int32, (cs, cs), 1)
    tri = ri >= ci
    tri_b = tri.astype(BF16)
    lane_head = lax.broadcasted_iota(jnp.int32, (cs, GLA_KW), 1) // GLA_DK
    gnorm = gn_ref[...]

    def chunk(c, s_prev):
        r0 = pl.multiple_of(c * cs, cs)
        g = la_ref[pl.ds(r0, cs), :]
        g_hi, g_mid, g_lo = _split3(g)
        bcum = _dot(tri_b, g_hi) + _dot(tri_b, g_mid) + _dot(tri_b, g_lo)
        b_last = bcum[cs - 1:cs, :]
        qf = gq_ref[pl.ds(r0, cs), :]
        kf = gk_ref[pl.ds(r0, cs), :]
        v = gv_ref[pl.ds(r0, cs), :]
        q_t = qf * jnp.exp(bcum)
        k_t = (kf * jnp.exp(-bcum)).astype(BF16)
        k_h = (kf * jnp.exp(b_last - bcum)).astype(BF16)
        decay = jnp.exp(b_last)
        q_stack = jnp.concatenate(
            [jnp.where(lane_head == h, q_t, 0.0) for h in range(GLA_HEADS)],
            axis=0).astype(BF16)
        a_stack = _dot_nt(q_stack, k_t)
        o_inter = _dot(q_stack, s_prev.astype(BF16))
        kv = _dot_tn(k_h, v)
        decay_col = jnp.transpose(jnp.broadcast_to(decay, (GLA_DV, GLA_KW)))
        kv_diag = jnp.concatenate(
            [kv[h * GLA_DK:(h + 1) * GLA_DK, h * GLA_DV:(h + 1) * GLA_DV]
             for h in range(GLA_HEADS)], axis=0)
        s_new = decay_col * s_prev + kv_diag
        for h in range(GLA_HEADS):
            a_h = jnp.where(tri, a_stack[h * cs:(h + 1) * cs], 0.0).astype(BF16)
            o_h = _dot(a_h, v[:, h * GLA_DV:(h + 1) * GLA_DV]) + o_inter[h * cs:(h + 1) * cs]
            gg_h = gg_ref[pl.ds(r0, cs), h * GLA_DV:(h + 1) * GLA_DV]
            o_ref[pl.ds(r0, cs), h * GLA_DV:(h + 1) * GLA_DV] = _gla_gate_out(
                o_h, gg_h, gnorm).astype(o_ref.dtype)
        return s_new

    def trip(j, carry):
        s = s_scr[...]
        for u in range(GLA_UNROLL):
            s = chunk(j * GLA_UNROLL + u, s)
        s_scr[...] = s
        return carry

    lax.fori_loop(0, n_sub // GLA_UNROLL, trip, 0)
    s_out_ref[0] = s_scr[...]


def _gla_prompt(gq, gk, gv, la, gg, gnorm, batch, seq, tc=1024):
    nt = seq // tc
    n_sub = tc // GLA_CHUNK
    row = lambda b, t: (b * nt + t, 0)
    kern = functools.partial(_gla_prompt_kernel, n_sub=n_sub)
    return pl.pallas_call(
        kern, grid=(batch, nt),
        in_specs=[
            pl.BlockSpec((tc, GLA_KW), row),
            pl.BlockSpec((tc, GLA_KW), row),
            pl.BlockSpec((tc, GLA_WIDTH), row),
            pl.BlockSpec((tc, GLA_KW), row),
            pl.BlockSpec((tc, GLA_WIDTH), row),
            pl.BlockSpec((1, GLA_DV), lambda b, t: (0, 0)),
        ],
        out_specs=[
            pl.BlockSpec((tc, GLA_WIDTH), row),
            pl.BlockSpec((1, GLA_KW, GLA_DV), lambda b, t: (b, 0, 0)),
        ],
        out_shape=[
            jax.ShapeDtypeStruct((batch * seq, GLA_WIDTH), BF16),
            jax.ShapeDtypeStruct((batch, GLA_KW, GLA_DV), F32),
        ],
        scratch_shapes=[pltpu.VMEM((GLA_KW, GLA_DV), F32)],
        compiler_params=pltpu.CompilerParams(
            dimension_semantics=("arbitrary", "arbitrary"), vmem_limit_bytes=VMEM_LIMIT),
        name="gla_prompt",
    )(gq, gk, gv, la, gg, gnorm)


def _gla_step_kernel(s_ref, gq_ref, gk_ref, gv_ref, la_ref, gg_ref, gn_ref, o_ref, s_out_ref,
                     *, bb):
    gnorm = gn_ref[...]

    def col(x_row):
        return jnp.transpose(jnp.broadcast_to(x_row, (GLA_DV, GLA_KW)))

    for i in range(bb):
        s = s_ref[i]
        e_col = col(jnp.exp(la_ref[i:i + 1, :]))
        k_col = col(gk_ref[i:i + 1, :])
        q_col = col(gq_ref[i:i + 1, :].astype(BF16).astype(F32))
        v_row = gv_ref[i:i + 1, :].astype(F32)
        v_rows = jnp.concatenate(
            [jnp.broadcast_to(v_row[:, h * GLA_DV:(h + 1) * GLA_DV], (GLA_DK, GLA_DV))
             for h in range(GLA_HEADS)], axis=0)
        s_new = e_col * s + k_col * v_rows
        s_out_ref[i] = s_new
        prod = q_col * s_new.astype(BF16).astype(F32)
        for h in range(GLA_HEADS):
            o_h = jnp.sum(prod[h * GLA_DK:(h + 1) * GLA_DK], axis=0, keepdims=True)
            gg_h = gg_ref[i:i + 1, h * GLA_DV:(h + 1) * GLA_DV]
            o_ref[i:i + 1, h * GLA_DV:(h + 1) * GLA_DV] = _gla_gate_out(o_h, gg_h, gnorm)


def _gla_step(state, gq, gk, gv, la, gg, gnorm, bb=8):
    n = state.shape[0]
    row = lambda i: (i, 0)
    kern = functools.partial(_gla_step_kernel, bb=bb)
    return pl.pallas_call(
        kern, grid=(n // bb,),
        in_specs=[
            pl.BlockSpec((bb, GLA_KW, GLA_DV), lambda i: (i, 0, 0)),
            pl.BlockSpec((bb, GLA_KW), row),
            pl.BlockSpec((bb, GLA_KW), row),
            pl.BlockSpec((bb, GLA_WIDTH), row),
            pl.BlockSpec((bb, GLA_KW), row),
            pl.BlockSpec((bb, GLA_WIDTH), row),
            pl.BlockSpec((1, GLA_DV), lambda i: (0, 0)),
        ],
        out_specs=[
            pl.BlockSpec((bb, GLA_WIDTH), row),
            pl.BlockSpec((bb, GLA_KW, GLA_DV), lambda i: (i, 0, 0)),
        ],
        out_shape=[
            jax.ShapeDtypeStruct((n, GLA_WIDTH), F32),
            jax.ShapeDtypeStruct((n, GLA_KW, GLA_DV), F32),
        ],
        compiler_params=pltpu.CompilerParams(
            dimension_semantics=("arbitrary",), vmem_limit_bytes=VMEM_LIMIT),
        name="gla_step",
    )(state, gq, gk, gv, la, gg, gnorm)


FF_CHUNK = 1024


def _post_kernel(x_ref, mla_ref, gla_ref, wout_ref, g1_ref, b1_ref, w1_ref, w2_ref,
                 g2_ref, b2_ref, y_ref):
    half = MLA_HEADS * MLA_V
    mix = (_dot(mla_ref[...].astype(BF16), wout_ref[0:half, :])
           + _dot(gla_ref[...].astype(BF16), wout_ref[half:, :]))
    x1 = _layer_norm(ALPHA * x_ref[...] + mix, g1_ref[...], b1_ref[...])
    x1b = x1.astype(BF16)
    acc = jnp.zeros(x1.shape, F32)
    for c in range(D_FF // FF_CHUNK):
        hmid = _dot(x1b, w1_ref[:, c * FF_CHUNK:(c + 1) * FF_CHUNK])
        hmid = jnp.square(jnp.maximum(hmid, 0.0)).astype(BF16)
        acc = acc + _dot(hmid, w2_ref[c * FF_CHUNK:(c + 1) * FF_CHUNK, :])
    y_ref[...] = _layer_norm(ALPHA * x1 + acc, g2_ref[...], b2_ref[...])


def _post(x2d, mla, gla, w, tm):
    m = x2d.shape[0]
    row = lambda i: (i, 0)
    const = lambda i: (0, 0)
    resident = dict(pipeline_mode=pl.Buffered(1))
    return pl.pallas_call(
        _post_kernel, grid=(m // tm,),
        in_specs=[
            pl.BlockSpec((tm, D_MODEL), row),
            pl.BlockSpec((tm, MLA_HEADS * MLA_V), row),
            pl.BlockSpec((tm, GLA_WIDTH), row),
            pl.BlockSpec((D_MODEL, D_MODEL), const, **resident),
            pl.BlockSpec((1, D_MODEL), const),
            pl.BlockSpec((1, D_MODEL), const),
            pl.BlockSpec((D_MODEL, D_FF), const, **resident),
            pl.BlockSpec((D_FF, D_MODEL), const, **resident),
            pl.BlockSpec((1, D_MODEL), const),
            pl.BlockSpec((1, D_MODEL), const),
        ],
        out_specs=pl.BlockSpec((tm, D_MODEL), row),
        out_shape=jax.ShapeDtypeStruct((m, D_MODEL), F32),
        compiler_params=pltpu.CompilerParams(
            dimension_semantics=("arbitrary",), vmem_limit_bytes=VMEM_LIMIT),
        name="post",
    )(x2d, mla, gla, w['w_out'], w['ln1_g'], w['ln1_b'], w['w1'], w['w2'],
      w['ln2_g'], w['ln2_b'])


def _permute_w_in(w):
    sizes = (Q_RANK, KV_RANK, MLA_ROPE, GLA_KW, GLA_KW, GLA_WIDTH, GATE_RANK, GLA_WIDTH)
    off = np.concatenate([[0], np.cumsum(sizes)]).tolist()
    cq, ckv, kr, gq, gk, gv, gr, gg = [w[:, off[i]:off[i + 1]] for i in range(8)]
    half = MLA_ROPE // 2
    pad = jnp.zeros((w.shape[0], LANES - GATE_RANK), w.dtype)
    return jnp.concatenate(
        [cq, ckv, gq, gk, gv, gg, kr, kr[:, half:], kr[:, :half], gr, pad], axis=1).astype(BF16)


def _permute_w_uq(w):
    per_head = MLA_NOPE + MLA_ROPE
    half = MLA_ROPE // 2
    pad = jnp.zeros((w.shape[0], LANES - MLA_ROPE), w.dtype)
    nope, rope, rope_sw = [], [], []
    for h in range(MLA_HEADS):
        base = h * per_head
        nope.append(w[:, base:base + MLA_NOPE])
        r = w[:, base + MLA_NOPE:base + per_head]
        rope += [r, pad]
        rope_sw += [r[:, half:], r[:, :half], pad]
    return jnp.concatenate(nope + rope + rope_sw, axis=1).astype(BF16)


def _prep_weights(w_in, mla_q_norm, mla_w_uq, mla_kv_norm, mla_w_uk, mla_w_uv,
                  gla_w_gate2, gla_b_gate, gla_norm, w_out, ln1_g, ln1_b,
                  mlp_w1, mlp_w2, ln2_g, ln2_b, l):
    w_in_p = _permute_w_in(w_in[l])
    w_uq_p = _permute_w_uq(mla_w_uq[l])
    return dict(
        w_in=w_in_p,
        q_norm=mla_q_norm[l][None, :],
        w_uq=w_uq_p,
        kv_norm=mla_kv_norm[l][None, :],
        w_uk=jnp.transpose(mla_w_uk[l], (1, 2, 0)).astype(BF16),
        w_uv=jnp.transpose(mla_w_uv[l], (1, 0, 2)).astype(BF16),
        w_gate2=jnp.pad(gla_w_gate2[l], ((0, LANES - GATE_RANK), (0, 0))).astype(BF16),
        b_gate=gla_b_gate[l][None, :],
        gla_norm=gla_norm[l][None, :],
        w_out=w_out[l].astype(BF16),
        ln1_g=ln1_g[l][None, :], ln1_b=ln1_b[l][None, :],
        w1=mlp_w1[l].astype(BF16), w2=mlp_w2[l].astype(BF16),
        ln2_g=ln2_g[l][None, :], ln2_b=ln2_b[l][None, :],
    )


def _rope_tables(pos):
    inv = ROPE_THETA ** (-jnp.arange(0, MLA_ROPE, 2, dtype=F32) / MLA_ROPE)
    ang = pos.astype(F32)[:, None] * inv[None, :]
    cos, sin = jnp.cos(ang), jnp.sin(ang)
    zeros = jnp.zeros((pos.shape[0], LANES - MLA_ROPE), F32)
    return (jnp.concatenate([cos, cos, zeros], axis=-1),
            jnp.concatenate([-sin, sin, zeros], axis=-1))


def kernel(x_prompt, x_sample, cache_ckv, cache_krope, state_gla, page_table, w_in,
           mla_q_norm, mla_w_uq, mla_kv_norm, mla_w_uk, mla_w_uv, gla_w_gate2, gla_b_gate,
           gla_norm, w_out, ln1_g, ln1_b, mlp_w1, mlp_w2, ln2_g, ln2_b):
    assert w_in.shape[0] == DEPTH == 1
    batch, seq, _ = x_prompt.shape
    n_dec, t_new, _ = x_sample.shape
    assert t_new == 1
    l = 0
    w = _prep_weights(w_in, mla_q_norm, mla_w_uq, mla_kv_norm, mla_w_uk, mla_w_uv,
                      gla_w_gate2, gla_b_gate, gla_norm, w_out, ln1_g, ln1_b,
                      mlp_w1, mlp_w2, ln2_g, ln2_b, l)

    xp = x_prompt.reshape(batch * seq, D_MODEL)
    cos_p, sin_p = _rope_tables(jnp.arange(seq, dtype=jnp.int32))
    q, kcat, ckv_p, kr_p, gq, gk, gv, la, gg, ckv_t = _proj(
        xp, cos_p, sin_p, w, 512, BF16, True)
    mla_p = _mla_prompt(q, kcat, ckv_t, w['w_uv'], batch, seq)
    gla_p, s_p = _gla_prompt(gq, gk, gv, la, gg, w['gla_norm'], batch, seq)
    y_p = _post(xp, mla_p, gla_p, w, tm=512)

    xs = x_sample.reshape(n_dec, D_MODEL)
    cos_s, sin_s = _rope_tables(jnp.full((n_dec,), PAST_LEN, dtype=jnp.int32))
    q, kcat, ckv_s, kr_s, gq, gk, gv, la, gg = _proj(xs, cos_s, sin_s, w, n_dec, F32, False)
    q_s = jnp.pad(jnp.transpose(q, (1, 0, 2)),
                  ((0, 0), (0, SAMPLE_Q_ROWS - MLA_HEADS), (0, 0)))
    krope_t = jnp.swapaxes(cache_krope[l], 1, 2)
    mla_s = _mla_sample(page_table, q_s, kcat[:, None, :], w['w_uv'],
                        cache_ckv[l], krope_t)
    gla_s, s_s = _gla_step(state_gla[l].reshape(n_dec, GLA_KW, GLA_DV),
                           gq, gk, gv, la, gg, w['gla_norm'])
    y_s = _post(xs, mla_s.reshape(n_dec, MLA_HEADS * MLA_V), gla_s, w, tm=n_dec)

    return (y_p.reshape(batch, seq, D_MODEL),
            y_s.reshape(n_dec, 1, D_MODEL),
            ckv_p.reshape(1, batch, seq, KV_RANK),
            kr_p.reshape(1, batch, seq, MLA_ROPE),
            s_p.reshape(1, batch, GLA_HEADS, GLA_DK, GLA_DV),
            ckv_s.reshape(1, n_dec, 1, KV_RANK),
            kr_s.reshape(1, n_dec, 1, MLA_ROPE),
            s_s.reshape(1, n_dec, GLA_HEADS, GLA_DK, GLA_DV))
```

```python
import functools

import numpy as np
import jax
import jax.numpy as jnp
from jax import lax
from jax.experimental import pallas as pl
from jax.experimental.pallas import tpu as pltpu

D_MODEL = 1024
PAST_LEN = 16384
PAGE_SIZE = 128
MLA_HEADS = 4
MLA_V = 128
MLA_NOPE = 128
MLA_ROPE = 64
Q_RANK = 384
KV_RANK = 256
MLA_SCALE = (MLA_NOPE + MLA_ROPE) ** -0.5
ROPE_THETA = 10000.0
GLA_HEADS = 4
GLA_DV = 128
GLA_DK = 64
GLA_WIDTH = GLA_HEADS * GLA_DV
GLA_KW = GLA_HEADS * GLA_DK
GATE_RANK = 16
GATE_TAU = 16.0
GLA_CHUNK = 64
D_FF = 4 * D_MODEL
DEPTH = 1
ALPHA = (2.0 * DEPTH) ** 0.25
EPS = 1e-5

LANES = 128
QK_WIDTH = KV_RANK + LANES
VMEM_LIMIT = 56 * 1024 * 1024
KEY_TILE = 256
STREAM_HEADS = 2
STEP_TILES = 2

_C_CQ = 0
_C_CKV = _C_CQ + Q_RANK
_C_GQ = _C_CKV + KV_RANK
_C_GK = _C_GQ + GLA_KW
_C_GV = _C_GK + GLA_KW
_C_GG = _C_GV + GLA_WIDTH
_C_KR = _C_GG + GLA_WIDTH
_C_GR = _C_KR + 2 * MLA_ROPE
IN_COLS_P = _C_GR + LANES

BF16 = jnp.bfloat16
F32 = jnp.float32


def _dot(a, b):
    return jnp.dot(a, b, preferred_element_type=F32)


def _dot_nt(a, b):
    return lax.dot_general(a, b, (((1,), (1,)), ((), ())), preferred_element_type=F32)


def _dot_tn(a, b):
    return lax.dot_general(a, b, (((0,), (0,)), ((), ())), preferred_element_type=F32)


def _rms(x, g):
    return x * lax.rsqrt(jnp.mean(x * x, axis=-1, keepdims=True) + EPS) * g


def _layer_norm(x, g, b):
    mu = jnp.mean(x, axis=-1, keepdims=True)
    xc = x - mu
    var = jnp.mean(xc * xc, axis=-1, keepdims=True)
    return xc * lax.rsqrt(var + EPS) * g + b


def _proj_kernel(x_ref, cos_ref, sin_ref, win_ref, qn_ref, wuq_ref, kvn_ref, wuk_ref,
                 wg2_ref, bg_ref,
                 q_ref, kcat_ref, ckv_ref, kr_ref, gq_ref, gk_ref, gv_ref, la_ref, gg_ref,
                 ckvt_ref=None):
    xb = x_ref[...].astype(BF16)
    cos = cos_ref[...]
    sin = sin_ref[...]

    cq = _dot(xb, win_ref[:, _C_CQ:_C_CQ + Q_RANK])
    cqn = _rms(cq, qn_ref[...]).astype(BF16)
    q = _dot(cqn, wuq_ref[...])
    nh = MLA_HEADS * MLA_NOPE
    for h in range(MLA_HEADS):
        nope = q[:, h * MLA_NOPE:(h + 1) * MLA_NOPE].astype(BF16)
        q_lat = _dot(nope, wuk_ref[h])
        r = q[:, nh + h * LANES: nh + (h + 1) * LANES]
        r_sw = q[:, nh + (MLA_HEADS + h) * LANES: nh + (MLA_HEADS + h + 1) * LANES]
        q_rope = r * cos + r_sw * sin
        q_ref[h, :, 0:KV_RANK] = (q_lat * MLA_SCALE).astype(BF16)
        q_ref[h, :, KV_RANK:QK_WIDTH] = (q_rope * MLA_SCALE).astype(BF16)

    ckv = _rms(_dot(xb, win_ref[:, _C_CKV:_C_CKV + KV_RANK]), kvn_ref[...])
    ckv_ref[...] = ckv
    krr = _dot(xb, win_ref[:, _C_KR:_C_KR + 2 * MLA_ROPE])
    k_rope = (krr[:, :MLA_ROPE] * cos[:, :MLA_ROPE]
              + krr[:, MLA_ROPE:] * sin[:, :MLA_ROPE])
    kr_ref[...] = k_rope
    kcat_ref[:, 0:KV_RANK] = ckv.astype(BF16)
    kcat_ref[:, KV_RANK:QK_WIDTH] = jnp.concatenate(
        [k_rope, jnp.zeros_like(k_rope)], axis=-1).astype(BF16)
    if ckvt_ref is not None:
        for n in range(ckvt_ref.shape[0]):
            ckvt_ref[n] = jnp.transpose(ckv[n * KEY_TILE:(n + 1) * KEY_TILE, :]).astype(BF16)

    gq_ref[...] = _dot(xb, win_ref[:, _C_GQ:_C_GQ + GLA_KW]) * (GLA_DK ** -0.5)
    gk_ref[...] = _dot(xb, win_ref[:, _C_GK:_C_GK + GLA_KW])
    gv_ref[...] = _dot(xb, win_ref[:, _C_GV:_C_GV + GLA_WIDTH]).astype(gv_ref.dtype)
    gg_ref[...] = _dot(xb, win_ref[:, _C_GG:_C_GG + GLA_WIDTH])
    gr = _dot(xb, win_ref[:, _C_GR:_C_GR + LANES]).astype(BF16)
    z = _dot(gr, wg2_ref[...]) + bg_ref[...]
    la_ref[...] = (jnp.minimum(z, 0.0) - jnp.log1p(jnp.exp(-jnp.abs(z)))) / GATE_TAU


def _proj(x2d, cos_t, sin_t, w, tm, gv_dtype, with_ckv_t):
    m = x2d.shape[0]
    grid = (m // tm,)
    row = lambda i: (i, 0)
    const2 = lambda i: (0, 0)
    const3 = lambda i: (0, 0, 0)
    const3_row = lambda i: (i, 0, 0)
    pos_blocks = cos_t.shape[0] // tm
    pos_row = lambda i: (i % pos_blocks, 0)
    in_specs = [
        pl.BlockSpec((tm, D_MODEL), row),
        pl.BlockSpec((tm, LANES), pos_row),
        pl.BlockSpec((tm, LANES), pos_row),
        pl.BlockSpec((D_MODEL, IN_COLS_P), const2),
        pl.BlockSpec((1, Q_RANK), const2),
        pl.BlockSpec(w['w_uq'].shape, const2),
        pl.BlockSpec((1, KV_RANK), const2),
        pl.BlockSpec((MLA_HEADS, MLA_NOPE, KV_RANK), const3),
        pl.BlockSpec((LANES, GLA_KW), const2),
        pl.BlockSpec((1, GLA_KW), const2),
    ]
    out_shape = [
        jax.ShapeDtypeStruct((MLA_HEADS, m, QK_WIDTH), BF16),
        jax.ShapeDtypeStruct((m, QK_WIDTH), BF16),
        jax.ShapeDtypeStruct((m, KV_RANK), F32),
        jax.ShapeDtypeStruct((m, MLA_ROPE), F32),
        jax.ShapeDtypeStruct((m, GLA_KW), F32),
        jax.ShapeDtypeStruct((m, GLA_KW), F32),
        jax.ShapeDtypeStruct((m, GLA_WIDTH), gv_dtype),
        jax.ShapeDtypeStruct((m, GLA_KW), F32),
        jax.ShapeDtypeStruct((m, GLA_WIDTH), F32),
    ]
    out_specs = [
        pl.BlockSpec((MLA_HEADS, tm, QK_WIDTH), lambda i: (0, i, 0)),
        pl.BlockSpec((tm, QK_WIDTH), row),
        pl.BlockSpec((tm, KV_RANK), row),
        pl.BlockSpec((tm, MLA_ROPE), row),
        pl.BlockSpec((tm, GLA_KW), row),
        pl.BlockSpec((tm, GLA_KW), row),
        pl.BlockSpec((tm, GLA_WIDTH), row),
        pl.BlockSpec((tm, GLA_KW), row),
        pl.BlockSpec((tm, GLA_WIDTH), row),
    ]
    if with_ckv_t:
        out_shape.append(jax.ShapeDtypeStruct((m // KEY_TILE, KV_RANK, KEY_TILE), BF16))
        out_specs.append(pl.BlockSpec((tm // KEY_TILE, KV_RANK, KEY_TILE), const3_row))
    return pl.pallas_call(
        _proj_kernel, grid=grid, in_specs=in_specs, out_specs=out_specs, out_shape=out_shape,
        compiler_params=pltpu.CompilerParams(
            dimension_semantics=("arbitrary",), vmem_limit_bytes=VMEM_LIMIT),
        name="proj",
    )(x2d, cos_t, sin_t, w['w_in'], w['q_norm'], w['w_uq'], w['kv_norm'], w['w_uk'],
      w['w_gate2'], w['b_gate'])


def _mla_prompt_kernel(q_ref, k_ref, vt_ref, wuv_ref, o_ref, m_scr, l_scr, acc_scr,
                       s_a, s_b, *, tq):
    i = pl.program_id(1)
    rows = MLA_HEADS * tq
    size = STEP_TILES * KEY_TILE
    hw = STREAM_HEADS * tq
    n_streams = MLA_HEADS // STREAM_HEADS
    m_scr[...] = jnp.full(m_scr.shape, -jnp.inf, F32)
    l_scr[...] = jnp.zeros(l_scr.shape, F32)
    acc_scr[...] = jnp.zeros(acc_scr.shape, F32)

    def scores_to(s_ref, j):
        start = pl.multiple_of(j * size, size)
        kblk = k_ref[pl.ds(start, size), :]
        for g in range(n_streams):
            qg = q_ref[g * STREAM_HEADS:(g + 1) * STREAM_HEADS].reshape(hw, QK_WIDTH)
            s_ref[:, g * hw:(g + 1) * hw] = _dot_nt(kblk, qg)

    def fold(s_ref, j, masked):
        tile0 = j * STEP_TILES
        if masked:
            key = j * size + lax.broadcasted_iota(jnp.int32, (size, hw), 0)
            tok = i * tq + (lax.broadcasted_iota(jnp.int32, (size, hw), 1) & (tq - 1))
            visible = key <= tok
        m_all = m_scr[...]
        l_all = l_scr[...]
        m_out, l_out, updates = [], [], []
        for g in range(n_streams):
            lanes = slice(g * hw, (g + 1) * hw)
            st = s_ref[:, lanes]
            if masked:
                st = jnp.where(visible, st, -jnp.inf)
            m_old = m_all[:, lanes]
            m_new = jnp.maximum(m_old, jnp.max(st, axis=0, keepdims=True))
            p = jnp.exp(st - m_new)
            alpha = jnp.exp(m_old - m_new)
            l_out.append(alpha * l_all[:, lanes] + jnp.sum(p, axis=0, keepdims=True))
            m_out.append(m_new)
            pb = p.astype(BF16)
            pv = _dot(vt_ref[tile0], pb[0:KEY_TILE])
            for t in range(1, STEP_TILES):
                pv = pv + _dot(vt_ref[tile0 + t], pb[t * KEY_TILE:(t + 1) * KEY_TILE])
            updates.append((lanes, alpha, pv))
        for lanes, alpha, pv in updates:
            acc_scr[:, lanes] = alpha * acc_scr[:, lanes] + pv
        m_scr[...] = jnp.concatenate(m_out, axis=1)
        l_scr[...] = jnp.concatenate(l_out, axis=1)

    last = (i * tq) // size
    scores_to(s_a, 0)

    def two_steps(k, carry):
        scores_to(s_b, 2 * k + 1)
        fold(s_a, 2 * k, masked=False)
        scores_to(s_a, 2 * k + 2)
        fold(s_b, 2 * k + 1, masked=False)
        return carry

    lax.fori_loop(0, last // 2, two_steps, 0)

    @pl.when(last % 2 == 1)
    def _():
        scores_to(s_b, last)
        fold(s_a, last - 1, masked=False)
        fold(s_b, last, masked=True)

    @pl.when(last % 2 == 0)
    def _():
        fold(s_a, last, masked=True)

    o_lat_t = (acc_scr[...] / l_scr[...]).astype(BF16)
    for h in range(MLA_HEADS):
        o_ref[:, h * MLA_V:(h + 1) * MLA_V] = _dot_tn(
            o_lat_t[:, h * tq:(h + 1) * tq], wuv_ref[h]).astype(o_ref.dtype)


def _mla_prompt(q, kcat, ckv_t, w_uv, batch, seq):
    tq = KEY_TILE
    nq = seq // tq
    kern = functools.partial(_mla_prompt_kernel, tq=tq)
    rows = MLA_HEADS * tq
    return pl.pallas_call(
        kern, grid=(batch, nq),
        in_specs=[
            pl.BlockSpec((MLA_HEADS, tq, QK_WIDTH), lambda b, i: (0, b * nq + i, 0)),
            pl.BlockSpec((seq, QK_WIDTH), lambda b, i: (b, 0)),
            pl.BlockSpec((seq // KEY_TILE, KV_RANK, KEY_TILE), lambda b, i: (b, 0, 0)),
            pl.BlockSpec((MLA_HEADS, KV_RANK, MLA_V), lambda b, i: (0, 0, 0)),
        ],
        out_specs=pl.BlockSpec((tq, MLA_HEADS * MLA_V), lambda b, i: (b * nq + i, 0)),
        out_shape=jax.ShapeDtypeStruct((batch * seq, MLA_HEADS * MLA_V), BF16),
        scratch_shapes=[pltpu.VMEM((1, rows), F32), pltpu.VMEM((1, rows), F32),
                        pltpu.VMEM((KV_RANK, rows), F32),
                        pltpu.VMEM((STEP_TILES * KEY_TILE, rows), F32),
                        pltpu.VMEM((STEP_TILES * KEY_TILE, rows), F32)],
        compiler_params=pltpu.CompilerParams(
            dimension_semantics=("arbitrary", "arbitrary"), vmem_limit_bytes=VMEM_LIMIT),
        name="mla_prompt",
    )(q, kcat, ckv_t, w_uv)


SAMPLE_Q_ROWS = 16
PAGES_PER_CHUNK = 32
SAMPLE_SLOTS = 4
SAMPLE_PREFETCH = SAMPLE_SLOTS - 1


def _mla_sample_kernel(pt_ref, q_ref, knew_ref, wuv_ref, ckv_hbm, krt_hbm, o_ref,
                       cbuf, kbuf, cbf, sem_c, sem_k, *, n_batch, n_chunks):
    b = pl.program_id(0)
    g_pages = PAGES_PER_CHUNK

    def page_copies(bb, c, slot, g):
        page = pt_ref[bb, c * g_pages + g]
        tok = pl.ds(g * PAGE_SIZE, PAGE_SIZE)
        return (pltpu.make_async_copy(ckv_hbm.at[page], cbuf.at[slot, tok, :], sem_c.at[slot]),
                pltpu.make_async_copy(krt_hbm.at[page], kbuf.at[slot, :, tok], sem_k.at[slot]))

    def start_chunk(bb, c, slot):
        for g in range(g_pages):
            cc, ck = page_copies(bb, c, slot, g)
            cc.start()
            ck.start()

    def wait_chunk(bb, c, slot):
        for g in range(g_pages):
            cc, ck = page_copies(bb, c, slot, g)
            cc.wait()
            ck.wait()

    def start_ahead(c):
        cn = c + SAMPLE_PREFETCH
        if cn < n_chunks:
            start_chunk(b, cn, cn % SAMPLE_SLOTS)
        else:
            @pl.when(b + 1 < n_batch)
            def _():
                start_chunk(b + 1, cn - n_chunks, (cn - n_chunks) % SAMPLE_SLOTS)

    @pl.when(b == 0)
    def _():
        for c0 in range(SAMPLE_PREFETCH):
            start_chunk(0, c0, c0)

    q = q_ref[0]
    q_lat = q[:, :KV_RANK]
    q_rope = q[:, KV_RANK:KV_RANK + MLA_ROPE]
    m = jnp.full((SAMPLE_Q_ROWS, 1), -jnp.inf, F32)
    l = jnp.zeros((SAMPLE_Q_ROWS, 1), F32)
    acc = jnp.zeros((SAMPLE_Q_ROWS, KV_RANK), F32)

    n_tiles = g_pages * PAGE_SIZE // KEY_TILE

    def tiled_pv(p_b, cbf_slot):
        out = _dot(p_b[:, 0:KEY_TILE], cbf[cbf_slot, 0:KEY_TILE, :])
        for t in range(1, n_tiles):
            tok = slice(t * KEY_TILE, (t + 1) * KEY_TILE)
            out = out + _dot(p_b[:, tok], cbf[cbf_slot, tok, :])
        return out

    pending = None
    for c in range(n_chunks):
        slot = c % SAMPLE_SLOTS
        wait_chunk(b, c, slot)
        if pending is not None:
            alpha_p, p_p, slot_p = pending
            acc = alpha_p * acc + tiled_pv(p_p, slot_p)
        s_tiles = []
        for t in range(n_tiles):
            tok = slice(t * KEY_TILE, (t + 1) * KEY_TILE)
            cb = cbuf[slot, tok, :].astype(BF16)
            cbf[c % 2, tok, :] = cb
            kb = kbuf[slot, :, tok].astype(BF16)
            s_tiles.append(_dot_nt(q_lat, cb) + _dot(q_rope, kb))
        s = jnp.concatenate(s_tiles, axis=1)
        m_new = jnp.maximum(m, jnp.max(s, axis=-1, keepdims=True))
        p = jnp.exp(s - m_new)
        alpha = jnp.exp(m - m_new)
        l = alpha * l + jnp.sum(p, axis=-1, keepdims=True)
        m = m_new
        pending = (alpha, p.astype(BF16), c % 2)
        start_ahead(c)
    alpha_p, p_p, slot_p = pending
    acc = alpha_p * acc + tiled_pv(p_p, slot_p)

    knew = knew_ref[0].astype(F32)
    s_self = jnp.sum(q.astype(F32) * knew, axis=-1, keepdims=True)
    m_new = jnp.maximum(m, s_self)
    p_self = jnp.exp(s_self - m_new)
    alpha = jnp.exp(m - m_new)
    l = alpha * l + p_self
    acc = alpha * acc + p_self.astype(BF16).astype(F32) * knew[:, :KV_RANK]
    o_lat = (acc / l).astype(BF16)
    for h in range(MLA_HEADS):
        res = _dot(o_lat, wuv_ref[h])
        o_ref[0, :, h * MLA_V:(h + 1) * MLA_V] = res[h:h + 1, :]


def _mla_sample(page_table, q_s, knew, w_uv, cache_ckv, cache_krope_t):
    n_batch, n_pages = page_table.shape
    n_chunks = n_pages // PAGES_PER_CHUNK
    chunk_rows = PAGES_PER_CHUNK * PAGE_SIZE
    assert n_chunks * PAGES_PER_CHUNK == n_pages and n_chunks % SAMPLE_SLOTS == 0
    assert SAMPLE_PREFETCH <= n_chunks
    kern = functools.partial(_mla_sample_kernel, n_batch=n_batch, n_chunks=n_chunks)
    grid_spec = pltpu.PrefetchScalarGridSpec(
        num_scalar_prefetch=1,
        grid=(n_batch,),
        in_specs=[
            pl.BlockSpec((1, SAMPLE_Q_ROWS, QK_WIDTH), lambda b, pt: (b, 0, 0)),
            pl.BlockSpec((1, 1, QK_WIDTH), lambda b, pt: (b, 0, 0)),
            pl.BlockSpec((MLA_HEADS, KV_RANK, MLA_V), lambda b, pt: (0, 0, 0)),
            pl.BlockSpec(memory_space=pl.ANY),
            pl.BlockSpec(memory_space=pl.ANY),
        ],
        out_specs=pl.BlockSpec((1, 1, MLA_HEADS * MLA_V), lambda b, pt: (b, 0, 0)),
        scratch_shapes=[
            pltpu.VMEM((SAMPLE_SLOTS, chunk_rows, KV_RANK), F32),
            pltpu.VMEM((SAMPLE_SLOTS, MLA_ROPE, chunk_rows), F32),
            pltpu.VMEM((2, chunk_rows, KV_RANK), BF16),
            pltpu.SemaphoreType.DMA((SAMPLE_SLOTS,)),
            pltpu.SemaphoreType.DMA((SAMPLE_SLOTS,)),
        ],
    )
    return pl.pallas_call(
        kern, grid_spec=grid_spec,
        out_shape=jax.ShapeDtypeStruct((n_batch, 1, MLA_HEADS * MLA_V), F32),
        compiler_params=pltpu.CompilerParams(
            dimension_semantics=("arbitrary",), vmem_limit_bytes=VMEM_LIMIT),
        name="mla_sample",
    )(page_table, q_s, knew, w_uv, cache_ckv, cache_krope_t)


GLA_UNROLL = 4


def _split3(x):
    hi = x.astype(BF16)
    r1 = x - hi.astype(F32)
    mid = r1.astype(BF16)
    lo = (r1 - mid.astype(F32)).astype(BF16)
    return hi, mid, lo


def _gla_gate_out(o, gg, gnorm):
    return _rms(o, gnorm) * (gg * jax.nn.sigmoid(gg))


def _gla_prompt_kernel(gq_ref, gk_ref, gv_ref, la_ref, gg_ref, gn_ref, o_ref, s_out_ref,
                       s_scr, *, n_sub):
    t = pl.program_id(1)
    cs = GLA_CHUNK

    @pl.when(t == 0)
    def _():
        s_scr[...] = jnp.zeros(s_scr.shape, F32)

    ri = lax.broadcasted_iota(jnp.int32, (cs, cs), 0)
    ci = lax.broadcasted_iota(jnp.int32, (cs, cs), 1)
    tri = ri >= ci
    tri_b = tri.astype(BF16)
    lane_head = lax.broadcasted_iota(jnp.int32, (cs, GLA_KW), 1) // GLA_DK
    gnorm = gn_ref[...]

    def chunk(c, s_prev):
        r0 = pl.multiple_of(c * cs, cs)
        g = la_ref[pl.ds(r0, cs), :]
        g_hi, g_mid, g_lo = _split3(g)
        bcum = _dot(tri_b, g_hi) + _dot(tri_b, g_mid) + _dot(tri_b, g_lo)
        b_last = bcum[cs - 1:cs, :]
        qf = gq_ref[pl.ds(r0, cs), :]
        kf = gk_ref[pl.ds(r0, cs), :]
        v = gv_ref[pl.ds(r0, cs), :]
        q_t = qf * jnp.exp(bcum)
        k_t = (kf * jnp.exp(-bcum)).astype(BF16)
        k_h = (kf * jnp.exp(b_last - bcum)).astype(BF16)
        decay = jnp.exp(b_last)
        q_stack = jnp.concatenate(
            [jnp.where(lane_head == h, q_t, 0.0) for h in range(GLA_HEADS)],
            axis=0).astype(BF16)
        a_stack = _dot_nt(q_stack, k_t)
        o_inter = _dot(q_stack, s_prev.astype(BF16))
        kv = _dot_tn(k_h, v)
        decay_col = jnp.transpose(jnp.broadcast_to(decay, (GLA_DV, GLA_KW)))
        kv_diag = jnp.concatenate(
            [kv[h * GLA_DK:(h + 1) * GLA_DK, h * GLA_DV:(h + 1) * GLA_DV]
             for h in range(GLA_HEADS)], axis=0)
        s_new = decay_col * s_prev + kv_diag
        for h in range(GLA_HEADS):
            a_h = jnp.where(tri, a_stack[h * cs:(h + 1) * cs], 0.0).astype(BF16)
            o_h = _dot(a_h, v[:, h * GLA_DV:(h + 1) * GLA_DV]) + o_inter[h * cs:(h + 1) * cs]
            gg_h = gg_ref[pl.ds(r0, cs), h * GLA_DV:(h + 1) * GLA_DV]
            o_ref[pl.ds(r0, cs), h * GLA_DV:(h + 1) * GLA_DV] = _gla_gate_out(
                o_h, gg_h, gnorm).astype(o_ref.dtype)
        return s_new

    def trip(j, carry):
        s = s_scr[...]
        for u in range(GLA_UNROLL):
            s = chunk(j * GLA_UNROLL + u, s)
        s_scr[...] = s
        return carry

    lax.fori_loop(0, n_sub // GLA_UNROLL, trip, 0)
    s_out_ref[0] = s_scr[...]


def _gla_prompt(gq, gk, gv, la, gg, gnorm, batch, seq, tc=1024):
    nt = seq // tc
    n_sub = tc // GLA_CHUNK
    row = lambda b, t: (b * nt + t, 0)
    kern = functools.partial(_gla_prompt_kernel, n_sub=n_sub)
    return pl.pallas_call(
        kern, grid=(batch, nt),
        in_specs=[
            pl.BlockSpec((tc, GLA_KW), row),
            pl.BlockSpec((tc, GLA_KW), row),
            pl.BlockSpec((tc, GLA_WIDTH), row),
            pl.BlockSpec((tc, GLA_KW), row),
            pl.BlockSpec((tc, GLA_WIDTH), row),
            pl.BlockSpec((1, GLA_DV), lambda b, t: (0, 0)),
        ],
        out_specs=[
            pl.BlockSpec((tc, GLA_WIDTH), row),
            pl.BlockSpec((1, GLA_KW, GLA_DV), lambda b, t: (b, 0, 0)),
        ],
        out_shape=[
            jax.ShapeDtypeStruct((batch * seq, GLA_WIDTH), BF16),
            jax.ShapeDtypeStruct((batch, GLA_KW, GLA_DV), F32),
        ],
        scratch_shapes=[pltpu.VMEM((GLA_KW, GLA_DV), F32)],
        compiler_params=pltpu.CompilerParams(
            dimension_semantics=("arbitrary", "arbitrary"), vmem_limit_bytes=VMEM_LIMIT),
        name="gla_prompt",
    )(gq, gk, gv, la, gg, gnorm)


def _gla_step_kernel(s_ref, gq_ref, gk_ref, gv_ref, la_ref, gg_ref, gn_ref, o_ref, s_out_ref,
                     *, bb):
    gnorm = gn_ref[...]

    def col(x_row):
        return jnp.transpose(jnp.broadcast_to(x_row, (GLA_DV, GLA_KW)))

    for i in range(bb):
        s = s_ref[i]
        e_col = col(jnp.exp(la_ref[i:i + 1, :]))
        k_col = col(gk_ref[i:i + 1, :])
        q_col = col(gq_ref[i:i + 1, :].astype(BF16).astype(F32))
        v_row = gv_ref[i:i + 1, :].astype(F32)
        v_rows = jnp.concatenate(
            [jnp.broadcast_to(v_row[:, h * GLA_DV:(h + 1) * GLA_DV], (GLA_DK, GLA_DV))
             for h in range(GLA_HEADS)], axis=0)
        s_new = e_col * s + k_col * v_rows
        s_out_ref[i] = s_new
        prod = q_col * s_new.astype(BF16).astype(F32)
        for h in range(GLA_HEADS):
            o_h = jnp.sum(prod[h * GLA_DK:(h + 1) * GLA_DK], axis=0, keepdims=True)
            gg_h = gg_ref[i:i + 1, h * GLA_DV:(h + 1) * GLA_DV]
            o_ref[i:i + 1, h * GLA_DV:(h + 1) * GLA_DV] = _gla_gate_out(o_h, gg_h, gnorm)


def _gla_step(state, gq, gk, gv, la, gg, gnorm, bb=8):
    n = state.shape[0]
    row = lambda i: (i, 0)
    kern = functools.partial(_gla_step_kernel, bb=bb)
    return pl.pallas_call(
        kern, grid=(n // bb,),
        in_specs=[
            pl.BlockSpec((bb, GLA_KW, GLA_DV), lambda i: (i, 0, 0)),
            pl.BlockSpec((bb, GLA_KW), row),
            pl.BlockSpec((bb, GLA_KW), row),
            pl.BlockSpec((bb, GLA_WIDTH), row),
            pl.BlockSpec((bb, GLA_KW), row),
            pl.BlockSpec((bb, GLA_WIDTH), row),
            pl.BlockSpec((1, GLA_DV), lambda i: (0, 0)),
        ],
        out_specs=[
            pl.BlockSpec((bb, GLA_WIDTH), row),
            pl.BlockSpec((bb, GLA_KW, GLA_DV), lambda i: (i, 0, 0)),
        ],
        out_shape=[
            jax.ShapeDtypeStruct((n, GLA_WIDTH), F32),
            jax.ShapeDtypeStruct((n, GLA_KW, GLA_DV), F32),
        ],
        compiler_params=pltpu.CompilerParams(
            dimension_semantics=("arbitrary",), vmem_limit_bytes=VMEM_LIMIT),
        name="gla_step",
    )(state, gq, gk, gv, la, gg, gnorm)


FF_CHUNK = 1024


def _post_kernel(x_ref, mla_ref, gla_ref, wout_ref, g1_ref, b1_ref, w1_ref, w2_ref,
                 g2_ref, b2_ref, y_ref):
    half = MLA_HEADS * MLA_V
    mix = (_dot(mla_ref[...].astype(BF16), wout_ref[0:half, :])
           + _dot(gla_ref[...].astype(BF16), wout_ref[half:, :]))
    x1 = _layer_norm(ALPHA * x_ref[...] + mix, g1_ref[...], b1_ref[...])
    x1b = x1.astype(BF16)
    acc = jnp.zeros(x1.shape, F32)
    for c in range(D_FF // FF_CHUNK):
        hmid = _dot(x1b, w1_ref[:, c * FF_CHUNK:(c + 1) * FF_CHUNK])
        hmid = jnp.square(jnp.maximum(hmid, 0.0)).astype(BF16)
        acc = acc + _dot(hmid, w2_ref[c * FF_CHUNK:(c + 1) * FF_CHUNK, :])
    y_ref[...] = _layer_norm(ALPHA * x1 + acc, g2_ref[...], b2_ref[...])


def _post(x2d, mla, gla, w, tm):
    m = x2d.shape[0]
    row = lambda i: (i, 0)
    const = lambda i: (0, 0)
    resident = dict(pipeline_mode=pl.Buffered(1))
    return pl.pallas_call(
        _post_kernel, grid=(m // tm,),
        in_specs=[
            pl.BlockSpec((tm, D_MODEL), row),
            pl.BlockSpec((tm, MLA_HEADS * MLA_V), row),
            pl.BlockSpec((tm, GLA_WIDTH), row),
            pl.BlockSpec((D_MODEL, D_MODEL), const, **resident),
            pl.BlockSpec((1, D_MODEL), const),
            pl.BlockSpec((1, D_MODEL), const),
            pl.BlockSpec((D_MODEL, D_FF), const, **resident),
            pl.BlockSpec((D_FF, D_MODEL), const, **resident),
            pl.BlockSpec((1, D_MODEL), const),
            pl.BlockSpec((1, D_MODEL), const),
        ],
        out_specs=pl.BlockSpec((tm, D_MODEL), row),
        out_shape=jax.ShapeDtypeStruct((m, D_MODEL), F32),
        compiler_params=pltpu.CompilerParams(
            dimension_semantics=("arbitrary",), vmem_limit_bytes=VMEM_LIMIT),
        name="post",
    )(x2d, mla, gla, w['w_out'], w['ln1_g'], w['ln1_b'], w['w1'], w['w2'],
      w['ln2_g'], w['ln2_b'])


def _permute_w_in(w):
    sizes = (Q_RANK, KV_RANK, MLA_ROPE, GLA_KW, GLA_KW, GLA_WIDTH, GATE_RANK, GLA_WIDTH)
    off = np.concatenate([[0], np.cumsum(sizes)]).tolist()
    cq, ckv, kr, gq, gk, gv, gr, gg = [w[:, off[i]:off[i + 1]] for i in range(8)]
    half = MLA_ROPE // 2
    pad = jnp.zeros((w.shape[0], LANES - GATE_RANK), w.dtype)
    return jnp.concatenate(
        [cq, ckv, gq, gk, gv, gg, kr, kr[:, half:], kr[:, :half], gr, pad], axis=1).astype(BF16)


def _permute_w_uq(w):
    per_head = MLA_NOPE + MLA_ROPE
    half = MLA_ROPE // 2
    pad = jnp.zeros((w.shape[0], LANES - MLA_ROPE), w.dtype)
    nope, rope, rope_sw = [], [], []
    for h in range(MLA_HEADS):
        base = h * per_head
        nope.append(w[:, base:base + MLA_NOPE])
        r = w[:, base + MLA_NOPE:base + per_head]
        rope += [r, pad]
        rope_sw += [r[:, half:], r[:, :half], pad]
    return jnp.concatenate(nope + rope + rope_sw, axis=1).astype(BF16)


def _prep_weights(w_in, mla_q_norm, mla_w_uq, mla_kv_norm, mla_w_uk, mla_w_uv,
                  gla_w_gate2, gla_b_gate, gla_norm, w_out, ln1_g, ln1_b,
                  mlp_w1, mlp_w2, ln2_g, ln2_b, l):
    w_in_p = _permute_w_in(w_in[l])
    w_uq_p = _permute_w_uq(mla_w_uq[l])
    return dict(
        w_in=w_in_p,
        q_norm=mla_q_norm[l][None, :],
        w_uq=w_uq_p,
        kv_norm=mla_kv_norm[l][None, :],
        w_uk=jnp.transpose(mla_w_uk[l], (1, 2, 0)).astype(BF16),
        w_uv=jnp.transpose(mla_w_uv[l], (1, 0, 2)).astype(BF16),
        w_gate2=jnp.pad(gla_w_gate2[l], ((0, LANES - GATE_RANK), (0, 0))).astype(BF16),
        b_gate=gla_b_gate[l][None, :],
        gla_norm=gla_norm[l][None, :],
        w_out=w_out[l].astype(BF16),
        ln1_g=ln1_g[l][None, :], ln1_b=ln1_b[l][None, :],
        w1=mlp_w1[l].astype(BF16), w2=mlp_w2[l].astype(BF16),
        ln2_g=ln2_g[l][None, :], ln2_b=ln2_b[l][None, :],
    )


def _rope_tables(pos):
    inv = ROPE_THETA ** (-jnp.arange(0, MLA_ROPE, 2, dtype=F32) / MLA_ROPE)
    ang = pos.astype(F32)[:, None] * inv[None, :]
    cos, sin = jnp.cos(ang), jnp.sin(ang)
    zeros = jnp.zeros((pos.shape[0], LANES - MLA_ROPE), F32)
    return (jnp.concatenate([cos, cos, zeros], axis=-1),
            jnp.concatenate([-sin, sin, zeros], axis=-1))


def kernel(x_prompt, x_sample, cache_ckv, cache_krope, state_gla, page_table, w_in,
           mla_q_norm, mla_w_uq, mla_kv_norm, mla_w_uk, mla_w_uv, gla_w_gate2, gla_b_gate,
           gla_norm, w_out, ln1_g, ln1_b, mlp_w1, mlp_w2, ln2_g, ln2_b):
    assert w_in.shape[0] == DEPTH == 1
    batch, seq, _ = x_prompt.shape
    n_dec, t_new, _ = x_sample.shape
    assert t_new == 1
    l = 0
    w = _prep_weights(w_in, mla_q_norm, mla_w_uq, mla_kv_norm, mla_w_uk, mla_w_uv,
                      gla_w_gate2, gla_b_gate, gla_norm, w_out, ln1_g, ln1_b,
                      mlp_w1, mlp_w2, ln2_g, ln2_b, l)

    xp = x_prompt.reshape(batch * seq, D_MODEL)
    cos_p, sin_p = _rope_tables(jnp.arange(seq, dtype=jnp.int32))
    q, kcat, ckv_p, kr_p, gq, gk, gv, la, gg, ckv_t = _proj(
        xp, cos_p, sin_p, w, 512, BF16, True)
    mla_p = _mla_prompt(q, kcat, ckv_t, w['w_uv'], batch, seq)
    gla_p, s_p = _gla_prompt(gq, gk, gv, la, gg, w['gla_norm'], batch, seq)
    y_p = _post(xp, mla_p, gla_p, w, tm=512)

    xs = x_sample.reshape(n_dec, D_MODEL)
    cos_s, sin_s = _rope_tables(jnp.full((n_dec,), PAST_LEN, dtype=jnp.int32))
    q, kcat, ckv_s, kr_s, gq, gk, gv, la, gg = _proj(xs, cos_s, sin_s, w, n_dec, F32, False)
    q_s = jnp.pad(jnp.transpose(q, (1, 0, 2)),
                  ((0, 0), (0, SAMPLE_Q_ROWS - MLA_HEADS), (0, 0)))
    krope_t = jnp.swapaxes(cache_krope[l], 1, 2)
    mla_s = _mla_sample(page_table, q_s, kcat[:, None, :], w['w_uv'],
                        cache_ckv[l], krope_t)
    gla_s, s_s = _gla_step(state_gla[l].reshape(n_dec, GLA_KW, GLA_DV),
                           gq, gk, gv, la, gg, w['gla_norm'])
    y_s = _post(xs, mla_s.reshape(n_dec, MLA_HEADS * MLA_V), gla_s, w, tm=n_dec)

    return (y_p.reshape(batch, seq, D_MODEL),
            y_s.reshape(n_dec, 1, D_MODEL),
            ckv_p.reshape(1, batch, seq, KV_RANK),
            kr_p.reshape(1, batch, seq, MLA_ROPE),
            s_p.reshape(1, batch, GLA_HEADS, GLA_DK, GLA_DV),
            ckv_s.reshape(1, n_dec, 1, KV_RANK),
            kr_s.reshape(1, n_dec, 1, MLA_ROPE),
            s_s.reshape(1, n_dec, GLA_HEADS, GLA_DK, GLA_DV))
```

```python
import functools

import numpy as np
import jax
import jax.numpy as jnp
from jax import lax
from jax.experimental import pallas as pl
from jax.experimental.pallas import tpu as pltpu

D_MODEL = 1024
PAST_LEN = 16384
PAGE_SIZE = 128
MLA_HEADS = 4
MLA_V = 128
MLA_NOPE = 128
MLA_ROPE = 64
Q_RANK = 384
KV_RANK = 256
MLA_SCALE = (MLA_NOPE + MLA_ROPE) ** -0.5
ROPE_THETA = 10000.0
GLA_HEADS = 4
GLA_DV = 128
GLA_DK = 64
GLA_WIDTH = GLA_HEADS * GLA_DV
GLA_KW = GLA_HEADS * GLA_DK
GATE_RANK = 16
GATE_TAU = 16.0
GLA_CHUNK = 64
D_FF = 4 * D_MODEL
DEPTH = 1
ALPHA = (2.0 * DEPTH) ** 0.25
EPS = 1e-5

LANES = 128
QK_WIDTH = KV_RANK + LANES
VMEM_LIMIT = 56 * 1024 * 1024
KEY_TILE = 256
STREAM_HEADS = 2
STEP_TILES = 2

_C_CQ = 0
_C_CKV = _C_CQ + Q_RANK
_C_GQ = _C_CKV + KV_RANK
_C_GK = _C_GQ + GLA_KW
_C_GV = _C_GK + GLA_KW
_C_GG = _C_GV + GLA_WIDTH
_C_KR = _C_GG + GLA_WIDTH
_C_GR = _C_KR + 2 * MLA_ROPE
IN_COLS_P = _C_GR + LANES

BF16 = jnp.bfloat16
F32 = jnp.float32


def _dot(a, b):
    return jnp.dot(a, b, preferred_element_type=F32)


def _dot_nt(a, b):
    return lax.dot_general(a, b, (((1,), (1,)), ((), ())), preferred_element_type=F32)


def _dot_tn(a, b):
    return lax.dot_general(a, b, (((0,), (0,)), ((), ())), preferred_element_type=F32)


def _rms(x, g):
    return x * lax.rsqrt(jnp.mean(x * x, axis=-1, keepdims=True) + EPS) * g


def _layer_norm(x, g, b):
    mu = jnp.mean(x, axis=-1, keepdims=True)
    xc = x - mu
    var = jnp.mean(xc * xc, axis=-1, keepdims=True)
    return xc * lax.rsqrt(var + EPS) * g + b


def _proj_kernel(x_ref, cos_ref, sin_ref, win_ref, qn_ref, wuq_ref, kvn_ref, wuk_ref,
                 wg2_ref, bg_ref,
                 q_ref, kcat_ref, ckv_ref, kr_ref, gq_ref, gk_ref, gv_ref, la_ref, gg_ref,
                 ckvt_ref=None):
    tm = x_ref.shape[0]
    sub = KEY_TILE if tm % KEY_TILE == 0 else tm
    for k in range(tm // sub):
        _proj_rows(pl.ds(k * sub, sub), k, x_ref, cos_ref, sin_ref, win_ref, qn_ref, wuq_ref,
                   kvn_ref, wuk_ref, wg2_ref, bg_ref, q_ref, kcat_ref, ckv_ref, kr_ref,
                   gq_ref, gk_ref, gv_ref, la_ref, gg_ref, ckvt_ref)


def _proj_rows(r, k, x_ref, cos_ref, sin_ref, win_ref, qn_ref, wuq_ref, kvn_ref, wuk_ref,
               wg2_ref, bg_ref, q_ref, kcat_ref, ckv_ref, kr_ref, gq_ref, gk_ref, gv_ref,
               la_ref, gg_ref, ckvt_ref):
    xb = x_ref[r, :].astype(BF16)
    cos = cos_ref[r, :]
    sin = sin_ref[r, :]

    cq = _dot(xb, win_ref[:, _C_CQ:_C_CQ + Q_RANK])
    cqn = _rms(cq, qn_ref[...]).astype(BF16)
    q = _dot(cqn, wuq_ref[...])
    nh = MLA_HEADS * MLA_NOPE
    nr = MLA_HEADS * MLA_ROPE
    low_half = lax.broadcasted_iota(jnp.int32, (1, LANES), 1) < MLA_ROPE
    for h in range(MLA_HEADS):
        nope = q[:, h * MLA_NOPE:(h + 1) * MLA_NOPE].astype(BF16)
        q_lat = _dot(nope, wuk_ref[h])
        q_ref[h, r, 0:KV_RANK] = (q_lat * MLA_SCALE).astype(BF16)
    for pair in range(MLA_HEADS // 2):
        lanes = slice(nh + pair * LANES, nh + (pair + 1) * LANES)
        lanes_sw = slice(nh + nr + pair * LANES, nh + nr + (pair + 1) * LANES)
        rope2 = (q[:, lanes] * cos + q[:, lanes_sw] * sin) * MLA_SCALE
        for j, blk in enumerate((rope2, pltpu.roll(rope2, MLA_ROPE, 1))):
            q_ref[2 * pair + j, r, KV_RANK:QK_WIDTH] = jnp.where(low_half, blk, 0.0).astype(BF16)

    ckv = _rms(_dot(xb, win_ref[:, _C_CKV:_C_CKV + KV_RANK]), kvn_ref[...])
    ckv_ref[r, :] = ckv
    kr_gr = _dot(xb, win_ref[:, _C_KR:_C_GR + LANES])
    krr = kr_gr[:, :2 * MLA_ROPE]
    k_rope = (krr[:, :MLA_ROPE] * cos[:, :MLA_ROPE]
              + krr[:, MLA_ROPE:] * sin[:, :MLA_ROPE])
    kr_ref[r, :] = k_rope
    kcat_ref[r, 0:KV_RANK] = ckv.astype(BF16)
    kcat_ref[r, KV_RANK:QK_WIDTH] = jnp.concatenate(
        [k_rope, jnp.zeros_like(k_rope)], axis=-1).astype(BF16)
    if ckvt_ref is not None:
        ckvt_ref[k] = jnp.transpose(ckv).astype(BF16)

    gq_ref[r, :] = _dot(xb, win_ref[:, _C_GQ:_C_GQ + GLA_KW]) * (GLA_DK ** -0.5)
    gk_ref[r, :] = _dot(xb, win_ref[:, _C_GK:_C_GK + GLA_KW])
    gv_ref[r, :] = _dot(xb, win_ref[:, _C_GV:_C_GV + GLA_WIDTH]).astype(gv_ref.dtype)
    gg_ref[r, :] = _dot(xb, win_ref[:, _C_GG:_C_GG + GLA_WIDTH])
    gr = kr_gr[:, 2 * MLA_ROPE:].astype(BF16)
    z = _dot(gr, wg2_ref[...]) + bg_ref[...]
    la_ref[r, :] = (jnp.minimum(z, 0.0) - jnp.log1p(jnp.exp(-jnp.abs(z)))) / GATE_TAU


def _proj(x2d, cos_t, sin_t, w, tm, gv_dtype, with_ckv_t):
    m = x2d.shape[0]
    grid = (m // tm,)
    row = lambda i: (i, 0)
    const2 = lambda i: (0, 0)
    const3 = lambda i: (0, 0, 0)
    const3_row = lambda i: (i, 0, 0)
    pos_blocks = cos_t.shape[0] // tm
    pos_row = lambda i: (i % pos_blocks, 0)
    in_specs = [
        pl.BlockSpec((tm, D_MODEL), row),
        pl.BlockSpec((tm, LANES), pos_row),
        pl.BlockSpec((tm, LANES), pos_row),
        pl.BlockSpec((D_MODEL, IN_COLS_P), const2),
        pl.BlockSpec((1, Q_RANK), const2),
        pl.BlockSpec(w['w_uq'].shape, const2),
        pl.BlockSpec((1, KV_RANK), const2),
        pl.BlockSpec((MLA_HEADS, MLA_NOPE, KV_RANK), const3),
        pl.BlockSpec((LANES, GLA_KW), const2),
        pl.BlockSpec((1, GLA_KW), const2),
    ]
    out_shape = [
        jax.ShapeDtypeStruct((MLA_HEADS, m, QK_WIDTH), BF16),
        jax.ShapeDtypeStruct((m, QK_WIDTH), BF16),
        jax.ShapeDtypeStruct((m, KV_RANK), F32),
        jax.ShapeDtypeStruct((m, MLA_ROPE), F32),
        jax.ShapeDtypeStruct((m, GLA_KW), F32),
        jax.ShapeDtypeStruct((m, GLA_KW), F32),
        jax.ShapeDtypeStruct((m, GLA_WIDTH), gv_dtype),
        jax.ShapeDtypeStruct((m, GLA_KW), F32),
        jax.ShapeDtypeStruct((m, GLA_WIDTH), F32),
    ]
    out_specs = [
        pl.BlockSpec((MLA_HEADS, tm, QK_WIDTH), lambda i: (0, i, 0)),
        pl.BlockSpec((tm, QK_WIDTH), row),
        pl.BlockSpec((tm, KV_RANK), row),
        pl.BlockSpec((tm, MLA_ROPE), row),
        pl.BlockSpec((tm, GLA_KW), row),
        pl.BlockSpec((tm, GLA_KW), row),
        pl.BlockSpec((tm, GLA_WIDTH), row),
        pl.BlockSpec((tm, GLA_KW), row),
        pl.BlockSpec((tm, GLA_WIDTH), row),
    ]
    if with_ckv_t:
        out_shape.append(jax.ShapeDtypeStruct((m // KEY_TILE, KV_RANK, KEY_TILE), BF16))
        out_specs.append(pl.BlockSpec((tm // KEY_TILE, KV_RANK, KEY_TILE), const3_row))
    return pl.pallas_call(
        _proj_kernel, grid=grid, in_specs=in_specs, out_specs=out_specs, out_shape=out_shape,
        compiler_params=pltpu.CompilerParams(
            dimension_semantics=("arbitrary",), vmem_limit_bytes=VMEM_LIMIT),
        name="proj",
    )(x2d, cos_t, sin_t, w['w_in'], w['q_norm'], w['w_uq'], w['kv_norm'], w['w_uk'],
      w['w_gate2'], w['b_gate'])


def _mla_prompt_kernel(q_ref, k_ref, vt_ref, wuv_ref, o_ref, m_scr, l_scr, acc_scr,
                       s_a, s_b, *, tq):
    i = pl.program_id(1)
    rows = MLA_HEADS * tq
    size = STEP_TILES * KEY_TILE
    hw = STREAM_HEADS * tq
    n_streams = MLA_HEADS // STREAM_HEADS
    m_scr[...] = jnp.full(m_scr.shape, -jnp.inf, F32)
    l_scr[...] = jnp.zeros(l_scr.shape, F32)
    acc_scr[...] = jnp.zeros(acc_scr.shape, F32)

    def scores_to(s_ref, j):
        start = pl.multiple_of(j * size, size)
        kblk = k_ref[pl.ds(start, size), :]
        for g in range(n_streams):
            qg = q_ref[g * STREAM_HEADS:(g + 1) * STREAM_HEADS].reshape(hw, QK_WIDTH)
            s_ref[:, g * hw:(g + 1) * hw] = _dot_nt(kblk, qg)

    def fold(s_ref, j, masked):
        tile0 = j * STEP_TILES
        if masked:
            key = j * size + lax.broadcasted_iota(jnp.int32, (size, hw), 0)
            tok = i * tq + (lax.broadcasted_iota(jnp.int32, (size, hw), 1) & (tq - 1))
            visible = key <= tok
        m_all = m_scr[...]
        l_all = l_scr[...]
        m_out, l_out, updates = [], [], []
        for g in range(n_streams):
            lanes = slice(g * hw, (g + 1) * hw)
            st = s_ref[:, lanes]
            if masked:
                st = jnp.where(visible, st, -jnp.inf)
            m_old = m_all[:, lanes]
            m_new = jnp.maximum(m_old, jnp.max(st, axis=0, keepdims=True))
            p = jnp.exp(st - m_new)
            alpha = jnp.exp(m_old - m_new)
            l_out.append(alpha * l_all[:, lanes] + jnp.sum(p, axis=0, keepdims=True))
            m_out.append(m_new)
            pb = p.astype(BF16)
            pv = _dot(vt_ref[tile0], pb[0:KEY_TILE])
            for t in range(1, STEP_TILES):
                pv = pv + _dot(vt_ref[tile0 + t], pb[t * KEY_TILE:(t + 1) * KEY_TILE])
            updates.append((lanes, alpha, pv))
        for lanes, alpha, pv in updates:
            acc_scr[:, lanes] = alpha * acc_scr[:, lanes] + pv
        m_scr[...] = jnp.concatenate(m_out, axis=1)
        l_scr[...] = jnp.concatenate(l_out, axis=1)

    last = (i * tq) // size
    scores_to(s_a, 0)

    def two_steps(k, carry):
        scores_to(s_b, 2 * k + 1)
        fold(s_a, 2 * k, masked=False)
        scores_to(s_a, 2 * k + 2)
        fold(s_b, 2 * k + 1, masked=False)
        return carry

    lax.fori_loop(0, last // 2, two_steps, 0)

    @pl.when(last % 2 == 1)
    def _():
        scores_to(s_b, last)
        fold(s_a, last - 1, masked=False)
        fold(s_b, last, masked=True)

    @pl.when(last % 2 == 0)
    def _():
        fold(s_a, last, masked=True)

    o_lat_t = (acc_scr[...] / l_scr[...]).astype(BF16)
    for h in range(MLA_HEADS):
        o_ref[:, h * MLA_V:(h + 1) * MLA_V] = _dot_tn(
            o_lat_t[:, h * tq:(h + 1) * tq], wuv_ref[h]).astype(o_ref.dtype)


def _mla_prompt(q, kcat, ckv_t, w_uv, batch, seq):
    tq = KEY_TILE
    nq = seq // tq
    kern = functools.partial(_mla_prompt_kernel, tq=tq)
    rows = MLA_HEADS * tq
    return pl.pallas_call(
        kern, grid=(batch, nq),
        in_specs=[
            pl.BlockSpec((MLA_HEADS, tq, QK_WIDTH), lambda b, i: (0, b * nq + i, 0)),
            pl.BlockSpec((seq, QK_WIDTH), lambda b, i: (b, 0)),
            pl.BlockSpec((seq // KEY_TILE, KV_RANK, KEY_TILE), lambda b, i: (b, 0, 0)),
            pl.BlockSpec((MLA_HEADS, KV_RANK, MLA_V), lambda b, i: (0, 0, 0)),
        ],
        out_specs=pl.BlockSpec((tq, MLA_HEADS * MLA_V), lambda b, i: (b * nq + i, 0)),
        out_shape=jax.ShapeDtypeStruct((batch * seq, MLA_HEADS * MLA_V), BF16),
        scratch_shapes=[pltpu.VMEM((1, rows), F32), pltpu.VMEM((1, rows), F32),
                        pltpu.VMEM((KV_RANK, rows), F32),
                        pltpu.VMEM((STEP_TILES * KEY_TILE, rows), F32),
                        pltpu.VMEM((STEP_TILES * KEY_TILE, rows), F32)],
        compiler_params=pltpu.CompilerParams(
            dimension_semantics=("arbitrary", "arbitrary"), vmem_limit_bytes=VMEM_LIMIT),
        name="mla_prompt",
    )(q, kcat, ckv_t, w_uv)


SAMPLE_Q_ROWS = 16
PAGES_PER_CHUNK = 32
SAMPLE_SLOTS = 4
SAMPLE_PREFETCH = SAMPLE_SLOTS - 1


def _mla_sample_kernel(pt_ref, q_ref, knew_ref, wuv_ref, ckv_hbm, krt_hbm, o_ref,
                       cbuf, kbuf, cbf, sem_c, sem_k, *, n_batch, n_chunks):
    b = pl.program_id(0)
    g_pages = PAGES_PER_CHUNK

    def page_copies(bb, c, slot, g):
        page = pt_ref[bb, c * g_pages + g]
        tok = pl.ds(g * PAGE_SIZE, PAGE_SIZE)
        return (pltpu.make_async_copy(ckv_hbm.at[page], cbuf.at[slot, tok, :], sem_c.at[slot]),
                pltpu.make_async_copy(krt_hbm.at[page], kbuf.at[slot, :, tok], sem_k.at[slot]))

    def start_chunk(bb, c, slot):
        for g in range(g_pages):
            cc, ck = page_copies(bb, c, slot, g)
            cc.start()
            ck.start()

    def wait_chunk(bb, c, slot):
        for g in range(g_pages):
            cc, ck = page_copies(bb, c, slot, g)
            cc.wait()
            ck.wait()

    def start_ahead(c):
        cn = c + SAMPLE_PREFETCH
        if cn < n_chunks:
            start_chunk(b, cn, cn % SAMPLE_SLOTS)
        else:
            @pl.when(b + 1 < n_batch)
            def _():
                start_chunk(b + 1, cn - n_chunks, (cn - n_chunks) % SAMPLE_SLOTS)

    @pl.when(b == 0)
    def _():
        for c0 in range(SAMPLE_PREFETCH):
            start_chunk(0, c0, c0)

    q = q_ref[0]
    q_lat = q[:, :KV_RANK]
    q_rope = q[:, KV_RANK:KV_RANK + MLA_ROPE]
    m = jnp.full((SAMPLE_Q_ROWS, 1), -jnp.inf, F32)
    l = jnp.zeros((SAMPLE_Q_ROWS, 1), F32)
    acc = jnp.zeros((SAMPLE_Q_ROWS, KV_RANK), F32)

    n_tiles = g_pages * PAGE_SIZE // KEY_TILE

    def tiled_pv(p_b, cbf_slot):
        out = _dot(p_b[:, 0:KEY_TILE], cbf[cbf_slot, 0:KEY_TILE, :])
        for t in range(1, n_tiles):
            tok = slice(t * KEY_TILE, (t + 1) * KEY_TILE)
            out = out + _dot(p_b[:, tok], cbf[cbf_slot, tok, :])
        return out

    pending = None
    for c in range(n_chunks):
        slot = c % SAMPLE_SLOTS
        wait_chunk(b, c, slot)
        if pending is not None:
            alpha_p, p_p, slot_p = pending
            acc = alpha_p * acc + tiled_pv(p_p, slot_p)
        s_tiles = []
        for t in range(n_tiles):
            tok = slice(t * KEY_TILE, (t + 1) * KEY_TILE)
            cb = cbuf[slot, tok, :].astype(BF16)
            cbf[c % 2, tok, :] = cb
            kb = kbuf[slot, :, tok].astype(BF16)
            s_tiles.append(_dot_nt(q_lat, cb) + _dot(q_rope, kb))
        s = jnp.concatenate(s_tiles, axis=1)
        start_ahead(c)
        m_new = jnp.maximum(m, jnp.max(s, axis=-1, keepdims=True))
        p = jnp.exp(s - m_new)
        alpha = jnp.exp(m - m_new)
        l = alpha * l + jnp.sum(p, axis=-1, keepdims=True)
        m = m_new
        pending = (alpha, p.astype(BF16), c % 2)
    alpha_p, p_p, slot_p = pending
    acc = alpha_p * acc + tiled_pv(p_p, slot_p)


    knew = knew_ref[0].astype(F32)
    s_self = jnp.sum(q.astype(F32) * knew, axis=-1, keepdims=True)
    m_new = jnp.maximum(m, s_self)
    p_self = jnp.exp(s_self - m_new)
    alpha = jnp.exp(m - m_new)
    l = alpha * l + p_self
    acc = alpha * acc + p_self.astype(BF16).astype(F32) * knew[:, :KV_RANK]
    o_lat = (acc / l).astype(BF16)
    for h in range(MLA_HEADS):
        res = _dot(o_lat, wuv_ref[h])
        o_ref[0, :, h * MLA_V:(h + 1) * MLA_V] = res[h:h + 1, :]


def _mla_sample(page_table, q_s, knew, w_uv, cache_ckv, cache_krope_t):
    n_batch, n_pages = page_table.shape
    n_chunks = n_pages // PAGES_PER_CHUNK
    chunk_rows = PAGES_PER_CHUNK * PAGE_SIZE
    assert n_chunks * PAGES_PER_CHUNK == n_pages and n_chunks % SAMPLE_SLOTS == 0
    assert SAMPLE_PREFETCH <= n_chunks
    kern = functools.partial(_mla_sample_kernel, n_batch=n_batch, n_chunks=n_chunks)
    grid_spec = pltpu.PrefetchScalarGridSpec(
        num_scalar_prefetch=1,
        grid=(n_batch,),
        in_specs=[
            pl.BlockSpec((1, SAMPLE_Q_ROWS, QK_WIDTH), lambda b, pt: (b, 0, 0)),
            pl.BlockSpec((1, 1, QK_WIDTH), lambda b, pt: (b, 0, 0)),
            pl.BlockSpec((MLA_HEADS, KV_RANK, MLA_V), lambda b, pt: (0, 0, 0)),
            pl.BlockSpec(memory_space=pl.ANY),
            pl.BlockSpec(memory_space=pl.ANY),
        ],
        out_specs=pl.BlockSpec((1, 1, MLA_HEADS * MLA_V), lambda b, pt: (b, 0, 0)),
        scratch_shapes=[
            pltpu.VMEM((SAMPLE_SLOTS, chunk_rows, KV_RANK), F32),
            pltpu.VMEM((SAMPLE_SLOTS, MLA_ROPE, chunk_rows), F32),
            pltpu.VMEM((2, chunk_rows, KV_RANK), BF16),
            pltpu.SemaphoreType.DMA((SAMPLE_SLOTS,)),
            pltpu.SemaphoreType.DMA((SAMPLE_SLOTS,)),
        ],
    )
    return pl.pallas_call(
        kern, grid_spec=grid_spec,
        out_shape=jax.ShapeDtypeStruct((n_batch, 1, MLA_HEADS * MLA_V), F32),
        compiler_params=pltpu.CompilerParams(
            dimension_semantics=("arbitrary",), vmem_limit_bytes=VMEM_LIMIT),
        name="mla_sample",
    )(page_table, q_s, knew, w_uv, cache_ckv, cache_krope_t)


GLA_UNROLL = 4


def _split3(x):
    hi = x.astype(BF16)
    r1 = x - hi.astype(F32)
    mid = r1.astype(BF16)
    lo = (r1 - mid.astype(F32)).astype(BF16)
    return hi, mid, lo


def _gla_gate_out(o, gg, gnorm):
    return _rms(o, gnorm) * (gg * jax.nn.sigmoid(gg))


def _gla_prompt_kernel(gq_ref, gk_ref, gv_ref, la_ref, gg_ref, gn_ref, o_ref, s_out_ref,
                       s_scr, *, n_sub):
    t = pl.program_id(1)
    cs = GLA_CHUNK

    @pl.when(t == 0)
    def _():
        s_scr[...] = jnp.zeros(s_scr.shape, F32)

    ri = lax.broadcasted_iota(jnp.int32, (cs, cs), 0)
    ci = lax.broadcasted_iota(jnp.int32, (cs, cs), 1)
    tri = ri >= ci
    tri_b = tri.astype(BF16)
    lane_head = lax.broadcasted_iota(jnp.int32, (cs, GLA_KW), 1) // GLA_DK
    gnorm = gn_ref[...]

    def chunk(c, s_prev):
        r0 = pl.multiple_of(c * cs, cs)
        g = la_ref[pl.ds(r0, cs), :]
        g_hi, g_mid, g_lo = _split3(g)
        bcum = _dot(tri_b, g_hi) + _dot(tri_b, g_mid) + _dot(tri_b, g_lo)
        b_last = bcum[cs - 1:cs, :]
        qf = gq_ref[pl.ds(r0, cs), :]
        kf = gk_ref[pl.ds(r0, cs), :]
        v = gv_ref[pl.ds(r0, cs), :]
        q_t = qf * jnp.exp(bcum)
        k_t = (kf * jnp.exp(-bcum)).astype(BF16)
        k_h = (kf * jnp.exp(b_last - bcum)).astype(BF16)
        decay = jnp.exp(b_last)
        q_stack = jnp.concatenate(
            [jnp.where(lane_head == h, q_t, 0.0) for h in range(GLA_HEADS)],
            axis=0).astype(BF16)
        a_stack = _dot_nt(q_stack, k_t)
        o_inter = _dot(q_stack, s_prev.astype(BF16))
        kv = _dot_tn(k_h, v)
        decay_col = jnp.transpose(jnp.broadcast_to(decay, (GLA_DV, GLA_KW)))
        kv_diag = jnp.concatenate(
            [kv[h * GLA_DK:(h + 1) * GLA_DK, h * GLA_DV:(h + 1) * GLA_DV]
             for h in range(GLA_HEADS)], axis=0)
        s_new = decay_col * s_prev + kv_diag
        for h in range(GLA_HEADS):
            a_h = jnp.where(tri, a_stack[h * cs:(h + 1) * cs], 0.0).astype(BF16)
            o_h = _dot(a_h, v[:, h * GLA_DV:(h + 1) * GLA_DV]) + o_inter[h * cs:(h + 1) * cs]
            gg_h = gg_ref[pl.ds(r0, cs), h * GLA_DV:(h + 1) * GLA_DV]
            o_ref[pl.ds(r0, cs), h * GLA_DV:(h + 1) * GLA_DV] = _gla_gate_out(
                o_h, gg_h, gnorm).astype(o_ref.dtype)
        return s_new

    def trip(j, carry):
        s = s_scr[...]
        for u in range(GLA_UNROLL):
            s = chunk(j * GLA_UNROLL + u, s)
        s_scr[...] = s
        return carry

    lax.fori_loop(0, n_sub // GLA_UNROLL, trip, 0)
    s_out_ref[0] = s_scr[...]


def _gla_prompt(gq, gk, gv, la, gg, gnorm, batch, seq, tc=1024):
    nt = seq // tc
    n_sub = tc // GLA_CHUNK
    row = lambda b, t: (b * nt + t, 0)
    kern = functools.partial(_gla_prompt_kernel, n_sub=n_sub)
    return pl.pallas_call(
        kern, grid=(batch, nt),
        in_specs=[
            pl.BlockSpec((tc, GLA_KW), row),
            pl.BlockSpec((tc, GLA_KW), row),
            pl.BlockSpec((tc, GLA_WIDTH), row),
            pl.BlockSpec((tc, GLA_KW), row),
            pl.BlockSpec((tc, GLA_WIDTH), row),
            pl.BlockSpec((1, GLA_DV), lambda b, t: (0, 0)),
        ],
        out_specs=[
            pl.BlockSpec((tc, GLA_WIDTH), row),
            pl.BlockSpec((1, GLA_KW, GLA_DV), lambda b, t: (b, 0, 0)),
        ],
        out_shape=[
            jax.ShapeDtypeStruct((batch * seq, GLA_WIDTH), BF16),
            jax.ShapeDtypeStruct((batch, GLA_KW, GLA_DV), F32),
        ],
        scratch_shapes=[pltpu.VMEM((GLA_KW, GLA_DV), F32)],
        compiler_params=pltpu.CompilerParams(
            dimension_semantics=("arbitrary", "arbitrary"), vmem_limit_bytes=VMEM_LIMIT),
        name="gla_prompt",
    )(gq, gk, gv, la, gg, gnorm)


def _gla_step_kernel(s_ref, gq_ref, gk_ref, gv_ref, la_ref, gg_ref, gn_ref, o_ref, s_out_ref,
                     *, bb):
    gnorm = gn_ref[...]

    def col(x_row):
        return jnp.transpose(jnp.broadcast_to(x_row, (GLA_DV, GLA_KW)))

    for i in range(bb):
        s = s_ref[i]
        e_col = col(jnp.exp(la_ref[i:i + 1, :]))
        k_col = col(gk_ref[i:i + 1, :])
        q_col = col(gq_ref[i:i + 1, :].astype(BF16).astype(F32))
        v_row = gv_ref[i:i + 1, :].astype(F32)
        v_rows = jnp.concatenate(
            [jnp.broadcast_to(v_row[:, h * GLA_DV:(h + 1) * GLA_DV], (GLA_DK, GLA_DV))
             for h in range(GLA_HEADS)], axis=0)
        s_new = e_col * s + k_col * v_rows
        s_out_ref[i] = s_new
        prod = q_col * s_new.astype(BF16).astype(F32)
        for h in range(GLA_HEADS):
            o_h = jnp.sum(prod[h * GLA_DK:(h + 1) * GLA_DK], axis=0, keepdims=True)
            gg_h = gg_ref[i:i + 1, h * GLA_DV:(h + 1) * GLA_DV]
            o_ref[i:i + 1, h * GLA_DV:(h + 1) * GLA_DV] = _gla_gate_out(o_h, gg_h, gnorm)


def _gla_step(state, gq, gk, gv, la, gg, gnorm, bb=8):
    n = state.shape[0]
    row = lambda i: (i, 0)
    kern = functools.partial(_gla_step_kernel, bb=bb)
    return pl.pallas_call(
        kern, grid=(n // bb,),
        in_specs=[
            pl.BlockSpec((bb, GLA_KW, GLA_DV), lambda i: (i, 0, 0)),
            pl.BlockSpec((bb, GLA_KW), row),
            pl.BlockSpec((bb, GLA_KW), row),
            pl.BlockSpec((bb, GLA_WIDTH), row),
            pl.BlockSpec((bb, GLA_KW), row),
            pl.BlockSpec((bb, GLA_WIDTH), row),
            pl.BlockSpec((1, GLA_DV), lambda i: (0, 0)),
        ],
        out_specs=[
            pl.BlockSpec((bb, GLA_WIDTH), row),
            pl.BlockSpec((bb, GLA_KW, GLA_DV), lambda i: (i, 0, 0)),
        ],
        out_shape=[
            jax.ShapeDtypeStruct((n, GLA_WIDTH), F32),
            jax.ShapeDtypeStruct((n, GLA_KW, GLA_DV), F32),
        ],
        compiler_params=pltpu.CompilerParams(
            dimension_semantics=("arbitrary",), vmem_limit_bytes=VMEM_LIMIT),
        name="gla_step",
    )(state, gq, gk, gv, la, gg, gnorm)


FF_CHUNK = 1024
POST_SUBBLOCKS = 2


def _post_kernel(x_ref, mla_ref, gla_ref, wout_ref, g1_ref, b1_ref, w1_ref, w2_ref,
                 g2_ref, b2_ref, y_ref):
    half = MLA_HEADS * MLA_V
    tm = x_ref.shape[0]
    n_sub = POST_SUBBLOCKS if tm >= POST_SUBBLOCKS * KEY_TILE else 1
    sub = tm // n_sub

    def front(r):
        mix = (_dot(mla_ref[r, :].astype(BF16), wout_ref[0:half, :])
               + _dot(gla_ref[r, :].astype(BF16), wout_ref[half:, :]))
        return _layer_norm(ALPHA * x_ref[r, :] + mix, g1_ref[...], b1_ref[...])

    rows = [pl.ds(k * sub, sub) for k in range(n_sub)]
    x1s = [front(r) for r in rows]
    for r, x1 in zip(rows, x1s):
        x1b = x1.astype(BF16)
        acc = jnp.zeros(x1.shape, F32)
        for c in range(D_FF // FF_CHUNK):
            hmid = _dot(x1b, w1_ref[:, c * FF_CHUNK:(c + 1) * FF_CHUNK])
            hmid = jnp.square(jnp.maximum(hmid, 0.0)).astype(BF16)
            acc = acc + _dot(hmid, w2_ref[c * FF_CHUNK:(c + 1) * FF_CHUNK, :])
        y_ref[r, :] = _layer_norm(ALPHA * x1 + acc, g2_ref[...], b2_ref[...])


def _post(x2d, mla, gla, w, tm):
    m = x2d.shape[0]
    row = lambda i: (i, 0)
    const = lambda i: (0, 0)
    resident = dict(pipeline_mode=pl.Buffered(1))
    return pl.pallas_call(
        _post_kernel, grid=(m // tm,),
        in_specs=[
            pl.BlockSpec((tm, D_MODEL), row),
            pl.BlockSpec((tm, MLA_HEADS * MLA_V), row),
            pl.BlockSpec((tm, GLA_WIDTH), row),
            pl.BlockSpec((D_MODEL, D_MODEL), const, **resident),
            pl.BlockSpec((1, D_MODEL), const),
            pl.BlockSpec((1, D_MODEL), const),
            pl.BlockSpec((D_MODEL, D_FF), const, **resident),
            pl.BlockSpec((D_FF, D_MODEL), const, **resident),
            pl.BlockSpec((1, D_MODEL), const),
            pl.BlockSpec((1, D_MODEL), const),
        ],
        out_specs=pl.BlockSpec((tm, D_MODEL), row),
        out_shape=jax.ShapeDtypeStruct((m, D_MODEL), F32),
        compiler_params=pltpu.CompilerParams(
            dimension_semantics=("arbitrary",), vmem_limit_bytes=VMEM_LIMIT),
        name="post",
    )(x2d, mla, gla, w['w_out'], w['ln1_g'], w['ln1_b'], w['w1'], w['w2'],
      w['ln2_g'], w['ln2_b'])


def _permute_w_in(w):
    sizes = (Q_RANK, KV_RANK, MLA_ROPE, GLA_KW, GLA_KW, GLA_WIDTH, GATE_RANK, GLA_WIDTH)
    off = np.concatenate([[0], np.cumsum(sizes)]).tolist()
    cq, ckv, kr, gq, gk, gv, gr, gg = [w[:, off[i]:off[i + 1]] for i in range(8)]
    half = MLA_ROPE // 2
    pad = jnp.zeros((w.shape[0], LANES - GATE_RANK), w.dtype)
    return jnp.concatenate(
        [cq, ckv, gq, gk, gv, gg, kr, kr[:, half:], kr[:, :half], gr, pad], axis=1).astype(BF16)


def _permute_w_uq(w):
    per_head = MLA_NOPE + MLA_ROPE
    half = MLA_ROPE // 2
    nope, rope, rope_sw = [], [], []
    for h in range(MLA_HEADS):
        base = h * per_head
        nope.append(w[:, base:base + MLA_NOPE])
        r = w[:, base + MLA_NOPE:base + per_head]
        rope.append(r)
        rope_sw += [r[:, half:], r[:, :half]]
    return jnp.concatenate(nope + rope + rope_sw, axis=1).astype(BF16)


def _prep_weights(w_in, mla_q_norm, mla_w_uq, mla_kv_norm, mla_w_uk, mla_w_uv,
                  gla_w_gate2, gla_b_gate, gla_norm, w_out, ln1_g, ln1_b,
                  mlp_w1, mlp_w2, ln2_g, ln2_b, l):
    w_in_p = _permute_w_in(w_in[l])
    w_uq_p = _permute_w_uq(mla_w_uq[l])
    return dict(
        w_in=w_in_p,
        q_norm=mla_q_norm[l][None, :],
        w_uq=w_uq_p,
        kv_norm=mla_kv_norm[l][None, :],
        w_uk=jnp.transpose(mla_w_uk[l], (1, 2, 0)).astype(BF16),
        w_uv=jnp.transpose(mla_w_uv[l], (1, 0, 2)).astype(BF16),
        w_gate2=jnp.pad(gla_w_gate2[l], ((0, LANES - GATE_RANK), (0, 0))).astype(BF16),
        b_gate=gla_b_gate[l][None, :],
        gla_norm=gla_norm[l][None, :],
        w_out=w_out[l].astype(BF16),
        ln1_g=ln1_g[l][None, :], ln1_b=ln1_b[l][None, :],
        w1=mlp_w1[l].astype(BF16), w2=mlp_w2[l].astype(BF16),
        ln2_g=ln2_g[l][None, :], ln2_b=ln2_b[l][None, :],
    )


def _rope_tables(pos):
    inv = ROPE_THETA ** (-jnp.arange(0, MLA_ROPE, 2, dtype=F32) / MLA_ROPE)
    ang = pos.astype(F32)[:, None] * inv[None, :]
    cos, sin = jnp.cos(ang), jnp.sin(ang)
    return (jnp.concatenate([cos, cos, cos, cos], axis=-1),
            jnp.concatenate([-sin, sin, -sin, sin], axis=-1))


def kernel(x_prompt, x_sample, cache_ckv, cache_krope, state_gla, page_table, w_in,
           mla_q_norm, mla_w_uq, mla_kv_norm, mla_w_uk, mla_w_uv, gla_w_gate2, gla_b_gate,
           gla_norm, w_out, ln1_g, ln1_b, mlp_w1, mlp_w2, ln2_g, ln2_b):
    assert w_in.shape[0] == DEPTH == 1
    batch, seq, _ = x_prompt.shape
    n_dec, t_new, _ = x_sample.shape
    assert t_new == 1
    l = 0
    w = _prep_weights(w_in, mla_q_norm, mla_w_uq, mla_kv_norm, mla_w_uk, mla_w_uv,
                      gla_w_gate2, gla_b_gate, gla_norm, w_out, ln1_g, ln1_b,
                      mlp_w1, mlp_w2, ln2_g, ln2_b, l)

    xp = x_prompt.reshape(batch * seq, D_MODEL)
    cos_p, sin_p = _rope_tables(jnp.arange(seq, dtype=jnp.int32))
    q, kcat, ckv_p, kr_p, gq, gk, gv, la, gg, ckv_t = _proj(
        xp, cos_p, sin_p, w, 512, BF16, True)
    mla_p = _mla_prompt(q, kcat, ckv_t, w['w_uv'], batch, seq)
    gla_p, s_p = _gla_prompt(gq, gk, gv, la, gg, w['gla_norm'], batch, seq)
    y_p = _post(xp, mla_p, gla_p, w, tm=512)

    xs = x_sample.reshape(n_dec, D_MODEL)
    cos_s, sin_s = _rope_tables(jnp.full((n_dec,), PAST_LEN, dtype=jnp.int32))
    q, kcat, ckv_s, kr_s, gq, gk, gv, la, gg = _proj(xs, cos_s, sin_s, w, n_dec, F32, False)
    q_s = jnp.pad(jnp.transpose(q, (1, 0, 2)),
                  ((0, 0), (0, SAMPLE_Q_ROWS - MLA_HEADS), (0, 0)))
    krope_t = jnp.swapaxes(cache_krope[l], 1, 2)
    mla_s = _mla_sample(page_table, q_s, kcat[:, None, :], w['w_uv'],
                        cache_ckv[l], krope_t)
    gla_s, s_s = _gla_step(state_gla[l].reshape(n_dec, GLA_KW, GLA_DV),
                           gq, gk, gv, la, gg, w['gla_norm'])
    y_s = _post(xs, mla_s.reshape(n_dec, MLA_HEADS * MLA_V), gla_s, w, tm=n_dec)

    return (y_p.reshape(batch, seq, D_MODEL),
            y_s.reshape(n_dec, 1, D_MODEL),
            ckv_p.reshape(1, batch, seq, KV_RANK),
            kr_p.reshape(1, batch, seq, MLA_ROPE),
            s_p.reshape(1, batch, GLA_HEADS, GLA_DK, GLA_DV),
            ckv_s.reshape(1, n_dec, 1, KV_RANK),
            kr_s.reshape(1, n_dec, 1, MLA_ROPE),
            s_s.reshape(1, n_dec, GLA_HEADS, GLA_DK, GLA_DV))
```

```python
import functools

import numpy as np
import jax
import jax.numpy as jnp
from jax import lax
from jax.experimental import pallas as pl
from jax.experimental.pallas import tpu as pltpu

D_MODEL = 1024
PAST_LEN = 16384
PAGE_SIZE = 128
MLA_HEADS = 4
MLA_V = 128
MLA_NOPE = 128
MLA_ROPE = 64
Q_RANK = 384
KV_RANK = 256
MLA_SCALE = (MLA_NOPE + MLA_ROPE) ** -0.5
ROPE_THETA = 10000.0
GLA_HEADS = 4
GLA_DV = 128
GLA_DK = 64
GLA_WIDTH = GLA_HEADS * GLA_DV
GLA_KW = GLA_HEADS * GLA_DK
GATE_RANK = 16
GATE_TAU = 16.0
GLA_CHUNK = 64
D_FF = 4 * D_MODEL
DEPTH = 1
ALPHA = (2.0 * DEPTH) ** 0.25
EPS = 1e-5

LANES = 128
QK_WIDTH = KV_RANK + LANES
VMEM_LIMIT = 56 * 1024 * 1024
KEY_TILE = 256
STREAM_HEADS = 2
STEP_TILES = 2

_C_CQ = 0
_C_CKV = _C_CQ + Q_RANK
_C_GQ = _C_CKV + KV_RANK
_C_GK = _C_GQ + GLA_KW
_C_GV = _C_GK + GLA_KW
_C_GG = _C_GV + GLA_WIDTH
_C_KR = _C_GG + GLA_WIDTH
_C_GR = _C_KR + 2 * MLA_ROPE
IN_COLS_P = _C_GR + LANES

BF16 = jnp.bfloat16
F32 = jnp.float32


def _dot(a, b):
    return jnp.dot(a, b, preferred_element_type=F32)


def _dot_nt(a, b):
    return lax.dot_general(a, b, (((1,), (1,)), ((), ())), preferred_element_type=F32)


def _dot_tn(a, b):
    return lax.dot_general(a, b, (((0,), (0,)), ((), ())), preferred_element_type=F32)


def _rms(x, g):
    return x * lax.rsqrt(jnp.mean(x * x, axis=-1, keepdims=True) + EPS) * g


def _layer_norm(x, g, b):
    mu = jnp.mean(x, axis=-1, keepdims=True)
    xc = x - mu
    var = jnp.mean(xc * xc, axis=-1, keepdims=True)
    return xc * lax.rsqrt(var + EPS) * g + b


def _proj_kernel(x_ref, cos_ref, sin_ref, win_ref, qn_ref, wuq_ref, kvn_ref, wuk_ref,
                 wg2_ref, bg_ref,
                 q_ref, kcat_ref, ckv_ref, kr_ref, gq_ref, gk_ref, gv_ref, la_ref, gg_ref,
                 ckvt_ref=None):
    tm = x_ref.shape[0]
    sub = KEY_TILE if tm % KEY_TILE == 0 else tm
    for k in range(tm // sub):
        _proj_rows(pl.ds(k * sub, sub), k, x_ref, cos_ref, sin_ref, win_ref, qn_ref, wuq_ref,
                   kvn_ref, wuk_ref, wg2_ref, bg_ref, q_ref, kcat_ref, ckv_ref, kr_ref,
                   gq_ref, gk_ref, gv_ref, la_ref, gg_ref, ckvt_ref)


def _proj_rows(r, k, x_ref, cos_ref, sin_ref, win_ref, qn_ref, wuq_ref, kvn_ref, wuk_ref,
               wg2_ref, bg_ref, q_ref, kcat_ref, ckv_ref, kr_ref, gq_ref, gk_ref, gv_ref,
               la_ref, gg_ref, ckvt_ref):
    xb = x_ref[r, :].astype(BF16)
    cos = cos_ref[r, :]
    sin = sin_ref[r, :]

    cq = _dot(xb, win_ref[:, _C_CQ:_C_CQ + Q_RANK])
    cqn = _rms(cq, qn_ref[...]).astype(BF16)
    q = _dot(cqn, wuq_ref[...])
    nh = MLA_HEADS * MLA_NOPE
    nr = MLA_HEADS * MLA_ROPE
    low_half = lax.broadcasted_iota(jnp.int32, (1, LANES), 1) < MLA_ROPE
    for h in range(MLA_HEADS):
        nope = q[:, h * MLA_NOPE:(h + 1) * MLA_NOPE].astype(BF16)
        q_lat = _dot(nope, wuk_ref[h])
        q_ref[h, r, 0:KV_RANK] = (q_lat * MLA_SCALE).astype(BF16)
    for pair in range(MLA_HEADS // 2):
        lanes = slice(nh + pair * LANES, nh + (pair + 1) * LANES)
        lanes_sw = slice(nh + nr + pair * LANES, nh + nr + (pair + 1) * LANES)
        rope2 = (q[:, lanes] * cos + q[:, lanes_sw] * sin) * MLA_SCALE
        for j, blk in enumerate((rope2, pltpu.roll(rope2, MLA_ROPE, 1))):
            q_ref[2 * pair + j, r, KV_RANK:QK_WIDTH] = jnp.where(low_half, blk, 0.0).astype(BF16)

    ckv = _rms(_dot(xb, win_ref[:, _C_CKV:_C_CKV + KV_RANK]), kvn_ref[...])
    ckv_ref[r, :] = ckv
    kr_gr = _dot(xb, win_ref[:, _C_KR:_C_GR + LANES])
    krr = kr_gr[:, :2 * MLA_ROPE]
    k_rope = (krr[:, :MLA_ROPE] * cos[:, :MLA_ROPE]
              + krr[:, MLA_ROPE:] * sin[:, :MLA_ROPE])
    kr_ref[r, :] = k_rope
    kcat_ref[r, 0:KV_RANK] = ckv.astype(BF16)
    kcat_ref[r, KV_RANK:QK_WIDTH] = jnp.concatenate(
        [k_rope, jnp.zeros_like(k_rope)], axis=-1).astype(BF16)
    if ckvt_ref is not None:
        ckvt_ref[k] = jnp.transpose(ckv).astype(BF16)

    gq_ref[r, :] = _dot(xb, win_ref[:, _C_GQ:_C_GQ + GLA_KW]) * (GLA_DK ** -0.5)
    gk_ref[r, :] = _dot(xb, win_ref[:, _C_GK:_C_GK + GLA_KW])
    gv_ref[r, :] = _dot(xb, win_ref[:, _C_GV:_C_GV + GLA_WIDTH]).astype(gv_ref.dtype)
    gg_ref[r, :] = _dot(xb, win_ref[:, _C_GG:_C_GG + GLA_WIDTH])
    gr = kr_gr[:, 2 * MLA_ROPE:].astype(BF16)
    z = _dot(gr, wg2_ref[...]) + bg_ref[...]
    la_ref[r, :] = (jnp.minimum(z, 0.0) - jnp.log1p(jnp.exp(-jnp.abs(z)))) / GATE_TAU


def _proj(x2d, cos_t, sin_t, w, tm, gv_dtype, with_ckv_t):
    m = x2d.shape[0]
    grid = (m // tm,)
    row = lambda i: (i, 0)
    const2 = lambda i: (0, 0)
    const3 = lambda i: (0, 0, 0)
    const3_row = lambda i: (i, 0, 0)
    pos_blocks = cos_t.shape[0] // tm
    pos_row = lambda i: (i % pos_blocks, 0)
    in_specs = [
        pl.BlockSpec((tm, D_MODEL), row),
        pl.BlockSpec((tm, LANES), pos_row),
        pl.BlockSpec((tm, LANES), pos_row),
        pl.BlockSpec((D_MODEL, IN_COLS_P), const2),
        pl.BlockSpec((1, Q_RANK), const2),
        pl.BlockSpec(w['w_uq'].shape, const2),
        pl.BlockSpec((1, KV_RANK), const2),
        pl.BlockSpec((MLA_HEADS, MLA_NOPE, KV_RANK), const3),
        pl.BlockSpec((LANES, GLA_KW), const2),
        pl.BlockSpec((1, GLA_KW), const2),
    ]
    out_shape = [
        jax.ShapeDtypeStruct((MLA_HEADS, m, QK_WIDTH), BF16),
        jax.ShapeDtypeStruct((m, QK_WIDTH), BF16),
        jax.ShapeDtypeStruct((m, KV_RANK), F32),
        jax.ShapeDtypeStruct((m, MLA_ROPE), F32),
        jax.ShapeDtypeStruct((m, GLA_KW), F32),
        jax.ShapeDtypeStruct((m, GLA_KW), F32),
        jax.ShapeDtypeStruct((m, GLA_WIDTH), gv_dtype),
        jax.ShapeDtypeStruct((m, GLA_KW), F32),
        jax.ShapeDtypeStruct((m, GLA_WIDTH), F32),
    ]
    out_specs = [
        pl.BlockSpec((MLA_HEADS, tm, QK_WIDTH), lambda i: (0, i, 0)),
        pl.BlockSpec((tm, QK_WIDTH), row),
        pl.BlockSpec((tm, KV_RANK), row),
        pl.BlockSpec((tm, MLA_ROPE), row),
        pl.BlockSpec((tm, GLA_KW), row),
        pl.BlockSpec((tm, GLA_KW), row),
        pl.BlockSpec((tm, GLA_WIDTH), row),
        pl.BlockSpec((tm, GLA_KW), row),
        pl.BlockSpec((tm, GLA_WIDTH), row),
    ]
    if with_ckv_t:
        out_shape.append(jax.ShapeDtypeStruct((m // KEY_TILE, KV_RANK, KEY_TILE), BF16))
        out_specs.append(pl.BlockSpec((tm // KEY_TILE, KV_RANK, KEY_TILE), const3_row))
    return pl.pallas_call(
        _proj_kernel, grid=grid, in_specs=in_specs, out_specs=out_specs, out_shape=out_shape,
        compiler_params=pltpu.CompilerParams(
            dimension_semantics=("arbitrary",), vmem_limit_bytes=VMEM_LIMIT),
        name="proj",
    )(x2d, cos_t, sin_t, w['w_in'], w['q_norm'], w['w_uq'], w['kv_norm'], w['w_uk'],
      w['w_gate2'], w['b_gate'])


def _mla_prompt_kernel(q_ref, k_ref, vt_ref, wuv_ref, o_ref, m_scr, l_scr, acc_scr,
                       s_a, s_b, *, tq):
    i = pl.program_id(1)
    rows = MLA_HEADS * tq
    size = STEP_TILES * KEY_TILE
    hw = STREAM_HEADS * tq
    n_streams = MLA_HEADS // STREAM_HEADS
    m_scr[...] = jnp.full(m_scr.shape, -jnp.inf, F32)
    l_scr[...] = jnp.zeros(l_scr.shape, F32)
    acc_scr[...] = jnp.zeros(acc_scr.shape, F32)

    def scores_to(s_ref, j):
        start = pl.multiple_of(j * size, size)
        kblk = k_ref[pl.ds(start, size), :]
        for g in range(n_streams):
            qg = q_ref[g * STREAM_HEADS:(g + 1) * STREAM_HEADS].reshape(hw, QK_WIDTH)
            s_ref[g] = _dot_nt(kblk, qg)

    def fold(s_ref, j, masked):
        tile0 = j * STEP_TILES
        if masked:
            key = j * size + lax.broadcasted_iota(jnp.int32, (size, hw), 0)
            tok = i * tq + (lax.broadcasted_iota(jnp.int32, (size, hw), 1) & (tq - 1))
            visible = key <= tok
        m_all = m_scr[...]
        l_all = l_scr[...]
        m_out, l_out, updates = [], [], []
        for g in range(n_streams):
            lanes = slice(g * hw, (g + 1) * hw)
            st = s_ref[g]
            if masked:
                st = jnp.where(visible, st, -jnp.inf)
            m_old = m_all[:, lanes]
            m_new = jnp.maximum(m_old, jnp.max(st, axis=0, keepdims=True))
            p = jnp.exp(st - m_new)
            alpha = jnp.exp(m_old - m_new)
            l_out.append(alpha * l_all[:, lanes] + jnp.sum(p, axis=0, keepdims=True))
            m_out.append(m_new)
            pb = p.astype(BF16)
            pv = _dot(vt_ref[tile0], pb[0:KEY_TILE])
            for t in range(1, STEP_TILES):
                pv = pv + _dot(vt_ref[tile0 + t], pb[t * KEY_TILE:(t + 1) * KEY_TILE])
            updates.append((lanes, alpha, pv))
        for lanes, alpha, pv in updates:
            acc_scr[:, lanes] = alpha * acc_scr[:, lanes] + pv
        m_scr[...] = jnp.concatenate(m_out, axis=1)
        l_scr[...] = jnp.concatenate(l_out, axis=1)

    last = (i * tq) // size
    scores_to(s_a, 0)

    def two_steps(k, carry):
        scores_to(s_b, 2 * k + 1)
        fold(s_a, 2 * k, masked=False)
        scores_to(s_a, 2 * k + 2)
        fold(s_b, 2 * k + 1, masked=False)
        return carry

    lax.fori_loop(0, last // 2, two_steps, 0)

    @pl.when(last % 2 == 1)
    def _():
        scores_to(s_b, last)
        fold(s_a, last - 1, masked=False)
        fold(s_b, last, masked=True)

    @pl.when(last % 2 == 0)
    def _():
        fold(s_a, last, masked=True)

    o_lat_t = (acc_scr[...] / l_scr[...]).astype(BF16)
    for h in range(MLA_HEADS):
        o_ref[:, h * MLA_V:(h + 1) * MLA_V] = _dot_tn(
            o_lat_t[:, h * tq:(h + 1) * tq], wuv_ref[h]).astype(o_ref.dtype)


def _mla_prompt(q, kcat, ckv_t, w_uv, batch, seq):
    tq = KEY_TILE
    nq = seq // tq
    kern = functools.partial(_mla_prompt_kernel, tq=tq)
    rows = MLA_HEADS * tq
    score_tile = (MLA_HEADS // STREAM_HEADS, STEP_TILES * KEY_TILE, STREAM_HEADS * tq)
    return pl.pallas_call(
        kern, grid=(batch, nq),
        in_specs=[
            pl.BlockSpec((MLA_HEADS, tq, QK_WIDTH), lambda b, i: (0, b * nq + i, 0)),
            pl.BlockSpec((seq, QK_WIDTH), lambda b, i: (b, 0)),
            pl.BlockSpec((seq // KEY_TILE, KV_RANK, KEY_TILE), lambda b, i: (b, 0, 0)),
            pl.BlockSpec((MLA_HEADS, KV_RANK, MLA_V), lambda b, i: (0, 0, 0)),
        ],
        out_specs=pl.BlockSpec((tq, MLA_HEADS * MLA_V), lambda b, i: (b * nq + i, 0)),
        out_shape=jax.ShapeDtypeStruct((batch * seq, MLA_HEADS * MLA_V), BF16),
        scratch_shapes=[pltpu.VMEM((1, rows), F32), pltpu.VMEM((1, rows), F32),
                        pltpu.VMEM((KV_RANK, rows), F32),
                        pltpu.VMEM(score_tile, F32), pltpu.VMEM(score_tile, F32)],
        compiler_params=pltpu.CompilerParams(
            dimension_semantics=("arbitrary", "arbitrary"), vmem_limit_bytes=VMEM_LIMIT),
        name="mla_prompt",
    )(q, kcat, ckv_t, w_uv)


SAMPLE_Q_ROWS = 16
PAGES_PER_CHUNK = 32
SAMPLE_SLOTS = 4
SAMPLE_PREFETCH = SAMPLE_SLOTS - 1


def _mla_sample_kernel(pt_ref, q_ref, knew_ref, wuv_ref, ckv_hbm, krt_hbm, o_ref,
                       cbuf, kbuf, cbf, sem_c, sem_k, *, n_batch, n_chunks):
    b = pl.program_id(0)
    g_pages = PAGES_PER_CHUNK

    def page_copies(bb, c, slot, g):
        page = pt_ref[bb, c * g_pages + g]
        tok = pl.ds(g * PAGE_SIZE, PAGE_SIZE)
        return (pltpu.make_async_copy(ckv_hbm.at[page], cbuf.at[slot, tok, :], sem_c.at[slot]),
                pltpu.make_async_copy(krt_hbm.at[page], kbuf.at[slot, :, tok], sem_k.at[slot]))

    def start_chunk(bb, c, slot):
        for g in range(g_pages):
            cc, ck = page_copies(bb, c, slot, g)
            cc.start()
            ck.start()

    def wait_chunk(bb, c, slot):
        for g in range(g_pages):
            cc, ck = page_copies(bb, c, slot, g)
            cc.wait()
            ck.wait()

    def start_ahead(c):
        cn = c + SAMPLE_PREFETCH
        if cn < n_chunks:
            start_chunk(b, cn, cn % SAMPLE_SLOTS)
        else:
            @pl.when(b + 1 < n_batch)
            def _():
                start_chunk(b + 1, cn - n_chunks, (cn - n_chunks) % SAMPLE_SLOTS)

    @pl.when(b == 0)
    def _():
        for c0 in range(SAMPLE_PREFETCH):
            start_chunk(0, c0, c0)

    q = q_ref[0]
    q_lat = q[:, :KV_RANK]
    q_rope = q[:, KV_RANK:KV_RANK + MLA_ROPE]
    m = jnp.full((SAMPLE_Q_ROWS, 1), -jnp.inf, F32)
    l = jnp.zeros((SAMPLE_Q_ROWS, 1), F32)
    acc = jnp.zeros((SAMPLE_Q_ROWS, KV_RANK), F32)

    n_tiles = g_pages * PAGE_SIZE // KEY_TILE

    def tiled_pv(p_b, cbf_slot):
        out = _dot(p_b[:, 0:KEY_TILE], cbf[cbf_slot, 0:KEY_TILE, :])
        for t in range(1, n_tiles):
            tok = slice(t * KEY_TILE, (t + 1) * KEY_TILE)
            out = out + _dot(p_b[:, tok], cbf[cbf_slot, tok, :])
        return out

    pending = None
    for c in range(n_chunks):
        slot = c % SAMPLE_SLOTS
        wait_chunk(b, c, slot)
        if pending is not None:
            alpha_p, p_p, slot_p = pending
            acc = alpha_p * acc + tiled_pv(p_p, slot_p)
        s_tiles = []
        for t in range(n_tiles):
            tok = slice(t * KEY_TILE, (t + 1) * KEY_TILE)
            cb = cbuf[slot, tok, :].astype(BF16)
            cbf[c % 2, tok, :] = cb
            kb = kbuf[slot, :, tok].astype(BF16)
            s_tiles.append(_dot_nt(q_lat, cb) + _dot(q_rope, kb))
        s = jnp.concatenate(s_tiles, axis=1)
        start_ahead(c)
        m_new = jnp.maximum(m, jnp.max(s, axis=-1, keepdims=True))
        p = jnp.exp(s - m_new)
        alpha = jnp.exp(m - m_new)
        l = alpha * l + jnp.sum(p, axis=-1, keepdims=True)
        m = m_new
        pending = (alpha, p.astype(BF16), c % 2)
    alpha_p, p_p, slot_p = pending
    acc = alpha_p * acc + tiled_pv(p_p, slot_p)


    knew = knew_ref[0].astype(F32)
    s_self = jnp.sum(q.astype(F32) * knew, axis=-1, keepdims=True)
    m_new = jnp.maximum(m, s_self)
    p_self = jnp.exp(s_self - m_new)
    alpha = jnp.exp(m - m_new)
    l = alpha * l + p_self
    acc = alpha * acc + p_self.astype(BF16).astype(F32) * knew[:, :KV_RANK]
    o_lat = (acc / l).astype(BF16)
    for h in range(MLA_HEADS):
        res = _dot(o_lat, wuv_ref[h])
        o_ref[0, :, h * MLA_V:(h + 1) * MLA_V] = res[h:h + 1, :]


def _mla_sample(page_table, q_s, knew, w_uv, cache_ckv, cache_krope_t):
    n_batch, n_pages = page_table.shape
    n_chunks = n_pages // PAGES_PER_CHUNK
    chunk_rows = PAGES_PER_CHUNK * PAGE_SIZE
    assert n_chunks * PAGES_PER_CHUNK == n_pages and n_chunks % SAMPLE_SLOTS == 0
    assert SAMPLE_PREFETCH <= n_chunks
    kern = functools.partial(_mla_sample_kernel, n_batch=n_batch, n_chunks=n_chunks)
    grid_spec = pltpu.PrefetchScalarGridSpec(
        num_scalar_prefetch=1,
        grid=(n_batch,),
        in_specs=[
            pl.BlockSpec((1, SAMPLE_Q_ROWS, QK_WIDTH), lambda b, pt: (b, 0, 0)),
            pl.BlockSpec((1, 1, QK_WIDTH), lambda b, pt: (b, 0, 0)),
            pl.BlockSpec((MLA_HEADS, KV_RANK, MLA_V), lambda b, pt: (0, 0, 0)),
            pl.BlockSpec(memory_space=pl.ANY),
            pl.BlockSpec(memory_space=pl.ANY),
        ],
        out_specs=pl.BlockSpec((1, 1, MLA_HEADS * MLA_V), lambda b, pt: (b, 0, 0)),
        scratch_shapes=[
            pltpu.VMEM((SAMPLE_SLOTS, chunk_rows, KV_RANK), F32),
            pltpu.VMEM((SAMPLE_SLOTS, MLA_ROPE, chunk_rows), F32),
            pltpu.VMEM((2, chunk_rows, KV_RANK), BF16),
            pltpu.SemaphoreType.DMA((SAMPLE_SLOTS,)),
            pltpu.SemaphoreType.DMA((SAMPLE_SLOTS,)),
        ],
    )
    return pl.pallas_call(
        kern, grid_spec=grid_spec,
        out_shape=jax.ShapeDtypeStruct((n_batch, 1, MLA_HEADS * MLA_V), F32),
        compiler_params=pltpu.CompilerParams(
            dimension_semantics=("arbitrary",), vmem_limit_bytes=VMEM_LIMIT),
        name="mla_sample",
    )(page_table, q_s, knew, w_uv, cache_ckv, cache_krope_t)


GLA_UNROLL = 8


def _split3(x):
    hi = x.astype(BF16)
    r1 = x - hi.astype(F32)
    mid = r1.astype(BF16)
    lo = (r1 - mid.astype(F32)).astype(BF16)
    return hi, mid, lo


def _gla_gate_out(o, gg, gnorm):
    return _rms(o, gnorm) * (gg * jax.nn.sigmoid(gg))


def _gla_prompt_kernel(gq_ref, gk_ref, gv_ref, la_ref, gg_ref, gn_ref, o_ref, s_out_ref,
                       s_scr, *, n_sub):
    t = pl.program_id(1)
    cs = GLA_CHUNK

    @pl.when(t == 0)
    def _():
        s_scr[...] = jnp.zeros(s_scr.shape, F32)

    ri = lax.broadcasted_iota(jnp.int32, (cs, cs), 0)
    ci = lax.broadcasted_iota(jnp.int32, (cs, cs), 1)
    tri = ri >= ci
    tri_b = tri.astype(BF16)
    lane_head = lax.broadcasted_iota(jnp.int32, (cs, GLA_KW), 1) // GLA_DK
    gnorm = gn_ref[...]

    def chunk(c, s_prev):
        r0 = pl.multiple_of(c * cs, cs)
        g = la_ref[pl.ds(r0, cs), :]
        g_hi, g_mid, g_lo = _split3(g)
        bcum = _dot(tri_b, g_hi) + _dot(tri_b, g_mid) + _dot(tri_b, g_lo)
        b_last = bcum[cs - 1:cs, :]
        qf = gq_ref[pl.ds(r0, cs), :]
        kf = gk_ref[pl.ds(r0, cs), :]
        v = gv_ref[pl.ds(r0, cs), :]
        q_t = qf * jnp.exp(bcum)
        k_t = (kf * jnp.exp(-bcum)).astype(BF16)
        k_h = (kf * jnp.exp(b_last - bcum)).astype(BF16)
        decay = jnp.exp(b_last)
        q_stack = jnp.concatenate(
            [jnp.where(lane_head == h, q_t, 0.0) for h in range(GLA_HEADS)],
            axis=0).astype(BF16)
        a_stack = _dot_nt(q_stack, k_t)
        o_inter = _dot(q_stack, s_prev.astype(BF16))
        kv = _dot_tn(k_h, v)
        decay_col = jnp.transpose(jnp.broadcast_to(decay, (GLA_DV, GLA_KW)))
        kv_diag = jnp.concatenate(
            [kv[h * GLA_DK:(h + 1) * GLA_DK, h * GLA_DV:(h + 1) * GLA_DV]
             for h in range(GLA_HEADS)], axis=0)
        s_new = decay_col * s_prev + kv_diag
        for h in range(GLA_HEADS):
            a_h = jnp.where(tri, a_stack[h * cs:(h + 1) * cs], 0.0).astype(BF16)
            o_h = _dot(a_h, v[:, h * GLA_DV:(h + 1) * GLA_DV]) + o_inter[h * cs:(h + 1) * cs]
            gg_h = gg_ref[pl.ds(r0, cs), h * GLA_DV:(h + 1) * GLA_DV]
            o_ref[pl.ds(r0, cs), h * GLA_DV:(h + 1) * GLA_DV] = _gla_gate_out(
                o_h, gg_h, gnorm).astype(o_ref.dtype)
        return s_new

    def trip(j, carry):
        s = s_scr[...]
        for u in range(GLA_UNROLL):
            s = chunk(j * GLA_UNROLL + u, s)
        s_scr[...] = s
        return carry

    lax.fori_loop(0, n_sub // GLA_UNROLL, trip, 0)
    s_out_ref[0] = s_scr[...]


def _gla_prompt(gq, gk, gv, la, gg, gnorm, batch, seq, tc=1024):
    nt = seq // tc
    n_sub = tc // GLA_CHUNK
    row = lambda b, t: (b * nt + t, 0)
    kern = functools.partial(_gla_prompt_kernel, n_sub=n_sub)
    return pl.pallas_call(
        kern, grid=(batch, nt),
        in_specs=[
            pl.BlockSpec((tc, GLA_KW), row),
            pl.BlockSpec((tc, GLA_KW), row),
            pl.BlockSpec((tc, GLA_WIDTH), row),
            pl.BlockSpec((tc, GLA_KW), row),
            pl.BlockSpec((tc, GLA_WIDTH), row),
            pl.BlockSpec((1, GLA_DV), lambda b, t: (0, 0)),
        ],
        out_specs=[
            pl.BlockSpec((tc, GLA_WIDTH), row),
            pl.BlockSpec((1, GLA_KW, GLA_DV), lambda b, t: (b, 0, 0)),
        ],
        out_shape=[
            jax.ShapeDtypeStruct((batch * seq, GLA_WIDTH), BF16),
            jax.ShapeDtypeStruct((batch, GLA_KW, GLA_DV), F32),
        ],
        scratch_shapes=[pltpu.VMEM((GLA_KW, GLA_DV), F32)],
        compiler_params=pltpu.CompilerParams(
            dimension_semantics=("arbitrary", "arbitrary"), vmem_limit_bytes=VMEM_LIMIT),
        name="gla_prompt",
    )(gq, gk, gv, la, gg, gnorm)


def _gla_step_kernel(s_ref, gq_ref, gk_ref, gv_ref, la_ref, gg_ref, gn_ref, o_ref, s_out_ref,
                     *, bb):
    gnorm = gn_ref[...]

    def col(x_row):
        return jnp.transpose(jnp.broadcast_to(x_row, (GLA_DV, GLA_KW)))

    for i in range(bb):
        s = s_ref[i]
        e_col = col(jnp.exp(la_ref[i:i + 1, :]))
        k_col = col(gk_ref[i:i + 1, :])
        q_col = col(gq_ref[i:i + 1, :].astype(BF16).astype(F32))
        v_row = gv_ref[i:i + 1, :].astype(F32)
        v_rows = jnp.concatenate(
            [jnp.broadcast_to(v_row[:, h * GLA_DV:(h + 1) * GLA_DV], (GLA_DK, GLA_DV))
             for h in range(GLA_HEADS)], axis=0)
        s_new = e_col * s + k_col * v_rows
        s_out_ref[i] = s_new
        prod = q_col * s_new.astype(BF16).astype(F32)
        for h in range(GLA_HEADS):
            o_h = jnp.sum(prod[h * GLA_DK:(h + 1) * GLA_DK], axis=0, keepdims=True)
            gg_h = gg_ref[i:i + 1, h * GLA_DV:(h + 1) * GLA_DV]
            o_ref[i:i + 1, h * GLA_DV:(h + 1) * GLA_DV] = _gla_gate_out(o_h, gg_h, gnorm)


def _gla_step(state, gq, gk, gv, la, gg, gnorm, bb=8):
    n = state.shape[0]
    row = lambda i: (i, 0)
    kern = functools.partial(_gla_step_kernel, bb=bb)
    return pl.pallas_call(
        kern, grid=(n // bb,),
        in_specs=[
            pl.BlockSpec((bb, GLA_KW, GLA_DV), lambda i: (i, 0, 0)),
            pl.BlockSpec((bb, GLA_KW), row),
            pl.BlockSpec((bb, GLA_KW), row),
            pl.BlockSpec((bb, GLA_WIDTH), row),
            pl.BlockSpec((bb, GLA_KW), row),
            pl.BlockSpec((bb, GLA_WIDTH), row),
            pl.BlockSpec((1, GLA_DV), lambda i: (0, 0)),
        ],
        out_specs=[
            pl.BlockSpec((bb, GLA_WIDTH), row),
            pl.BlockSpec((bb, GLA_KW, GLA_DV), lambda i: (i, 0, 0)),
        ],
        out_shape=[
            jax.ShapeDtypeStruct((n, GLA_WIDTH), F32),
            jax.ShapeDtypeStruct((n, GLA_KW, GLA_DV), F32),
        ],
        compiler_params=pltpu.CompilerParams(
            dimension_semantics=("arbitrary",), vmem_limit_bytes=VMEM_LIMIT),
        name="gla_step",
    )(state, gq, gk, gv, la, gg, gnorm)


FF_CHUNK = 1024
POST_SUBBLOCKS = 2


def _post_kernel(x_ref, mla_ref, gla_ref, wout_ref, g1_ref, b1_ref, w1_ref, w2_ref,
                 g2_ref, b2_ref, y_ref):
    half = MLA_HEADS * MLA_V
    tm = x_ref.shape[0]
    n_sub = POST_SUBBLOCKS if tm >= POST_SUBBLOCKS * KEY_TILE else 1
    sub = tm // n_sub

    def front(r):
        mix = (_dot(mla_ref[r, :].astype(BF16), wout_ref[0:half, :])
               + _dot(gla_ref[r, :].astype(BF16), wout_ref[half:, :]))
        return _layer_norm(ALPHA * x_ref[r, :] + mix, g1_ref[...], b1_ref[...])

    rows = [pl.ds(k * sub, sub) for k in range(n_sub)]
    x1s = [front(r) for r in rows]
    for r, x1 in zip(rows, x1s):
        x1b = x1.astype(BF16)
        acc = jnp.zeros(x1.shape, F32)
        for c in range(D_FF // FF_CHUNK):
            hmid = _dot(x1b, w1_ref[:, c * FF_CHUNK:(c + 1) * FF_CHUNK])
            hmid = jnp.square(jnp.maximum(hmid, 0.0)).astype(BF16)
            acc = acc + _dot(hmid, w2_ref[c * FF_CHUNK:(c + 1) * FF_CHUNK, :])
        y_ref[r, :] = _layer_norm(ALPHA * x1 + acc, g2_ref[...], b2_ref[...])


def _post(x2d, mla, gla, w, tm):
    m = x2d.shape[0]
    row = lambda i: (i, 0)
    const = lambda i: (0, 0)
    resident = dict(pipeline_mode=pl.Buffered(1))
    return pl.pallas_call(
        _post_kernel, grid=(m // tm,),
        in_specs=[
            pl.BlockSpec((tm, D_MODEL), row),
            pl.BlockSpec((tm, MLA_HEADS * MLA_V), row),
            pl.BlockSpec((tm, GLA_WIDTH), row),
            pl.BlockSpec((D_MODEL, D_MODEL), const, **resident),
            pl.BlockSpec((1, D_MODEL), const),
            pl.BlockSpec((1, D_MODEL), const),
            pl.BlockSpec((D_MODEL, D_FF), const, **resident),
            pl.BlockSpec((D_FF, D_MODEL), const, **resident),
            pl.BlockSpec((1, D_MODEL), const),
            pl.BlockSpec((1, D_MODEL), const),
        ],
        out_specs=pl.BlockSpec((tm, D_MODEL), row),
        out_shape=jax.ShapeDtypeStruct((m, D_MODEL), F32),
        compiler_params=pltpu.CompilerParams(
            dimension_semantics=("arbitrary",), vmem_limit_bytes=VMEM_LIMIT),
        name="post",
    )(x2d, mla, gla, w['w_out'], w['ln1_g'], w['ln1_b'], w['w1'], w['w2'],
      w['ln2_g'], w['ln2_b'])


def _permute_w_in(w):
    sizes = (Q_RANK, KV_RANK, MLA_ROPE, GLA_KW, GLA_KW, GLA_WIDTH, GATE_RANK, GLA_WIDTH)
    off = np.concatenate([[0], np.cumsum(sizes)]).tolist()
    cq, ckv, kr, gq, gk, gv, gr, gg = [w[:, off[i]:off[i + 1]] for i in range(8)]
    half = MLA_ROPE // 2
    pad = jnp.zeros((w.shape[0], LANES - GATE_RANK), w.dtype)
    return jnp.concatenate(
        [cq, ckv, gq, gk, gv, gg, kr, kr[:, half:], kr[:, :half], gr, pad], axis=1).astype(BF16)


def _permute_w_uq(w):
    per_head = MLA_NOPE + MLA_ROPE
    half = MLA_ROPE // 2
    nope, rope, rope_sw = [], [], []
    for h in range(MLA_HEADS):
        base = h * per_head
        nope.append(w[:, base:base + MLA_NOPE])
        r = w[:, base + MLA_NOPE:base + per_head]
        rope.append(r)
        rope_sw += [r[:, half:], r[:, :half]]
    return jnp.concatenate(nope + rope + rope_sw, axis=1).astype(BF16)


def _prep_weights(w_in, mla_q_norm, mla_w_uq, mla_kv_norm, mla_w_uk, mla_w_uv,
                  gla_w_gate2, gla_b_gate, gla_norm, w_out, ln1_g, ln1_b,
                  mlp_w1, mlp_w2, ln2_g, ln2_b, l):
    w_in_p = _permute_w_in(w_in[l])
    w_uq_p = _permute_w_uq(mla_w_uq[l])
    return dict(
        w_in=w_in_p,
        q_norm=mla_q_norm[l][None, :],
        w_uq=w_uq_p,
        kv_norm=mla_kv_norm[l][None, :],
        w_uk=jnp.transpose(mla_w_uk[l], (1, 2, 0)).astype(BF16),
        w_uv=jnp.transpose(mla_w_uv[l], (1, 0, 2)).astype(BF16),
        w_gate2=jnp.pad(gla_w_gate2[l], ((0, LANES - GATE_RANK), (0, 0))).astype(BF16),
        b_gate=gla_b_gate[l][None, :],
        gla_norm=gla_norm[l][None, :],
        w_out=w_out[l].astype(BF16),
        ln1_g=ln1_g[l][None, :], ln1_b=ln1_b[l][None, :],
        w1=mlp_w1[l].astype(BF16), w2=mlp_w2[l].astype(BF16),
        ln2_g=ln2_g[l][None, :], ln2_b=ln2_b[l][None, :],
    )


def _rope_tables(pos):
    inv = ROPE_THETA ** (-jnp.arange(0, MLA_ROPE, 2, dtype=F32) / MLA_ROPE)
    ang = pos.astype(F32)[:, None] * inv[None, :]
    cos, sin = jnp.cos(ang), jnp.sin(ang)
    return (jnp.concatenate([cos, cos, cos, cos], axis=-1),
            jnp.concatenate([-sin, sin, -sin, sin], axis=-1))


def kernel(x_prompt, x_sample, cache_ckv, cache_krope, state_gla, page_table, w_in,
           mla_q_norm, mla_w_uq, mla_kv_norm, mla_w_uk, mla_w_uv, gla_w_gate2, gla_b_gate,
           gla_norm, w_out, ln1_g, ln1_b, mlp_w1, mlp_w2, ln2_g, ln2_b):
    assert w_in.shape[0] == DEPTH == 1
    batch, seq, _ = x_prompt.shape
    n_dec, t_new, _ = x_sample.shape
    assert t_new == 1
    l = 0
    w = _prep_weights(w_in, mla_q_norm, mla_w_uq, mla_kv_norm, mla_w_uk, mla_w_uv,
                      gla_w_gate2, gla_b_gate, gla_norm, w_out, ln1_g, ln1_b,
                      mlp_w1, mlp_w2, ln2_g, ln2_b, l)

    xp = x_prompt.reshape(batch * seq, D_MODEL)
    cos_p, sin_p = _rope_tables(jnp.arange(seq, dtype=jnp.int32))
    q, kcat, ckv_p, kr_p, gq, gk, gv, la, gg, ckv_t = _proj(
        xp, cos_p, sin_p, w, 512, BF16, True)
    mla_p = _mla_prompt(q, kcat, ckv_t, w['w_uv'], batch, seq)
    gla_p, s_p = _gla_prompt(gq, gk, gv, la, gg, w['gla_norm'], batch, seq)
    y_p = _post(xp, mla_p, gla_p, w, tm=512)

    xs = x_sample.reshape(n_dec, D_MODEL)
    cos_s, sin_s = _rope_tables(jnp.full((n_dec,), PAST_LEN, dtype=jnp.int32))
    q, kcat, ckv_s, kr_s, gq, gk, gv, la, gg = _proj(xs, cos_s, sin_s, w, n_dec, F32, False)
    q_s = jnp.pad(jnp.transpose(q, (1, 0, 2)),
                  ((0, 0), (0, SAMPLE_Q_ROWS - MLA_HEADS), (0, 0)))
    krope_t = jnp.swapaxes(cache_krope[l], 1, 2)
    mla_s = _mla_sample(page_table, q_s, kcat[:, None, :], w['w_uv'],
                        cache_ckv[l], krope_t)
    gla_s, s_s = _gla_step(state_gla[l].reshape(n_dec, GLA_KW, GLA_DV),
                           gq, gk, gv, la, gg, w['gla_norm'])
    y_s = _post(xs, mla_s.reshape(n_dec, MLA_HEADS * MLA_V), gla_s, w, tm=n_dec)

    return (y_p.reshape(batch, seq, D_MODEL),
            y_s.reshape(n_dec, 1, D_MODEL),
            ckv_p.reshape(1, batch, seq, KV_RANK),
            kr_p.reshape(1, batch, seq, MLA_ROPE),
            s_p.reshape(1, batch, GLA_HEADS, GLA_DK, GLA_DV),
            ckv_s.reshape(1, n_dec, 1, KV_RANK),
            kr_s.reshape(1, n_dec, 1, MLA_ROPE),
            s_s.reshape(1, n_dec, GLA_HEADS, GLA_DK, GLA_DV))
```

```python
import functools

import numpy as np
import jax
import jax.numpy as jnp
from jax import lax
from jax.experimental import pallas as pl
from jax.experimental.pallas import tpu as pltpu

D_MODEL = 1024
PAST_LEN = 16384
PAGE_SIZE = 128
MLA_HEADS = 4
MLA_V = 128
MLA_NOPE = 128
MLA_ROPE = 64
Q_RANK = 384
KV_RANK = 256
MLA_SCALE = (MLA_NOPE + MLA_ROPE) ** -0.5
ROPE_THETA = 10000.0
GLA_HEADS = 4
GLA_DV = 128
GLA_DK = 64
GLA_WIDTH = GLA_HEADS * GLA_DV
GLA_KW = GLA_HEADS * GLA_DK
GATE_RANK = 16
GATE_TAU = 16.0
GLA_CHUNK = 64
D_FF = 4 * D_MODEL
DEPTH = 1
ALPHA = (2.0 * DEPTH) ** 0.25
EPS = 1e-5

LANES = 128
QK_WIDTH = KV_RANK + LANES
VMEM_LIMIT = 56 * 1024 * 1024
KEY_TILE = 256
STREAM_HEADS = 2
STEP_TILES = 2

_C_CQ = 0
_C_CKV = _C_CQ + Q_RANK
_C_GQ = _C_CKV + KV_RANK
_C_GK = _C_GQ + GLA_KW
_C_GV = _C_GK + GLA_KW
_C_GG = _C_GV + GLA_WIDTH
_C_KR = _C_GG + GLA_WIDTH
_C_GR = _C_KR + 2 * MLA_ROPE
IN_COLS_P = _C_GR + LANES

BF16 = jnp.bfloat16
F32 = jnp.float32


def _dot(a, b):
    return jnp.dot(a, b, preferred_element_type=F32)


def _dot_nt(a, b):
    return lax.dot_general(a, b, (((1,), (1,)), ((), ())), preferred_element_type=F32)


def _dot_tn(a, b):
    return lax.dot_general(a, b, (((0,), (0,)), ((), ())), preferred_element_type=F32)


def _rms(x, g):
    return x * lax.rsqrt(jnp.mean(x * x, axis=-1, keepdims=True) + EPS) * g


def _layer_norm(x, g, b):
    mu = jnp.mean(x, axis=-1, keepdims=True)
    xc = x - mu
    var = jnp.mean(xc * xc, axis=-1, keepdims=True)
    return xc * lax.rsqrt(var + EPS) * g + b


def _proj_kernel(x_ref, cos_ref, sin_ref, win_ref, qn_ref, wuq_ref, kvn_ref, wuk_ref,
                 wg2_ref, bg_ref,
                 q_ref, kcat_ref, ckv_ref, kr_ref, gq_ref, gk_ref, gv_ref, la_ref, gg_ref,
                 ckvt_ref=None):
    tm = x_ref.shape[0]
    sub = KEY_TILE if tm % KEY_TILE == 0 else tm
    for k in range(tm // sub):
        _proj_rows(pl.ds(k * sub, sub), k, x_ref, cos_ref, sin_ref, win_ref, qn_ref, wuq_ref,
                   kvn_ref, wuk_ref, wg2_ref, bg_ref, q_ref, kcat_ref, ckv_ref, kr_ref,
                   gq_ref, gk_ref, gv_ref, la_ref, gg_ref, ckvt_ref)


def _proj_rows(r, k, x_ref, cos_ref, sin_ref, win_ref, qn_ref, wuq_ref, kvn_ref, wuk_ref,
               wg2_ref, bg_ref, q_ref, kcat_ref, ckv_ref, kr_ref, gq_ref, gk_ref, gv_ref,
               la_ref, gg_ref, ckvt_ref):
    xb = x_ref[r, :].astype(BF16)
    cos = cos_ref[r, :]
    sin = sin_ref[r, :]

    cq = _dot(xb, win_ref[:, _C_CQ:_C_CQ + Q_RANK])
    cqn = _rms(cq, qn_ref[...]).astype(BF16)
    q = _dot(cqn, wuq_ref[...])
    nh = MLA_HEADS * MLA_NOPE
    nr = MLA_HEADS * MLA_ROPE
    low_half = lax.broadcasted_iota(jnp.int32, (1, LANES), 1) < MLA_ROPE
    for h in range(MLA_HEADS):
        nope = q[:, h * MLA_NOPE:(h + 1) * MLA_NOPE].astype(BF16)
        q_lat = _dot(nope, wuk_ref[h])
        q_ref[h, r, 0:KV_RANK] = (q_lat * MLA_SCALE).astype(BF16)
    for pair in range(MLA_HEADS // 2):
        lanes = slice(nh + pair * LANES, nh + (pair + 1) * LANES)
        lanes_sw = slice(nh + nr + pair * LANES, nh + nr + (pair + 1) * LANES)
        rope2 = (q[:, lanes] * cos + q[:, lanes_sw] * sin) * MLA_SCALE
        for j, blk in enumerate((rope2, pltpu.roll(rope2, MLA_ROPE, 1))):
            q_ref[2 * pair + j, r, KV_RANK:QK_WIDTH] = jnp.where(low_half, blk, 0.0).astype(BF16)

    ckv = _rms(_dot(xb, win_ref[:, _C_CKV:_C_CKV + KV_RANK]), kvn_ref[...])
    ckv_ref[r, :] = ckv
    kr_gr = _dot(xb, win_ref[:, _C_KR:_C_GR + LANES])
    krr = kr_gr[:, :2 * MLA_ROPE]
    k_rope = (krr[:, :MLA_ROPE] * cos[:, :MLA_ROPE]
              + krr[:, MLA_ROPE:] * sin[:, :MLA_ROPE])
    kr_ref[r, :] = k_rope
    kcat_ref[r, 0:KV_RANK] = ckv.astype(BF16)
    kcat_ref[r, KV_RANK:QK_WIDTH] = jnp.concatenate(
        [k_rope, jnp.zeros_like(k_rope)], axis=-1).astype(BF16)
    if ckvt_ref is not None:
        ckvt_ref[k] = jnp.transpose(ckv).astype(BF16)

    gq_ref[r, :] = _dot(xb, win_ref[:, _C_GQ:_C_GQ + GLA_KW]) * (GLA_DK ** -0.5)
    gk_ref[r, :] = _dot(xb, win_ref[:, _C_GK:_C_GK + GLA_KW])
    gv_ref[r, :] = _dot(xb, win_ref[:, _C_GV:_C_GV + GLA_WIDTH]).astype(gv_ref.dtype)
    gg_ref[r, :] = _dot(xb, win_ref[:, _C_GG:_C_GG + GLA_WIDTH])
    gr = kr_gr[:, 2 * MLA_ROPE:].astype(BF16)
    z = _dot(gr, wg2_ref[...]) + bg_ref[...]
    la_ref[r, :] = (jnp.minimum(z, 0.0) - jnp.log1p(jnp.exp(-jnp.abs(z)))) / GATE_TAU


def _proj(x2d, cos_t, sin_t, w, tm, gv_dtype, with_ckv_t):
    m = x2d.shape[0]
    grid = (m // tm,)
    row = lambda i: (i, 0)
    const2 = lambda i: (0, 0)
    const3 = lambda i: (0, 0, 0)
    const3_row = lambda i: (i, 0, 0)
    pos_blocks = cos_t.shape[0] // tm
    pos_row = lambda i: (i % pos_blocks, 0)
    in_specs = [
        pl.BlockSpec((tm, D_MODEL), row),
        pl.BlockSpec((tm, LANES), pos_row),
        pl.BlockSpec((tm, LANES), pos_row),
        pl.BlockSpec((D_MODEL, IN_COLS_P), const2),
        pl.BlockSpec((1, Q_RANK), const2),
        pl.BlockSpec(w['w_uq'].shape, const2),
        pl.BlockSpec((1, KV_RANK), const2),
        pl.BlockSpec((MLA_HEADS, MLA_NOPE, KV_RANK), const3),
        pl.BlockSpec((LANES, GLA_KW), const2),
        pl.BlockSpec((1, GLA_KW), const2),
    ]
    out_shape = [
        jax.ShapeDtypeStruct((MLA_HEADS, m, QK_WIDTH), BF16),
        jax.ShapeDtypeStruct((m, QK_WIDTH), BF16),
        jax.ShapeDtypeStruct((m, KV_RANK), F32),
        jax.ShapeDtypeStruct((m, MLA_ROPE), F32),
        jax.ShapeDtypeStruct((m, GLA_KW), F32),
        jax.ShapeDtypeStruct((m, GLA_KW), F32),
        jax.ShapeDtypeStruct((m, GLA_WIDTH), gv_dtype),
        jax.ShapeDtypeStruct((m, GLA_KW), F32),
        jax.ShapeDtypeStruct((m, GLA_WIDTH), F32),
    ]
    out_specs = [
        pl.BlockSpec((MLA_HEADS, tm, QK_WIDTH), lambda i: (0, i, 0)),
        pl.BlockSpec((tm, QK_WIDTH), row),
        pl.BlockSpec((tm, KV_RANK), row),
        pl.BlockSpec((tm, MLA_ROPE), row),
        pl.BlockSpec((tm, GLA_KW), row),
        pl.BlockSpec((tm, GLA_KW), row),
        pl.BlockSpec((tm, GLA_WIDTH), row),
        pl.BlockSpec((tm, GLA_KW), row),
        pl.BlockSpec((tm, GLA_WIDTH), row),
    ]
    if with_ckv_t:
        out_shape.append(jax.ShapeDtypeStruct((m // KEY_TILE, KV_RANK, KEY_TILE), BF16))
        out_specs.append(pl.BlockSpec((tm // KEY_TILE, KV_RANK, KEY_TILE), const3_row))
    return pl.pallas_call(
        _proj_kernel, grid=grid, in_specs=in_specs, out_specs=out_specs, out_shape=out_shape,
        compiler_params=pltpu.CompilerParams(
            dimension_semantics=("arbitrary",), vmem_limit_bytes=VMEM_LIMIT),
        name="proj",
    )(x2d, cos_t, sin_t, w['w_in'], w['q_norm'], w['w_uq'], w['kv_norm'], w['w_uk'],
      w['w_gate2'], w['b_gate'])


def _mla_prompt_kernel(q_ref, k_ref, vt_ref, wuv_ref, o_ref, m_scr, l_scr, acc_scr,
                       s_a, s_b, *, tq):
    i = pl.program_id(1)
    rows = MLA_HEADS * tq
    size = STEP_TILES * KEY_TILE
    hw = STREAM_HEADS * tq
    n_streams = MLA_HEADS // STREAM_HEADS
    m_scr[...] = jnp.full(m_scr.shape, -jnp.inf, F32)
    l_scr[...] = jnp.zeros(l_scr.shape, F32)
    acc_scr[...] = jnp.zeros(acc_scr.shape, F32)

    def scores_to(s_ref, j):
        start = pl.multiple_of(j * size, size)
        kblk = k_ref[pl.ds(start, size), :]
        for g in range(n_streams):
            qg = q_ref[g * STREAM_HEADS:(g + 1) * STREAM_HEADS].reshape(hw, QK_WIDTH)
            s_ref[g] = _dot_nt(kblk, qg)

    def fold(s_ref, j, masked):
        tile0 = j * STEP_TILES
        if masked:
            key = j * size + lax.broadcasted_iota(jnp.int32, (size, hw), 0)
            tok = i * tq + (lax.broadcasted_iota(jnp.int32, (size, hw), 1) & (tq - 1))
            visible = key <= tok
        m_all = m_scr[...]
        l_all = l_scr[...]
        m_out, l_out, updates = [], [], []
        for g in range(n_streams):
            lanes = slice(g * hw, (g + 1) * hw)
            st = s_ref[g]
            if masked:
                st = jnp.where(visible, st, -jnp.inf)
            m_old = m_all[:, lanes]
            m_new = jnp.maximum(m_old, jnp.max(st, axis=0, keepdims=True))
            p = jnp.exp(st - m_new)
            alpha = jnp.exp(m_old - m_new)
            l_out.append(alpha * l_all[:, lanes] + jnp.sum(p, axis=0, keepdims=True))
            m_out.append(m_new)
            pb = p.astype(BF16)
            pv = _dot(vt_ref[tile0], pb[0:KEY_TILE])
            for t in range(1, STEP_TILES):
                pv = pv + _dot(vt_ref[tile0 + t], pb[t * KEY_TILE:(t + 1) * KEY_TILE])
            updates.append((lanes, alpha, pv))
        for lanes, alpha, pv in updates:
            acc_scr[:, lanes] = alpha * acc_scr[:, lanes] + pv
        m_scr[...] = jnp.concatenate(m_out, axis=1)
        l_scr[...] = jnp.concatenate(l_out, axis=1)

    last = (i * tq) // size
    scores_to(s_a, 0)

    def two_steps(k, carry):
        scores_to(s_b, 2 * k + 1)
        fold(s_a, 2 * k, masked=False)
        scores_to(s_a, 2 * k + 2)
        fold(s_b, 2 * k + 1, masked=False)
        return carry

    lax.fori_loop(0, last // 2, two_steps, 0)

    @pl.when(last % 2 == 1)
    def _():
        scores_to(s_b, last)
        fold(s_a, last - 1, masked=False)
        fold(s_b, last, masked=True)

    @pl.when(last % 2 == 0)
    def _():
        fold(s_a, last, masked=True)

    o_lat_t = (acc_scr[...] / l_scr[...]).astype(BF16)
    for h in range(MLA_HEADS):
        o_ref[:, h * MLA_V:(h + 1) * MLA_V] = _dot_tn(
            o_lat_t[:, h * tq:(h + 1) * tq], wuv_ref[h]).astype(o_ref.dtype)


def _mla_prompt(q, kcat, ckv_t, w_uv, batch, seq):
    tq = KEY_TILE
    nq = seq // tq
    kern = functools.partial(_mla_prompt_kernel, tq=tq)
    rows = MLA_HEADS * tq
    score_tile = (MLA_HEADS // STREAM_HEADS, STEP_TILES * KEY_TILE, STREAM_HEADS * tq)
    return pl.pallas_call(
        kern, grid=(batch, nq),
        in_specs=[
            pl.BlockSpec((MLA_HEADS, tq, QK_WIDTH), lambda b, i: (0, b * nq + i, 0)),
            pl.BlockSpec((seq, QK_WIDTH), lambda b, i: (b, 0)),
            pl.BlockSpec((seq // KEY_TILE, KV_RANK, KEY_TILE), lambda b, i: (b, 0, 0)),
            pl.BlockSpec((MLA_HEADS, KV_RANK, MLA_V), lambda b, i: (0, 0, 0)),
        ],
        out_specs=pl.BlockSpec((tq, MLA_HEADS * MLA_V), lambda b, i: (b * nq + i, 0)),
        out_shape=jax.ShapeDtypeStruct((batch * seq, MLA_HEADS * MLA_V), BF16),
        scratch_shapes=[pltpu.VMEM((1, rows), F32), pltpu.VMEM((1, rows), F32),
                        pltpu.VMEM((KV_RANK, rows), F32),
                        pltpu.VMEM(score_tile, F32), pltpu.VMEM(score_tile, F32)],
        compiler_params=pltpu.CompilerParams(
            dimension_semantics=("arbitrary", "arbitrary"), vmem_limit_bytes=VMEM_LIMIT),
        name="mla_prompt",
    )(q, kcat, ckv_t, w_uv)


SAMPLE_Q_ROWS = 16
PAGES_PER_CHUNK = 32
SAMPLE_SLOTS = 4
SAMPLE_PREFETCH = SAMPLE_SLOTS - 1


def _mla_sample_kernel(pt_ref, q_ref, knew_ref, wuv_ref, ckv_hbm, krt_hbm, o_ref,
                       cbuf, kbuf, cbf, sem_c, sem_k, *, n_batch, n_chunks):
    b = pl.program_id(0)
    g_pages = PAGES_PER_CHUNK

    def page_copies(bb, c, slot, g):
        page = pt_ref[bb, c * g_pages + g]
        tok = pl.ds(g * PAGE_SIZE, PAGE_SIZE)
        return (pltpu.make_async_copy(ckv_hbm.at[page], cbuf.at[slot, tok, :], sem_c.at[slot]),
                pltpu.make_async_copy(krt_hbm.at[page], kbuf.at[slot, :, tok], sem_k.at[slot]))

    def start_chunk(bb, c, slot):
        for g in range(g_pages):
            cc, ck = page_copies(bb, c, slot, g)
            cc.start()
            ck.start()

    def wait_chunk(bb, c, slot):
        for g in range(g_pages):
            cc, ck = page_copies(bb, c, slot, g)
            cc.wait()
            ck.wait()

    def start_ahead(c):
        cn = c + SAMPLE_PREFETCH
        if cn < n_chunks:
            start_chunk(b, cn, cn % SAMPLE_SLOTS)
        else:
            @pl.when(b + 1 < n_batch)
            def _():
                start_chunk(b + 1, cn - n_chunks, (cn - n_chunks) % SAMPLE_SLOTS)

    @pl.when(b == 0)
    def _():
        for c0 in range(SAMPLE_PREFETCH):
            start_chunk(0, c0, c0)

    q = q_ref[0]
    q_lat = q[:, :KV_RANK]
    q_rope = q[:, KV_RANK:KV_RANK + MLA_ROPE]
    m = jnp.full((SAMPLE_Q_ROWS, 1), -jnp.inf, F32)
    l = jnp.zeros((SAMPLE_Q_ROWS, 1), F32)
    acc = jnp.zeros((SAMPLE_Q_ROWS, KV_RANK), F32)

    n_tiles = g_pages * PAGE_SIZE // KEY_TILE

    def tiled_pv(p_b, cbf_slot):
        out = _dot(p_b[:, 0:KEY_TILE], cbf[cbf_slot, 0:KEY_TILE, :])
        for t in range(1, n_tiles):
            tok = slice(t * KEY_TILE, (t + 1) * KEY_TILE)
            out = out + _dot(p_b[:, tok], cbf[cbf_slot, tok, :])
        return out

    pending = None
    for c in range(n_chunks):
        slot = c % SAMPLE_SLOTS
        wait_chunk(b, c, slot)
        if pending is not None:
            alpha_p, p_p, slot_p = pending
            acc = alpha_p * acc + tiled_pv(p_p, slot_p)
        s_tiles = []
        for t in range(n_tiles):
            tok = slice(t * KEY_TILE, (t + 1) * KEY_TILE)
            cb = cbuf[slot, tok, :].astype(BF16)
            cbf[c % 2, tok, :] = cb
            kb = kbuf[slot, :, tok].astype(BF16)
            s_tiles.append(_dot_nt(q_lat, cb) + _dot(q_rope, kb))
        s = jnp.concatenate(s_tiles, axis=1)
        start_ahead(c)
        m_new = jnp.maximum(m, jnp.max(s, axis=-1, keepdims=True))
        p = jnp.exp(s - m_new)
        alpha = jnp.exp(m - m_new)
        l = alpha * l + jnp.sum(p, axis=-1, keepdims=True)
        m = m_new
        pending = (alpha, p.astype(BF16), c % 2)
    alpha_p, p_p, slot_p = pending
    acc = alpha_p * acc + tiled_pv(p_p, slot_p)


    knew = knew_ref[0].astype(F32)
    s_self = jnp.sum(q.astype(F32) * knew, axis=-1, keepdims=True)
    m_new = jnp.maximum(m, s_self)
    p_self = jnp.exp(s_self - m_new)
    alpha = jnp.exp(m - m_new)
    l = alpha * l + p_self
    acc = alpha * acc + p_self.astype(BF16).astype(F32) * knew[:, :KV_RANK]
    o_lat = (acc / l).astype(BF16)
    for h in range(MLA_HEADS):
        res = _dot(o_lat, wuv_ref[h])
        o_ref[0, :, h * MLA_V:(h + 1) * MLA_V] = res[h:h + 1, :]


def _mla_sample(page_table, q_s, knew, w_uv, cache_ckv, cache_krope_t):
    n_batch, n_pages = page_table.shape
    n_chunks = n_pages // PAGES_PER_CHUNK
    chunk_rows = PAGES_PER_CHUNK * PAGE_SIZE
    assert n_chunks * PAGES_PER_CHUNK == n_pages and n_chunks % SAMPLE_SLOTS == 0
    assert SAMPLE_PREFETCH <= n_chunks
    kern = functools.partial(_mla_sample_kernel, n_batch=n_batch, n_chunks=n_chunks)
    grid_spec = pltpu.PrefetchScalarGridSpec(
        num_scalar_prefetch=1,
        grid=(n_batch,),
        in_specs=[
            pl.BlockSpec((1, SAMPLE_Q_ROWS, QK_WIDTH), lambda b, pt: (b, 0, 0)),
            pl.BlockSpec((1, 1, QK_WIDTH), lambda b, pt: (b, 0, 0)),
            pl.BlockSpec((MLA_HEADS, KV_RANK, MLA_V), lambda b, pt: (0, 0, 0)),
            pl.BlockSpec(memory_space=pl.ANY),
            pl.BlockSpec(memory_space=pl.ANY),
        ],
        out_specs=pl.BlockSpec((1, 1, MLA_HEADS * MLA_V), lambda b, pt: (b, 0, 0)),
        scratch_shapes=[
            pltpu.VMEM((SAMPLE_SLOTS, chunk_rows, KV_RANK), F32),
            pltpu.VMEM((SAMPLE_SLOTS, MLA_ROPE, chunk_rows), F32),
            pltpu.VMEM((2, chunk_rows, KV_RANK), BF16),
            pltpu.SemaphoreType.DMA((SAMPLE_SLOTS,)),
            pltpu.SemaphoreType.DMA((SAMPLE_SLOTS,)),
        ],
    )
    return pl.pallas_call(
        kern, grid_spec=grid_spec,
        out_shape=jax.ShapeDtypeStruct((n_batch, 1, MLA_HEADS * MLA_V), F32),
        compiler_params=pltpu.CompilerParams(
            dimension_semantics=("arbitrary",), vmem_limit_bytes=VMEM_LIMIT),
        name="mla_sample",
    )(page_table, q_s, knew, w_uv, cache_ckv, cache_krope_t)


GLA_UNROLL = 8


def _split3(x):
    hi = x.astype(BF16)
    r1 = x - hi.astype(F32)
    mid = r1.astype(BF16)
    lo = (r1 - mid.astype(F32)).astype(BF16)
    return hi, mid, lo


def _gla_gate_out(o, gg, gnorm):
    return _rms(o, gnorm) * (gg * jax.nn.sigmoid(gg))


def _gla_prompt_kernel(gq_ref, gk_ref, gv_ref, la_ref, gg_ref, gn_ref, o_ref, s_out_ref,
                       s_scr, *, n_sub):
    t = pl.program_id(1)
    cs = GLA_CHUNK

    @pl.when(t == 0)
    def _():
        s_scr[...] = jnp.zeros(s_scr.shape, F32)

    ri = lax.broadcasted_iota(jnp.int32, (cs, cs), 0)
    ci = lax.broadcasted_iota(jnp.int32, (cs, cs), 1)
    tri = ri >= ci
    tri_b = tri.astype(BF16)
    lane_head = lax.broadcasted_iota(jnp.int32, (cs, GLA_KW), 1) // GLA_DK
    gnorm = gn_ref[...]

    def chunk(c, s_prev):
        r0 = pl.multiple_of(c * cs, cs)
        g = la_ref[pl.ds(r0, cs), :]
        g_hi, g_mid, g_lo = _split3(g)
        bcum = _dot(tri_b, g_hi) + _dot(tri_b, g_mid) + _dot(tri_b, g_lo)
        b_last = bcum[cs - 1:cs, :]
        qf = gq_ref[pl.ds(r0, cs), :]
        kf = gk_ref[pl.ds(r0, cs), :]
        v = gv_ref[pl.ds(r0, cs), :]
        q_t = qf * jnp.exp(bcum)
        k_t = (kf * jnp.exp(-bcum)).astype(BF16)
        k_h = (kf * jnp.exp(b_last - bcum)).astype(BF16)
        decay = jnp.exp(b_last)
        q_stack = jnp.concatenate(
            [jnp.where(lane_head == h, q_t, 0.0) for h in range(GLA_HEADS)],
            axis=0).astype(BF16)
        a_stack = _dot_nt(q_stack, k_t)
        o_inter = _dot(q_stack, s_prev.astype(BF16))
        kv = _dot_tn(k_h, v)
        decay_col = jnp.transpose(jnp.broadcast_to(decay, (GLA_DV, GLA_KW)))
        kv_diag = jnp.concatenate(
            [kv[h * GLA_DK:(h + 1) * GLA_DK, h * GLA_DV:(h + 1) * GLA_DV]
             for h in range(GLA_HEADS)], axis=0)
        s_new = decay_col * s_prev + kv_diag
        for h in range(GLA_HEADS):
            a_h = jnp.where(tri, a_stack[h * cs:(h + 1) * cs], 0.0).astype(BF16)
            o_h = _dot(a_h, v[:, h * GLA_DV:(h + 1) * GLA_DV]) + o_inter[h * cs:(h + 1) * cs]
            gg_h = gg_ref[pl.ds(r0, cs), h * GLA_DV:(h + 1) * GLA_DV]
            o_ref[pl.ds(r0, cs), h * GLA_DV:(h + 1) * GLA_DV] = _gla_gate_out(
                o_h, gg_h, gnorm).astype(o_ref.dtype)
        return s_new

    def trip(j, carry):
        s = s_scr[...]
        for u in range(GLA_UNROLL):
            s = chunk(j * GLA_UNROLL + u, s)
        s_scr[...] = s
        return carry

    lax.fori_loop(0, n_sub // GLA_UNROLL, trip, 0)
    s_out_ref[0] = s_scr[...]


def _gla_prompt(gq, gk, gv, la, gg, gnorm, batch, seq, tc=1024):
    nt = seq // tc
    n_sub = tc // GLA_CHUNK
    row = lambda b, t: (b * nt + t, 0)
    kern = functools.partial(_gla_prompt_kernel, n_sub=n_sub)
    return pl.pallas_call(
        kern, grid=(batch, nt),
        in_specs=[
            pl.BlockSpec((tc, GLA_KW), row),
            pl.BlockSpec((tc, GLA_KW), row),
            pl.BlockSpec((tc, GLA_WIDTH), row),
            pl.BlockSpec((tc, GLA_KW), row),
            pl.BlockSpec((tc, GLA_WIDTH), row),
            pl.BlockSpec((1, GLA_DV), lambda b, t: (0, 0)),
        ],
        out_specs=[
            pl.BlockSpec((tc, GLA_WIDTH), row),
            pl.BlockSpec((1, GLA_KW, GLA_DV), lambda b, t: (b, 0, 0)),
        ],
        out_shape=[
            jax.ShapeDtypeStruct((batch * seq, GLA_WIDTH), BF16),
            jax.ShapeDtypeStruct((batch, GLA_KW, GLA_DV), F32),
        ],
        scratch_shapes=[pltpu.VMEM((GLA_KW, GLA_DV), F32)],
        compiler_params=pltpu.CompilerParams(
            dimension_semantics=("arbitrary", "arbitrary"), vmem_limit_bytes=VMEM_LIMIT),
        name="gla_prompt",
    )(gq, gk, gv, la, gg, gnorm)


def _gla_step_kernel(s_ref, gq_ref, gk_ref, gv_ref, la_ref, gg_ref, gn_ref, o_ref, s_out_ref,
                     *, bb):
    gnorm = gn_ref[...]

    def col(x_row):
        return jnp.transpose(jnp.broadcast_to(x_row, (GLA_DV, GLA_KW)))

    for i in range(bb):
        s = s_ref[i]
        e_col = col(jnp.exp(la_ref[i:i + 1, :]))
        k_col = col(gk_ref[i:i + 1, :])
        q_col = col(gq_ref[i:i + 1, :].astype(BF16).astype(F32))
        v_row = gv_ref[i:i + 1, :].astype(F32)
        v_rows = jnp.concatenate(
            [jnp.broadcast_to(v_row[:, h * GLA_DV:(h + 1) * GLA_DV], (GLA_DK, GLA_DV))
             for h in range(GLA_HEADS)], axis=0)
        s_new = e_col * s + k_col * v_rows
        s_out_ref[i] = s_new
        prod = q_col * s_new.astype(BF16).astype(F32)
        for h in range(GLA_HEADS):
            o_h = jnp.sum(prod[h * GLA_DK:(h + 1) * GLA_DK], axis=0, keepdims=True)
            gg_h = gg_ref[i:i + 1, h * GLA_DV:(h + 1) * GLA_DV]
            o_ref[i:i + 1, h * GLA_DV:(h + 1) * GLA_DV] = _gla_gate_out(o_h, gg_h, gnorm)


def _gla_step(state, gq, gk, gv, la, gg, gnorm, bb=8):
    n = state.shape[0]
    row = lambda i: (i, 0)
    kern = functools.partial(_gla_step_kernel, bb=bb)
    return pl.pallas_call(
        kern, grid=(n // bb,),
        in_specs=[
            pl.BlockSpec((bb, GLA_KW, GLA_DV), lambda i: (i, 0, 0)),
            pl.BlockSpec((bb, GLA_KW), row),
            pl.BlockSpec((bb, GLA_KW), row),
            pl.BlockSpec((bb, GLA_WIDTH), row),
            pl.BlockSpec((bb, GLA_KW), row),
            pl.BlockSpec((bb, GLA_WIDTH), row),
            pl.BlockSpec((1, GLA_DV), lambda i: (0, 0)),
        ],
        out_specs=[
            pl.BlockSpec((bb, GLA_WIDTH), row),
            pl.BlockSpec((bb, GLA_KW, GLA_DV), lambda i: (i, 0, 0)),
        ],
        out_shape=[
            jax.ShapeDtypeStruct((n, GLA_WIDTH), F32),
            jax.ShapeDtypeStruct((n, GLA_KW, GLA_DV), F32),
        ],
        compiler_params=pltpu.CompilerParams(
            dimension_semantics=("arbitrary",), vmem_limit_bytes=VMEM_LIMIT),
        name="gla_step",
    )(state, gq, gk, gv, la, gg, gnorm)


FF_CHUNK = 1024
POST_SUBBLOCKS = 2


def _post_kernel(x_ref, mla_ref, gla_ref, wout_ref, g1_ref, b1_ref, w1_ref, w2_ref,
                 g2_ref, b2_ref, y_ref):
    half = MLA_HEADS * MLA_V
    tm = x_ref.shape[0]
    n_sub = POST_SUBBLOCKS if tm >= POST_SUBBLOCKS * KEY_TILE else 1
    sub = tm // n_sub

    def front(r):
        mix = (_dot(mla_ref[r, :].astype(BF16), wout_ref[0:half, :])
               + _dot(gla_ref[r, :].astype(BF16), wout_ref[half:, :]))
        return _layer_norm(ALPHA * x_ref[r, :] + mix, g1_ref[...], b1_ref[...])

    rows = [pl.ds(k * sub, sub) for k in range(n_sub)]
    x1s = [front(r) for r in rows]
    for r, x1 in zip(rows, x1s):
        x1b = x1.astype(BF16)
        acc = jnp.zeros(x1.shape, F32)
        for c in range(D_FF // FF_CHUNK):
            hmid = _dot(x1b, w1_ref[:, c * FF_CHUNK:(c + 1) * FF_CHUNK])
            hmid = jnp.square(jnp.maximum(hmid, 0.0)).astype(BF16)
            acc = acc + _dot(hmid, w2_ref[c * FF_CHUNK:(c + 1) * FF_CHUNK, :])
        y_ref[r, :] = _layer_norm(ALPHA * x1 + acc, g2_ref[...], b2_ref[...])


def _post(x2d, mla, gla, w, tm):
    m = x2d.shape[0]
    row = lambda i: (i, 0)
    const = lambda i: (0, 0)
    resident = dict(pipeline_mode=pl.Buffered(1))
    return pl.pallas_call(
        _post_kernel, grid=(m // tm,),
        in_specs=[
            pl.BlockSpec((tm, D_MODEL), row),
            pl.BlockSpec((tm, MLA_HEADS * MLA_V), row),
            pl.BlockSpec((tm, GLA_WIDTH), row),
            pl.BlockSpec((D_MODEL, D_MODEL), const, **resident),
            pl.BlockSpec((1, D_MODEL), const),
            pl.BlockSpec((1, D_MODEL), const),
            pl.BlockSpec((D_MODEL, D_FF), const, **resident),
            pl.BlockSpec((D_FF, D_MODEL), const, **resident),
            pl.BlockSpec((1, D_MODEL), const),
            pl.BlockSpec((1, D_MODEL), const),
        ],
        out_specs=pl.BlockSpec((tm, D_MODEL), row),
        out_shape=jax.ShapeDtypeStruct((m, D_MODEL), F32),
        compiler_params=pltpu.CompilerParams(
            dimension_semantics=("arbitrary",), vmem_limit_bytes=VMEM_LIMIT),
        name="post",
    )(x2d, mla, gla, w['w_out'], w['ln1_g'], w['ln1_b'], w['w1'], w['w2'],
      w['ln2_g'], w['ln2_b'])


def _permute_w_in(w):
    sizes = (Q_RANK, KV_RANK, MLA_ROPE, GLA_KW, GLA_KW, GLA_WIDTH, GATE_RANK, GLA_WIDTH)
    off = np.concatenate([[0], np.cumsum(sizes)]).tolist()
    cq, ckv, kr, gq, gk, gv, gr, gg = [w[:, off[i]:off[i + 1]] for i in range(8)]
    half = MLA_ROPE // 2
    pad = jnp.zeros((w.shape[0], LANES - GATE_RANK), w.dtype)
    return jnp.concatenate(
        [cq, ckv, gq, gk, gv, gg, kr, kr[:, half:], kr[:, :half], gr, pad], axis=1).astype(BF16)


def _permute_w_uq(w):
    per_head = MLA_NOPE + MLA_ROPE
    half = MLA_ROPE // 2
    nope, rope, rope_sw = [], [], []
    for h in range(MLA_HEADS):
        base = h * per_head
        nope.append(w[:, base:base + MLA_NOPE])
        r = w[:, base + MLA_NOPE:base + per_head]
        rope.append(r)
        rope_sw += [r[:, half:], r[:, :half]]
    return jnp.concatenate(nope + rope + rope_sw, axis=1).astype(BF16)


def _prep_weights(w_in, mla_q_norm, mla_w_uq, mla_kv_norm, mla_w_uk, mla_w_uv,
                  gla_w_gate2, gla_b_gate, gla_norm, w_out, ln1_g, ln1_b,
                  mlp_w1, mlp_w2, ln2_g, ln2_b, l):
    w_in_p = _permute_w_in(w_in[l])
    w_uq_p = _permute_w_uq(mla_w_uq[l])
    return dict(
        w_in=w_in_p,
        q_norm=mla_q_norm[l][None, :],
        w_uq=w_uq_p,
        kv_norm=mla_kv_norm[l][None, :],
        w_uk=jnp.transpose(mla_w_uk[l], (1, 2, 0)).astype(BF16),
        w_uv=jnp.transpose(mla_w_uv[l], (1, 0, 2)).astype(BF16),
        w_gate2=jnp.pad(gla_w_gate2[l], ((0, LANES - GATE_RANK), (0, 0))).astype(BF16),
        b_gate=gla_b_gate[l][None, :],
        gla_norm=gla_norm[l][None, :],
        w_out=w_out[l].astype(BF16),
        ln1_g=ln1_g[l][None, :], ln1_b=ln1_b[l][None, :],
        w1=mlp_w1[l].astype(BF16), w2=mlp_w2[l].astype(BF16),
        ln2_g=ln2_g[l][None, :], ln2_b=ln2_b[l][None, :],
    )


def _rope_tables(pos):
    half = MLA_ROPE // 2
    inv = ROPE_THETA ** (-jnp.arange(0, MLA_ROPE, 2, dtype=F32) / MLA_ROPE)
    inv4 = jnp.tile(inv, LANES // half)
    sign = jnp.tile(jnp.concatenate([-jnp.ones((half,), F32), jnp.ones((half,), F32)]),
                    LANES // MLA_ROPE)
    ang = pos.astype(F32)[:, None] * inv4[None, :]
    return jnp.cos(ang), jnp.sin(ang) * sign[None, :]


def kernel(x_prompt, x_sample, cache_ckv, cache_krope, state_gla, page_table, w_in,
           mla_q_norm, mla_w_uq, mla_kv_norm, mla_w_uk, mla_w_uv, gla_w_gate2, gla_b_gate,
           gla_norm, w_out, ln1_g, ln1_b, mlp_w1, mlp_w2, ln2_g, ln2_b):
    assert w_in.shape[0] == DEPTH == 1
    batch, seq, _ = x_prompt.shape
    n_dec, t_new, _ = x_sample.shape
    assert t_new == 1
    l = 0
    w = _prep_weights(w_in, mla_q_norm, mla_w_uq, mla_kv_norm, mla_w_uk, mla_w_uv,
                      gla_w_gate2, gla_b_gate, gla_norm, w_out, ln1_g, ln1_b,
                      mlp_w1, mlp_w2, ln2_g, ln2_b, l)

    xp = x_prompt.reshape(batch * seq, D_MODEL)
    cos_p, sin_p = _rope_tables(jnp.arange(seq, dtype=jnp.int32))
    q, kcat, ckv_p, kr_p, gq, gk, gv, la, gg, ckv_t = _proj(
        xp, cos_p, sin_p, w, 512, BF16, True)
    mla_p = _mla_prompt(q, kcat, ckv_t, w['w_uv'], batch, seq)
    gla_p, s_p = _gla_prompt(gq, gk, gv, la, gg, w['gla_norm'], batch, seq)
    y_p = _post(xp, mla_p, gla_p, w, tm=512)

    xs = x_sample.reshape(n_dec, D_MODEL)
    cos_s, sin_s = _rope_tables(jnp.full((n_dec,), PAST_LEN, dtype=jnp.int32))
    q, kcat, ckv_s, kr_s, gq, gk, gv, la, gg = _proj(xs, cos_s, sin_s, w, n_dec, F32, False)
    q_s = jnp.pad(jnp.transpose(q, (1, 0, 2)),
                  ((0, 0), (0, SAMPLE_Q_ROWS - MLA_HEADS), (0, 0)))
    krope_t = jnp.swapaxes(cache_krope[l], 1, 2)
    mla_s = _mla_sample(page_table, q_s, kcat[:, None, :], w['w_uv'],
                        cache_ckv[l], krope_t)
    gla_s, s_s = _gla_step(state_gla[l].reshape(n_dec, GLA_KW, GLA_DV),
                           gq, gk, gv, la, gg, w['gla_norm'])
    y_s = _post(xs, mla_s.reshape(n_dec, MLA_HEADS * MLA_V), gla_s, w, tm=n_dec)

    return (y_p.reshape(batch, seq, D_MODEL),
            y_s.reshape(n_dec, 1, D_MODEL),
            ckv_p.reshape(1, batch, seq, KV_RANK),
            kr_p.reshape(1, batch, seq, MLA_ROPE),
            s_p.reshape(1, batch, GLA_HEADS, GLA_DK, GLA_DV),
            ckv_s.reshape(1, n_dec, 1, KV_RANK),
            kr_s.reshape(1, n_dec, 1, MLA_ROPE),
            s_s.reshape(1, n_dec, GLA_HEADS, GLA_DK, GLA_DV))
```

```python
import functools

import numpy as np
import jax
import jax.numpy as jnp
from jax import lax
from jax.experimental import pallas as pl
from jax.experimental.pallas import tpu as pltpu

D_MODEL = 1024
PAST_LEN = 16384
PAGE_SIZE = 128
MLA_HEADS = 4
MLA_V = 128
MLA_NOPE = 128
MLA_ROPE = 64
Q_RANK = 384
KV_RANK = 256
MLA_SCALE = (MLA_NOPE + MLA_ROPE) ** -0.5
ROPE_THETA = 10000.0
GLA_HEADS = 4
GLA_DV = 128
GLA_DK = 64
GLA_WIDTH = GLA_HEADS * GLA_DV
GLA_KW = GLA_HEADS * GLA_DK
GATE_RANK = 16
GATE_TAU = 16.0
GLA_CHUNK = 64
D_FF = 4 * D_MODEL
DEPTH = 1
ALPHA = (2.0 * DEPTH) ** 0.25
EPS = 1e-5

LANES = 128
QK_WIDTH = KV_RANK + LANES
HEAD_QK = MLA_NOPE + LANES
VMEM_LIMIT = 56 * 1024 * 1024
KEY_TILE = 256
STREAM_HEADS = 2
STEP_TILES = 2

_C_CQ = 0
_C_CKV = _C_CQ + Q_RANK
_C_GQ = _C_CKV + KV_RANK
_C_GK = _C_GQ + GLA_KW
_C_GV = _C_GK + GLA_KW
_C_GG = _C_GV + GLA_WIDTH
_C_KR = _C_GG + GLA_WIDTH
_C_GR = _C_KR + 2 * MLA_ROPE
IN_COLS_P = _C_GR + LANES

BF16 = jnp.bfloat16
F32 = jnp.float32


def _dot(a, b):
    return jnp.dot(a, b, preferred_element_type=F32)


def _dot_nt(a, b):
    return lax.dot_general(a, b, (((1,), (1,)), ((), ())), preferred_element_type=F32)


def _dot_tn(a, b):
    return lax.dot_general(a, b, (((0,), (0,)), ((), ())), preferred_element_type=F32)


def _rms(x, g):
    return x * lax.rsqrt(jnp.mean(x * x, axis=-1, keepdims=True) + EPS) * g


def _layer_norm(x, g, b):
    mu = jnp.mean(x, axis=-1, keepdims=True)
    xc = x - mu
    var = jnp.mean(xc * xc, axis=-1, keepdims=True)
    return xc * lax.rsqrt(var + EPS) * g + b


def _proj_kernel(x_ref, cos_ref, sin_ref, win_ref, qn_ref, wuq_ref, kvn_ref, wuk_ref,
                 wukv_ref, wg2_ref, bg_ref,
                 q_ref, key_ref, ckv_ref, kr_ref, gq_ref, gk_ref, gv_ref, la_ref, gg_ref,
                 vt_ref=None):
    tm = x_ref.shape[0]
    sub = KEY_TILE if tm % KEY_TILE == 0 else tm
    for k in range(tm // sub):
        _proj_rows(pl.ds(k * sub, sub), k, x_ref, cos_ref, sin_ref, win_ref, qn_ref, wuq_ref,
                   kvn_ref, wuk_ref, wukv_ref, wg2_ref, bg_ref, q_ref, key_ref, ckv_ref,
                   kr_ref, gq_ref, gk_ref, gv_ref, la_ref, gg_ref, vt_ref)


def _proj_rows(r, k, x_ref, cos_ref, sin_ref, win_ref, qn_ref, wuq_ref, kvn_ref, wuk_ref,
               wukv_ref, wg2_ref, bg_ref, q_ref, key_ref, ckv_ref, kr_ref, gq_ref, gk_ref,
               gv_ref, la_ref, gg_ref, vt_ref):
    xb = x_ref[r, :].astype(BF16)
    cos = cos_ref[r, :]
    sin = sin_ref[r, :]

    cq = _dot(xb, win_ref[:, _C_CQ:_C_CQ + Q_RANK])
    cqn = _rms(cq, qn_ref[...]).astype(BF16)
    q = _dot(cqn, wuq_ref[...])
    nh = MLA_HEADS * MLA_NOPE
    nr = MLA_HEADS * MLA_ROPE
    low_half = lax.broadcasted_iota(jnp.int32, (1, LANES), 1) < MLA_ROPE
    per_head = vt_ref is not None
    q_main = MLA_NOPE if per_head else KV_RANK
    for h in range(MLA_HEADS):
        nope = q[:, h * MLA_NOPE:(h + 1) * MLA_NOPE]
        if not per_head:
            nope = _dot(nope.astype(BF16), wuk_ref[h])
        q_ref[h, r, 0:q_main] = (nope * MLA_SCALE).astype(BF16)
    for pair in range(MLA_HEADS // 2):
        lanes = slice(nh + pair * LANES, nh + (pair + 1) * LANES)
        lanes_sw = slice(nh + nr + pair * LANES, nh + nr + (pair + 1) * LANES)
        rope2 = (q[:, lanes] * cos + q[:, lanes_sw] * sin) * MLA_SCALE
        for j, blk in enumerate((rope2, pltpu.roll(rope2, MLA_ROPE, 1))):
            q_ref[2 * pair + j, r, q_main:q_main + LANES] = jnp.where(
                low_half, blk, 0.0).astype(BF16)

    ckv = _rms(_dot(xb, win_ref[:, _C_CKV:_C_CKV + KV_RANK]), kvn_ref[...])
    ckv_ref[r, :] = ckv
    kr_gr = _dot(xb, win_ref[:, _C_KR:_C_GR + LANES])
    krr = kr_gr[:, :2 * MLA_ROPE]
    k_rope = (krr[:, :MLA_ROPE] * cos[:, :MLA_ROPE]
              + krr[:, MLA_ROPE:] * sin[:, :MLA_ROPE])
    kr_ref[r, :] = k_rope
    ckv_b = ckv.astype(BF16)
    kr_pad = jnp.concatenate([k_rope, jnp.zeros_like(k_rope)], axis=-1).astype(BF16)
    if per_head:
        kv = _dot(ckv_b, wukv_ref[...])
        hk = MLA_HEADS * MLA_NOPE
        for h in range(MLA_HEADS):
            base = h * HEAD_QK
            key_ref[r, base:base + MLA_NOPE] = kv[:, h * MLA_NOPE:(h + 1) * MLA_NOPE].astype(BF16)
            key_ref[r, base + MLA_NOPE:base + HEAD_QK] = kr_pad
        vt_ref[k] = jnp.transpose(kv[:, hk:]).astype(BF16)
    else:
        key_ref[r, 0:KV_RANK] = ckv_b
        key_ref[r, KV_RANK:QK_WIDTH] = kr_pad

    gq_ref[r, :] = _dot(xb, win_ref[:, _C_GQ:_C_GQ + GLA_KW]) * (GLA_DK ** -0.5)
    gk_ref[r, :] = _dot(xb, win_ref[:, _C_GK:_C_GK + GLA_KW])
    gv_ref[r, :] = _dot(xb, win_ref[:, _C_GV:_C_GV + GLA_WIDTH]).astype(gv_ref.dtype)
    gg_ref[r, :] = _dot(xb, win_ref[:, _C_GG:_C_GG + GLA_WIDTH])
    gr = kr_gr[:, 2 * MLA_ROPE:].astype(BF16)
    z = _dot(gr, wg2_ref[...]) + bg_ref[...]
    la_ref[r, :] = (jnp.minimum(z, 0.0) - jnp.log1p(jnp.exp(-jnp.abs(z)))) / GATE_TAU


def _proj(x2d, cos_t, sin_t, w, tm, per_head):
    m = x2d.shape[0]
    q_width = HEAD_QK if per_head else QK_WIDTH
    key_width = MLA_HEADS * HEAD_QK if per_head else QK_WIDTH
    grid = (m // tm,)
    row = lambda i: (i, 0)
    const2 = lambda i: (0, 0)
    const3 = lambda i: (0, 0, 0)
    const3_row = lambda i: (i, 0, 0)
    pos_blocks = cos_t.shape[0] // tm
    pos_row = lambda i: (i % pos_blocks, 0)
    in_specs = [
        pl.BlockSpec((tm, D_MODEL), row),
        pl.BlockSpec((tm, LANES), pos_row),
        pl.BlockSpec((tm, LANES), pos_row),
        pl.BlockSpec((D_MODEL, IN_COLS_P), const2),
        pl.BlockSpec((1, Q_RANK), const2),
        pl.BlockSpec(w['w_uq'].shape, const2),
        pl.BlockSpec((1, KV_RANK), const2),
        pl.BlockSpec((MLA_HEADS, MLA_NOPE, KV_RANK), const3),
        pl.BlockSpec(w['w_ukv'].shape, const2),
        pl.BlockSpec((LANES, GLA_KW), const2),
        pl.BlockSpec((1, GLA_KW), const2),
    ]
    out_shape = [
        jax.ShapeDtypeStruct((MLA_HEADS, m, q_width), BF16),
        jax.ShapeDtypeStruct((m, key_width), BF16),
        jax.ShapeDtypeStruct((m, KV_RANK), F32),
        jax.ShapeDtypeStruct((m, MLA_ROPE), F32),
        jax.ShapeDtypeStruct((m, GLA_KW), F32),
        jax.ShapeDtypeStruct((m, GLA_KW), F32),
        jax.ShapeDtypeStruct((m, GLA_WIDTH), BF16 if per_head else F32),
        jax.ShapeDtypeStruct((m, GLA_KW), F32),
        jax.ShapeDtypeStruct((m, GLA_WIDTH), F32),
    ]
    out_specs = [
        pl.BlockSpec((MLA_HEADS, tm, q_width), lambda i: (0, i, 0)),
        pl.BlockSpec((tm, key_width), row),
        pl.BlockSpec((tm, KV_RANK), row),
        pl.BlockSpec((tm, MLA_ROPE), row),
        pl.BlockSpec((tm, GLA_KW), row),
        pl.BlockSpec((tm, GLA_KW), row),
        pl.BlockSpec((tm, GLA_WIDTH), row),
        pl.BlockSpec((tm, GLA_KW), row),
        pl.BlockSpec((tm, GLA_WIDTH), row),
    ]
    if per_head:
        vt_rows = MLA_HEADS * MLA_V
        out_shape.append(jax.ShapeDtypeStruct((m // KEY_TILE, vt_rows, KEY_TILE), BF16))
        out_specs.append(pl.BlockSpec((tm // KEY_TILE, vt_rows, KEY_TILE), const3_row))
    return pl.pallas_call(
        _proj_kernel, grid=grid, in_specs=in_specs, out_specs=out_specs, out_shape=out_shape,
        compiler_params=pltpu.CompilerParams(
            dimension_semantics=("arbitrary",), vmem_limit_bytes=VMEM_LIMIT),
        name="proj",
    )(x2d, cos_t, sin_t, w['w_in'], w['q_norm'], w['w_uq'], w['kv_norm'], w['w_uk'],
      w['w_ukv'], w['w_gate2'], w['b_gate'])


def _mla_prompt_kernel(q_ref, k_ref, vt_ref, o_ref, m_scr, l_scr, acc_scr, s_a, s_b, *, tq):
    i = pl.program_id(1)
    rows = MLA_HEADS * tq
    size = STEP_TILES * KEY_TILE
    hw = STREAM_HEADS * tq
    n_streams = MLA_HEADS // STREAM_HEADS
    m_scr[...] = jnp.full(m_scr.shape, -jnp.inf, F32)
    l_scr[...] = jnp.zeros(l_scr.shape, F32)
    acc_scr[...] = jnp.zeros(acc_scr.shape, F32)

    def scores_to(s_ref, j):
        start = pl.multiple_of(j * size, size)
        for h in range(MLA_HEADS):
            g, hh = divmod(h, STREAM_HEADS)
            kh = k_ref[pl.ds(start, size), h * HEAD_QK:(h + 1) * HEAD_QK]
            s_ref[g, :, hh * tq:(hh + 1) * tq] = _dot_nt(kh, q_ref[h])

    def fold(s_ref, j, masked):
        tile0 = j * STEP_TILES
        if masked:
            key = j * size + lax.broadcasted_iota(jnp.int32, (size, hw), 0)
            tok = i * tq + (lax.broadcasted_iota(jnp.int32, (size, hw), 1) & (tq - 1))
            visible = key <= tok
        m_all = m_scr[...]
        l_all = l_scr[...]
        m_out, l_out, updates = [], [], []
        for g in range(n_streams):
            lanes = slice(g * hw, (g + 1) * hw)
            st = s_ref[g]
            if masked:
                st = jnp.where(visible, st, -jnp.inf)
            m_old = m_all[:, lanes]
            m_new = jnp.maximum(m_old, jnp.max(st, axis=0, keepdims=True))
            p = jnp.exp(st - m_new)
            alpha = jnp.exp(m_old - m_new)
            l_out.append(alpha * l_all[:, lanes] + jnp.sum(p, axis=0, keepdims=True))
            m_out.append(m_new)
            pb = p.astype(BF16)
            pvs = []
            for hh in range(STREAM_HEADS):
                h = g * STREAM_HEADS + hh
                vrows = slice(h * MLA_V, (h + 1) * MLA_V)
                cols = slice(hh * tq, (hh + 1) * tq)
                pv = _dot(vt_ref[tile0, vrows, :], pb[0:KEY_TILE, cols])
                for t in range(1, STEP_TILES):
                    pv = pv + _dot(vt_ref[tile0 + t, vrows, :],
                                   pb[t * KEY_TILE:(t + 1) * KEY_TILE, cols])
                pvs.append(pv)
            updates.append((lanes, alpha, jnp.concatenate(pvs, axis=1)))
        for lanes, alpha, pv in updates:
            acc_scr[:, lanes] = alpha * acc_scr[:, lanes] + pv
        m_scr[...] = jnp.concatenate(m_out, axis=1)
        l_scr[...] = jnp.concatenate(l_out, axis=1)

    last = (i * tq) // size
    scores_to(s_a, 0)

    def two_steps(k, carry):
        scores_to(s_b, 2 * k + 1)
        fold(s_a, 2 * k, masked=False)
        scores_to(s_a, 2 * k + 2)
        fold(s_b, 2 * k + 1, masked=False)
        return carry

    lax.fori_loop(0, last // 2, two_steps, 0)

    @pl.when(last % 2 == 1)
    def _():
        scores_to(s_b, last)
        fold(s_a, last - 1, masked=False)
        fold(s_b, last, masked=True)

    @pl.when(last % 2 == 0)
    def _():
        fold(s_a, last, masked=True)

    o_t = acc_scr[...] / l_scr[...]
    for h in range(MLA_HEADS):
        o_ref[:, h * MLA_V:(h + 1) * MLA_V] = jnp.transpose(
            o_t[:, h * tq:(h + 1) * tq]).astype(o_ref.dtype)


def _mla_prompt(q, k_heads, v_t, batch, seq):
    tq = KEY_TILE
    resident = dict(pipeline_mode=pl.Buffered(1))
    nq = seq // tq
    kern = functools.partial(_mla_prompt_kernel, tq=tq)
    rows = MLA_HEADS * tq
    score_tile = (MLA_HEADS // STREAM_HEADS, STEP_TILES * KEY_TILE, STREAM_HEADS * tq)
    return pl.pallas_call(
        kern, grid=(batch, nq),
        in_specs=[
            pl.BlockSpec((MLA_HEADS, tq, HEAD_QK), lambda b, i: (0, b * nq + i, 0)),
            pl.BlockSpec((seq, MLA_HEADS * HEAD_QK), lambda b, i: (b, 0), **resident),
            pl.BlockSpec((seq // KEY_TILE, MLA_HEADS * MLA_V, KEY_TILE),
                         lambda b, i: (b, 0, 0), **resident),
        ],
        out_specs=pl.BlockSpec((tq, MLA_HEADS * MLA_V), lambda b, i: (b * nq + i, 0)),
        out_shape=jax.ShapeDtypeStruct((batch * seq, MLA_HEADS * MLA_V), BF16),
        scratch_shapes=[pltpu.VMEM((1, rows), F32), pltpu.VMEM((1, rows), F32),
                        pltpu.VMEM((MLA_V, rows), F32),
                        pltpu.VMEM(score_tile, F32), pltpu.VMEM(score_tile, F32)],
        compiler_params=pltpu.CompilerParams(
            dimension_semantics=("arbitrary", "arbitrary"), vmem_limit_bytes=VMEM_LIMIT),
        name="mla_prompt",
    )(q, k_heads, v_t)


SAMPLE_Q_ROWS = 16
PAGES_PER_CHUNK = 32
SAMPLE_SLOTS = 4
SAMPLE_PREFETCH = SAMPLE_SLOTS - 1


def _mla_sample_kernel(pt_ref, q_ref, knew_ref, wuv_ref, ckv_hbm, krt_hbm, o_ref,
                       cbuf, kbuf, cbf, sem_c, sem_k, *, n_batch, n_chunks):
    b = pl.program_id(0)
    g_pages = PAGES_PER_CHUNK

    def page_copies(bb, c, slot, g):
        page = pt_ref[bb, c * g_pages + g]
        tok = pl.ds(g * PAGE_SIZE, PAGE_SIZE)
        return (pltpu.make_async_copy(ckv_hbm.at[page], cbuf.at[slot, tok, :], sem_c.at[slot]),
                pltpu.make_async_copy(krt_hbm.at[page], kbuf.at[slot, :, tok], sem_k.at[slot]))

    def start_chunk(bb, c, slot):
        for g in range(g_pages):
            cc, ck = page_copies(bb, c, slot, g)
            cc.start()
            ck.start()

    def wait_chunk(bb, c, slot):
        for g in range(g_pages):
            cc, ck = page_copies(bb, c, slot, g)
            cc.wait()
            ck.wait()

    def start_ahead(c):
        cn = c + SAMPLE_PREFETCH
        if cn < n_chunks:
            start_chunk(b, cn, cn % SAMPLE_SLOTS)
        else:
            @pl.when(b + 1 < n_batch)
            def _():
                start_chunk(b + 1, cn - n_chunks, (cn - n_chunks) % SAMPLE_SLOTS)

    @pl.when(b == 0)
    def _():
        for c0 in range(SAMPLE_PREFETCH):
            start_chunk(0, c0, c0)

    q = q_ref[0]
    q_lat = q[:, :KV_RANK]
    q_rope = q[:, KV_RANK:KV_RANK + MLA_ROPE]
    m = jnp.full((SAMPLE_Q_ROWS, 1), -jnp.inf, F32)
    l = jnp.zeros((SAMPLE_Q_ROWS, 1), F32)
    acc = jnp.zeros((SAMPLE_Q_ROWS, KV_RANK), F32)

    n_tiles = g_pages * PAGE_SIZE // KEY_TILE

    def tiled_pv(p_b, cbf_slot):
        out = _dot(p_b[:, 0:KEY_TILE], cbf[cbf_slot, 0:KEY_TILE, :])
        for t in range(1, n_tiles):
            tok = slice(t * KEY_TILE, (t + 1) * KEY_TILE)
            out = out + _dot(p_b[:, tok], cbf[cbf_slot, tok, :])
        return out

    pending = None
    for c in range(n_chunks):
        slot = c % SAMPLE_SLOTS
        wait_chunk(b, c, slot)
        if pending is not None:
            alpha_p, p_p, slot_p = pending
            acc = alpha_p * acc + tiled_pv(p_p, slot_p)
        s_tiles = []
        for t in range(n_tiles):
            tok = slice(t * KEY_TILE, (t + 1) * KEY_TILE)
            cb = cbuf[slot, tok, :].astype(BF16)
            cbf[c % 2, tok, :] = cb
            kb = kbuf[slot, :, tok].astype(BF16)
            s_tiles.append(_dot_nt(q_lat, cb) + _dot(q_rope, kb))
        s = jnp.concatenate(s_tiles, axis=1)
        start_ahead(c)
        m_new = jnp.maximum(m, jnp.max(s, axis=-1, keepdims=True))
        p = jnp.exp(s - m_new)
        alpha = jnp.exp(m - m_new)
        l = alpha * l + jnp.sum(p, axis=-1, keepdims=True)
        m = m_new
        pending = (alpha, p.astype(BF16), c % 2)
    alpha_p, p_p, slot_p = pending
    acc = alpha_p * acc + tiled_pv(p_p, slot_p)


    knew = knew_ref[0].astype(F32)
    s_self = jnp.sum(q.astype(F32) * knew, axis=-1, keepdims=True)
    m_new = jnp.maximum(m, s_self)
    p_self = jnp.exp(s_self - m_new)
    alpha = jnp.exp(m - m_new)
    l = alpha * l + p_self
    acc = alpha * acc + p_self.astype(BF16).astype(F32) * knew[:, :KV_RANK]
    o_lat = (acc / l).astype(BF16)
    for h in range(MLA_HEADS):
        res = _dot(o_lat, wuv_ref[h])
        o_ref[0, :, h * MLA_V:(h + 1) * MLA_V] = res[h:h + 1, :]


def _mla_sample(page_table, q_s, knew, w_uv, cache_ckv, cache_krope_t):
    n_batch, n_pages = page_table.shape
    n_chunks = n_pages // PAGES_PER_CHUNK
    chunk_rows = PAGES_PER_CHUNK * PAGE_SIZE
    assert n_chunks * PAGES_PER_CHUNK == n_pages and n_chunks % SAMPLE_SLOTS == 0
    assert SAMPLE_PREFETCH <= n_chunks
    kern = functools.partial(_mla_sample_kernel, n_batch=n_batch, n_chunks=n_chunks)
    grid_spec = pltpu.PrefetchScalarGridSpec(
        num_scalar_prefetch=1,
        grid=(n_batch,),
        in_specs=[
            pl.BlockSpec((1, SAMPLE_Q_ROWS, QK_WIDTH), lambda b, pt: (b, 0, 0)),
            pl.BlockSpec((1, 1, QK_WIDTH), lambda b, pt: (b, 0, 0)),
            pl.BlockSpec((MLA_HEADS, KV_RANK, MLA_V), lambda b, pt: (0, 0, 0)),
            pl.BlockSpec(memory_space=pl.ANY),
            pl.BlockSpec(memory_space=pl.ANY),
        ],
        out_specs=pl.BlockSpec((1, 1, MLA_HEADS * MLA_V), lambda b, pt: (b, 0, 0)),
        scratch_shapes=[
            pltpu.VMEM((SAMPLE_SLOTS, chunk_rows, KV_RANK), F32),
            pltpu.VMEM((SAMPLE_SLOTS, MLA_ROPE, chunk_rows), F32),
            pltpu.VMEM((2, chunk_rows, KV_RANK), BF16),
            pltpu.SemaphoreType.DMA((SAMPLE_SLOTS,)),
            pltpu.SemaphoreType.DMA((SAMPLE_SLOTS,)),
        ],
    )
    return pl.pallas_call(
        kern, grid_spec=grid_spec,
        out_shape=jax.ShapeDtypeStruct((n_batch, 1, MLA_HEADS * MLA_V), F32),
        compiler_params=pltpu.CompilerParams(
            dimension_semantics=("arbitrary",), vmem_limit_bytes=VMEM_LIMIT),
        name="mla_sample",
    )(page_table, q_s, knew, w_uv, cache_ckv, cache_krope_t)


GLA_UNROLL = 8


def _split3(x):
    hi = x.astype(BF16)
    r1 = x - hi.astype(F32)
    mid = r1.astype(BF16)
    lo = (r1 - mid.astype(F32)).astype(BF16)
    return hi, mid, lo


def _gla_gate_out(o, gg, gnorm):
    return _rms(o, gnorm) * (gg * jax.nn.sigmoid(gg))


def _gla_prompt_kernel(gq_ref, gk_ref, gv_ref, la_ref, gg_ref, gn_ref, o_ref, s_out_ref,
                       s_scr, *, n_sub):
    t = pl.program_id(1)
    cs = GLA_CHUNK

    @pl.when(t == 0)
    def _():
        s_scr[...] = jnp.zeros(s_scr.shape, F32)

    ri = lax.broadcasted_iota(jnp.int32, (cs, cs), 0)
    ci = lax.broadcasted_iota(jnp.int32, (cs, cs), 1)
    tri = ri >= ci
    tri_b = tri.astype(BF16)
    lane_head = lax.broadcasted_iota(jnp.int32, (cs, GLA_KW), 1) // GLA_DK
    gnorm = gn_ref[...]

    def chunk(c, s_prev):
        r0 = pl.multiple_of(c * cs, cs)
        g = la_ref[pl.ds(r0, cs), :]
        g_hi, g_mid, g_lo = _split3(g)
        bcum = _dot(tri_b, g_hi) + _dot(tri_b, g_mid) + _dot(tri_b, g_lo)
        b_last = bcum[cs - 1:cs, :]
        qf = gq_ref[pl.ds(r0, cs), :]
        kf = gk_ref[pl.ds(r0, cs), :]
        v = gv_ref[pl.ds(r0, cs), :]
        q_t = qf * jnp.exp(bcum)
        k_t = (kf * jnp.exp(-bcum)).astype(BF16)
        k_h = (kf * jnp.exp(b_last - bcum)).astype(BF16)
        decay = jnp.exp(b_last)
        q_stack = jnp.concatenate(
            [jnp.where(lane_head == h, q_t, 0.0) for h in range(GLA_HEADS)],
            axis=0).astype(BF16)
        a_stack = _dot_nt(q_stack, k_t)
        o_inter = _dot(q_stack, s_prev.astype(BF16))
        kv = _dot_tn(k_h, v)
        decay_col = jnp.transpose(jnp.broadcast_to(decay, (GLA_DV, GLA_KW)))
        kv_diag = jnp.concatenate(
            [kv[h * GLA_DK:(h + 1) * GLA_DK, h * GLA_DV:(h + 1) * GLA_DV]
             for h in range(GLA_HEADS)], axis=0)
        s_new = decay_col * s_prev + kv_diag
        for h in range(GLA_HEADS):
            a_h = jnp.where(tri, a_stack[h * cs:(h + 1) * cs], 0.0).astype(BF16)
            o_h = _dot(a_h, v[:, h * GLA_DV:(h + 1) * GLA_DV]) + o_inter[h * cs:(h + 1) * cs]
            gg_h = gg_ref[pl.ds(r0, cs), h * GLA_DV:(h + 1) * GLA_DV]
            o_ref[pl.ds(r0, cs), h * GLA_DV:(h + 1) * GLA_DV] = _gla_gate_out(
                o_h, gg_h, gnorm).astype(o_ref.dtype)
        return s_new

    def trip(j, carry):
        s = s_scr[...]
        for u in range(GLA_UNROLL):
            s = chunk(j * GLA_UNROLL + u, s)
        s_scr[...] = s
        return carry

    lax.fori_loop(0, n_sub // GLA_UNROLL, trip, 0)
    s_out_ref[0] = s_scr[...]


def _gla_prompt(gq, gk, gv, la, gg, gnorm, batch, seq, tc=1024):
    nt = seq // tc
    n_sub = tc // GLA_CHUNK
    row = lambda b, t: (b * nt + t, 0)
    kern = functools.partial(_gla_prompt_kernel, n_sub=n_sub)
    return pl.pallas_call(
        kern, grid=(batch, nt),
        in_specs=[
            pl.BlockSpec((tc, GLA_KW), row),
            pl.BlockSpec((tc, GLA_KW), row),
            pl.BlockSpec((tc, GLA_WIDTH), row),
            pl.BlockSpec((tc, GLA_KW), row),
            pl.BlockSpec((tc, GLA_WIDTH), row),
            pl.BlockSpec((1, GLA_DV), lambda b, t: (0, 0)),
        ],
        out_specs=[
            pl.BlockSpec((tc, GLA_WIDTH), row),
            pl.BlockSpec((1, GLA_KW, GLA_DV), lambda b, t: (b, 0, 0)),
        ],
        out_shape=[
            jax.ShapeDtypeStruct((batch * seq, GLA_WIDTH), BF16),
            jax.ShapeDtypeStruct((batch, GLA_KW, GLA_DV), F32),
        ],
        scratch_shapes=[pltpu.VMEM((GLA_KW, GLA_DV), F32)],
        compiler_params=pltpu.CompilerParams(
            dimension_semantics=("arbitrary", "arbitrary"), vmem_limit_bytes=VMEM_LIMIT),
        name="gla_prompt",
    )(gq, gk, gv, la, gg, gnorm)


def _gla_step_kernel(s_ref, gq_ref, gk_ref, gv_ref, la_ref, gg_ref, gn_ref, o_ref, s_out_ref,
                     *, bb):
    gnorm = gn_ref[...]

    def col(x_row):
        return jnp.transpose(jnp.broadcast_to(x_row, (GLA_DV, GLA_KW)))

    for i in range(bb):
        s = s_ref[i]
        e_col = col(jnp.exp(la_ref[i:i + 1, :]))
        k_col = col(gk_ref[i:i + 1, :])
        q_col = col(gq_ref[i:i + 1, :].astype(BF16).astype(F32))
        v_row = gv_ref[i:i + 1, :].astype(F32)
        v_rows = jnp.concatenate(
            [jnp.broadcast_to(v_row[:, h * GLA_DV:(h + 1) * GLA_DV], (GLA_DK, GLA_DV))
             for h in range(GLA_HEADS)], axis=0)
        s_new = e_col * s + k_col * v_rows
        s_out_ref[i] = s_new
        prod = q_col * s_new.astype(BF16).astype(F32)
        for h in range(GLA_HEADS):
            o_h = jnp.sum(prod[h * GLA_DK:(h + 1) * GLA_DK], axis=0, keepdims=True)
            gg_h = gg_ref[i:i + 1, h * GLA_DV:(h + 1) * GLA_DV]
            o_ref[i:i + 1, h * GLA_DV:(h + 1) * GLA_DV] = _gla_gate_out(o_h, gg_h, gnorm)


def _gla_step(state, gq, gk, gv, la, gg, gnorm, bb=8):
    n = state.shape[0]
    row = lambda i: (i, 0)
    kern = functools.partial(_gla_step_kernel, bb=bb)
    return pl.pallas_call(
        kern, grid=(n // bb,),
        in_specs=[
            pl.BlockSpec((bb, GLA_KW, GLA_DV), lambda i: (i, 0, 0)),
            pl.BlockSpec((bb, GLA_KW), row),
            pl.BlockSpec((bb, GLA_KW), row),
            pl.BlockSpec((bb, GLA_WIDTH), row),
            pl.BlockSpec((bb, GLA_KW), row),
            pl.BlockSpec((bb, GLA_WIDTH), row),
            pl.BlockSpec((1, GLA_DV), lambda i: (0, 0)),
        ],
        out_specs=[
            pl.BlockSpec((bb, GLA_WIDTH), row),
            pl.BlockSpec((bb, GLA_KW, GLA_DV), lambda i: (i, 0, 0)),
        ],
        out_shape=[
            jax.ShapeDtypeStruct((n, GLA_WIDTH), F32),
            jax.ShapeDtypeStruct((n, GLA_KW, GLA_DV), F32),
        ],
        compiler_params=pltpu.CompilerParams(
            dimension_semantics=("arbitrary",), vmem_limit_bytes=VMEM_LIMIT),
        name="gla_step",
    )(state, gq, gk, gv, la, gg, gnorm)


FF_CHUNK = 1024
POST_SUBBLOCKS = 2


def _post_kernel(x_ref, mla_ref, gla_ref, wout_ref, g1_ref, b1_ref, w1_ref, w2_ref,
                 g2_ref, b2_ref, y_ref):
    half = MLA_HEADS * MLA_V
    tm = x_ref.shape[0]
    n_sub = POST_SUBBLOCKS if tm >= POST_SUBBLOCKS * KEY_TILE else 1
    sub = tm // n_sub

    def front(r):
        mix = (_dot(mla_ref[r, :].astype(BF16), wout_ref[0:half, :])
               + _dot(gla_ref[r, :].astype(BF16), wout_ref[half:, :]))
        return _layer_norm(ALPHA * x_ref[r, :] + mix, g1_ref[...], b1_ref[...])

    rows = [pl.ds(k * sub, sub) for k in range(n_sub)]
    x1s = [front(r) for r in rows]
    for r, x1 in zip(rows, x1s):
        x1b = x1.astype(BF16)
        acc = jnp.zeros(x1.shape, F32)
        for c in range(D_FF // FF_CHUNK):
            hmid = _dot(x1b, w1_ref[:, c * FF_CHUNK:(c + 1) * FF_CHUNK])
            hmid = jnp.square(jnp.maximum(hmid, 0.0)).astype(BF16)
            acc = acc + _dot(hmid, w2_ref[c * FF_CHUNK:(c + 1) * FF_CHUNK, :])
        y_ref[r, :] = _layer_norm(ALPHA * x1 + acc, g2_ref[...], b2_ref[...])


def _post(x2d, mla, gla, w, tm):
    m = x2d.shape[0]
    row = lambda i: (i, 0)
    const = lambda i: (0, 0)
    resident = dict(pipeline_mode=pl.Buffered(1))
    return pl.pallas_call(
        _post_kernel, grid=(m // tm,),
        in_specs=[
            pl.BlockSpec((tm, D_MODEL), row),
            pl.BlockSpec((tm, MLA_HEADS * MLA_V), row),
            pl.BlockSpec((tm, GLA_WIDTH), row),
            pl.BlockSpec((D_MODEL, D_MODEL), const, **resident),
            pl.BlockSpec((1, D_MODEL), const),
            pl.BlockSpec((1, D_MODEL), const),
            pl.BlockSpec((D_MODEL, D_FF), const, **resident),
            pl.BlockSpec((D_FF, D_MODEL), const, **resident),
            pl.BlockSpec((1, D_MODEL), const),
            pl.BlockSpec((1, D_MODEL), const),
        ],
        out_specs=pl.BlockSpec((tm, D_MODEL), row),
        out_shape=jax.ShapeDtypeStruct((m, D_MODEL), F32),
        compiler_params=pltpu.CompilerParams(
            dimension_semantics=("arbitrary",), vmem_limit_bytes=VMEM_LIMIT),
        name="post",
    )(x2d, mla, gla, w['w_out'], w['ln1_g'], w['ln1_b'], w['w1'], w['w2'],
      w['ln2_g'], w['ln2_b'])


def _permute_w_in(w):
    sizes = (Q_RANK, KV_RANK, MLA_ROPE, GLA_KW, GLA_KW, GLA_WIDTH, GATE_RANK, GLA_WIDTH)
    off = np.concatenate([[0], np.cumsum(sizes)]).tolist()
    cq, ckv, kr, gq, gk, gv, gr, gg = [w[:, off[i]:off[i + 1]] for i in range(8)]
    half = MLA_ROPE // 2
    pad = jnp.zeros((w.shape[0], LANES - GATE_RANK), w.dtype)
    return jnp.concatenate(
        [cq, ckv, gq, gk, gv, gg, kr, kr[:, half:], kr[:, :half], gr, pad], axis=1).astype(BF16)


def _permute_w_uq(w):
    per_head = MLA_NOPE + MLA_ROPE
    half = MLA_ROPE // 2
    nope, rope, rope_sw = [], [], []
    for h in range(MLA_HEADS):
        base = h * per_head
        nope.append(w[:, base:base + MLA_NOPE])
        r = w[:, base + MLA_NOPE:base + per_head]
        rope.append(r)
        rope_sw += [r[:, half:], r[:, :half]]
    return jnp.concatenate(nope + rope + rope_sw, axis=1).astype(BF16)


def _prep_weights(w_in, mla_q_norm, mla_w_uq, mla_kv_norm, mla_w_uk, mla_w_uv,
                  gla_w_gate2, gla_b_gate, gla_norm, w_out, ln1_g, ln1_b,
                  mlp_w1, mlp_w2, ln2_g, ln2_b, l):
    w_in_p = _permute_w_in(w_in[l])
    w_uq_p = _permute_w_uq(mla_w_uq[l])
    return dict(
        w_in=w_in_p,
        q_norm=mla_q_norm[l][None, :],
        w_uq=w_uq_p,
        kv_norm=mla_kv_norm[l][None, :],
        w_uk=jnp.transpose(mla_w_uk[l], (1, 2, 0)).astype(BF16),
        w_ukv=jnp.concatenate(
            [mla_w_uk[l].reshape(KV_RANK, MLA_HEADS * MLA_NOPE),
             mla_w_uv[l].reshape(KV_RANK, MLA_HEADS * MLA_V)], axis=1).astype(BF16),
        w_uv=jnp.transpose(mla_w_uv[l], (1, 0, 2)).astype(BF16),
        w_gate2=jnp.pad(gla_w_gate2[l], ((0, LANES - GATE_RANK), (0, 0))).astype(BF16),
        b_gate=gla_b_gate[l][None, :],
        gla_norm=gla_norm[l][None, :],
        w_out=w_out[l].astype(BF16),
        ln1_g=ln1_g[l][None, :], ln1_b=ln1_b[l][None, :],
        w1=mlp_w1[l].astype(BF16), w2=mlp_w2[l].astype(BF16),
        ln2_g=ln2_g[l][None, :], ln2_b=ln2_b[l][None, :],
    )


def _rope_tables(pos):
    half = MLA_ROPE // 2
    inv = ROPE_THETA ** (-jnp.arange(0, MLA_ROPE, 2, dtype=F32) / MLA_ROPE)
    inv4 = jnp.tile(inv, LANES // half)
    sign = jnp.tile(jnp.concatenate([-jnp.ones((half,), F32), jnp.ones((half,), F32)]),
                    LANES // MLA_ROPE)
    ang = pos.astype(F32)[:, None] * inv4[None, :]
    return jnp.cos(ang), jnp.sin(ang) * sign[None, :]


def kernel(x_prompt, x_sample, cache_ckv, cache_krope, state_gla, page_table, w_in,
           mla_q_norm, mla_w_uq, mla_kv_norm, mla_w_uk, mla_w_uv, gla_w_gate2, gla_b_gate,
           gla_norm, w_out, ln1_g, ln1_b, mlp_w1, mlp_w2, ln2_g, ln2_b):
    assert w_in.shape[0] == DEPTH == 1
    batch, seq, _ = x_prompt.shape
    n_dec, t_new, _ = x_sample.shape
    assert t_new == 1
    l = 0
    w = _prep_weights(w_in, mla_q_norm, mla_w_uq, mla_kv_norm, mla_w_uk, mla_w_uv,
                      gla_w_gate2, gla_b_gate, gla_norm, w_out, ln1_g, ln1_b,
                      mlp_w1, mlp_w2, ln2_g, ln2_b, l)

    xp = x_prompt.reshape(batch * seq, D_MODEL)
    cos_p, sin_p = _rope_tables(jnp.arange(seq, dtype=jnp.int32))
    q, k_heads, ckv_p, kr_p, gq, gk, gv, la, gg, v_t = _proj(xp, cos_p, sin_p, w, 512, True)
    mla_p = _mla_prompt(q, k_heads, v_t, batch, seq)
    gla_p, s_p = _gla_prompt(gq, gk, gv, la, gg, w['gla_norm'], batch, seq)
    y_p = _post(xp, mla_p, gla_p, w, tm=512)

    xs = x_sample.reshape(n_dec, D_MODEL)
    cos_s, sin_s = _rope_tables(jnp.full((n_dec,), PAST_LEN, dtype=jnp.int32))
    q, kcat, ckv_s, kr_s, gq, gk, gv, la, gg = _proj(xs, cos_s, sin_s, w, n_dec, False)
    q_s = jnp.pad(jnp.transpose(q, (1, 0, 2)),
                  ((0, 0), (0, SAMPLE_Q_ROWS - MLA_HEADS), (0, 0)))
    krope_t = jnp.swapaxes(cache_krope[l], 1, 2)
    mla_s = _mla_sample(page_table, q_s, kcat[:, None, :], w['w_uv'],
                        cache_ckv[l], krope_t)
    gla_s, s_s = _gla_step(state_gla[l].reshape(n_dec, GLA_KW, GLA_DV),
                           gq, gk, gv, la, gg, w['gla_norm'])
    y_s = _post(xs, mla_s.reshape(n_dec, MLA_HEADS * MLA_V), gla_s, w, tm=n_dec)

    return (y_p.reshape(batch, seq, D_MODEL),
            y_s.reshape(n_dec, 1, D_MODEL),
            ckv_p.reshape(1, batch, seq, KV_RANK),
            kr_p.reshape(1, batch, seq, MLA_ROPE),
            s_p.reshape(1, batch, GLA_HEADS, GLA_DK, GLA_DV),
            ckv_s.reshape(1, n_dec, 1, KV_RANK),
            kr_s.reshape(1, n_dec, 1, MLA_ROPE),
            s_s.reshape(1, n_dec, GLA_HEADS, GLA_DK, GLA_DV))
```

```python
import functools

import numpy as np
import jax
import jax.numpy as jnp
from jax import lax
from jax.experimental import pallas as pl
from jax.experimental.pallas import tpu as pltpu

D_MODEL = 1024
PAST_LEN = 16384
PAGE_SIZE = 128
MLA_HEADS = 4
MLA_V = 128
MLA_NOPE = 128
MLA_ROPE = 64
Q_RANK = 384
KV_RANK = 256
MLA_SCALE = (MLA_NOPE + MLA_ROPE) ** -0.5
LOG2_E = 1.4426950408889634
ROPE_THETA = 10000.0
GLA_HEADS = 4
GLA_DV = 128
GLA_DK = 64
GLA_WIDTH = GLA_HEADS * GLA_DV
GLA_KW = GLA_HEADS * GLA_DK
GATE_RANK = 16
GATE_TAU = 16.0
GLA_CHUNK = 64
D_FF = 4 * D_MODEL
DEPTH = 1
ALPHA = (2.0 * DEPTH) ** 0.25
EPS = 1e-5

LANES = 128
BF16_SUBLANES = 16
QK_WIDTH = KV_RANK + LANES
HEAD_QK = MLA_NOPE + LANES
VMEM_LIMIT = 56 * 1024 * 1024
KEY_TILE = 256
STREAM_HEADS = 4
STEP_TILES = 2

_C_CQ = 0
_C_CKV = _C_CQ + Q_RANK
_C_GQ = _C_CKV + KV_RANK
_C_GK = _C_GQ + GLA_KW
_C_GV = _C_GK + GLA_KW
_C_GG = _C_GV + GLA_WIDTH
_C_KR = _C_GG + GLA_WIDTH
_C_GR = _C_KR + 2 * MLA_ROPE
IN_COLS_P = _C_GR + LANES

BF16 = jnp.bfloat16
F32 = jnp.float32


def _dot(a, b):
    return jnp.dot(a, b, preferred_element_type=F32)


def _dot_nt(a, b):
    return lax.dot_general(a, b, (((1,), (1,)), ((), ())), preferred_element_type=F32)


def _dot_tn(a, b):
    return lax.dot_general(a, b, (((0,), (0,)), ((), ())), preferred_element_type=F32)


def _rms(x, g):
    return x * lax.rsqrt(jnp.mean(x * x, axis=-1, keepdims=True) + EPS) * g


def _layer_norm(x, g, b):
    mu = jnp.mean(x, axis=-1, keepdims=True)
    xc = x - mu
    var = jnp.mean(xc * xc, axis=-1, keepdims=True)
    return xc * lax.rsqrt(var + EPS) * g + b


def _proj_kernel(x_ref, cos_ref, sin_ref, win_ref, qn_ref, wuq_ref, kvn_ref, wuk_ref,
                 wukv_ref, wg2_ref, bg_ref,
                 q_ref, key_ref, ckv_ref, kr_ref, gq_ref, gk_ref, gv_ref, la_ref, gg_ref,
                 vt_ref=None):
    tm = x_ref.shape[0]
    sub = KEY_TILE if tm % KEY_TILE == 0 else tm
    for k in range(tm // sub):
        _proj_rows(pl.ds(k * sub, sub), k, x_ref, cos_ref, sin_ref, win_ref, qn_ref, wuq_ref,
                   kvn_ref, wuk_ref, wukv_ref, wg2_ref, bg_ref, q_ref, key_ref, ckv_ref,
                   kr_ref, gq_ref, gk_ref, gv_ref, la_ref, gg_ref, vt_ref)


def _proj_rows(r, k, x_ref, cos_ref, sin_ref, win_ref, qn_ref, wuq_ref, kvn_ref, wuk_ref,
               wukv_ref, wg2_ref, bg_ref, q_ref, key_ref, ckv_ref, kr_ref, gq_ref, gk_ref,
               gv_ref, la_ref, gg_ref, vt_ref):
    xb = x_ref[r, :].astype(BF16)
    cos = cos_ref[r, :]
    sin = sin_ref[r, :]

    cq = _dot(xb, win_ref[:, _C_CQ:_C_CQ + Q_RANK])
    cqn = _rms(cq, qn_ref[...]).astype(BF16)
    q = _dot(cqn, wuq_ref[...])
    nh = MLA_HEADS * MLA_NOPE
    nr = MLA_HEADS * MLA_ROPE
    low_half = lax.broadcasted_iota(jnp.int32, (1, LANES), 1) < MLA_ROPE
    per_head = vt_ref is not None
    q_main = MLA_NOPE if per_head else KV_RANK
    q_scale = MLA_SCALE * LOG2_E if per_head else MLA_SCALE
    for h in range(MLA_HEADS):
        nope = q[:, h * MLA_NOPE:(h + 1) * MLA_NOPE]
        if not per_head:
            nope = _dot(nope.astype(BF16), wuk_ref[h])
        q_ref[h, r, 0:q_main] = (nope * q_scale).astype(BF16)
    for pair in range(MLA_HEADS // 2):
        lanes = slice(nh + pair * LANES, nh + (pair + 1) * LANES)
        lanes_sw = slice(nh + nr + pair * LANES, nh + nr + (pair + 1) * LANES)
        rope2 = (q[:, lanes] * cos + q[:, lanes_sw] * sin) * q_scale
        for j, blk in enumerate((rope2, pltpu.roll(rope2, MLA_ROPE, 1))):
            q_ref[2 * pair + j, r, q_main:q_main + LANES] = jnp.where(
                low_half, blk, 0.0).astype(BF16)

    ckv = _rms(_dot(xb, win_ref[:, _C_CKV:_C_CKV + KV_RANK]), kvn_ref[...])
    ckv_ref[r, :] = ckv
    kr_gr = _dot(xb, win_ref[:, _C_KR:_C_GR + LANES])
    krr = kr_gr[:, :2 * MLA_ROPE]
    k_rope = (krr[:, :MLA_ROPE] * cos[:, :MLA_ROPE]
              + krr[:, MLA_ROPE:] * sin[:, :MLA_ROPE])
    kr_ref[r, :] = k_rope
    ckv_b = ckv.astype(BF16)
    kr_pad = jnp.concatenate([k_rope, jnp.zeros_like(k_rope)], axis=-1).astype(BF16)
    if per_head:
        kv = _dot(ckv_b, wukv_ref[...])
        hk = MLA_HEADS * MLA_NOPE
        for h in range(MLA_HEADS):
            base = h * HEAD_QK
            key_ref[r, base:base + MLA_NOPE] = kv[:, h * MLA_NOPE:(h + 1) * MLA_NOPE].astype(BF16)
            key_ref[r, base + MLA_NOPE:base + HEAD_QK] = kr_pad
        vt_ref[k] = jnp.transpose(kv[:, hk:]).astype(BF16)
    else:
        key_ref[r, 0:KV_RANK] = ckv_b
        key_ref[r, KV_RANK:QK_WIDTH] = kr_pad

    gq_ref[r, :] = _dot(xb, win_ref[:, _C_GQ:_C_GQ + GLA_KW]) * (GLA_DK ** -0.5)
    gk_ref[r, :] = _dot(xb, win_ref[:, _C_GK:_C_GK + GLA_KW])
    gv_ref[r, :] = _dot(xb, win_ref[:, _C_GV:_C_GV + GLA_WIDTH]).astype(gv_ref.dtype)
    gg_ref[r, :] = _dot(xb, win_ref[:, _C_GG:_C_GG + GLA_WIDTH])
    gr = kr_gr[:, 2 * MLA_ROPE:].astype(BF16)
    z = _dot(gr, wg2_ref[...]) + bg_ref[...]
    la_ref[r, :] = (jnp.minimum(z, 0.0) - jnp.log1p(jnp.exp(-jnp.abs(z)))) / GATE_TAU


def _proj(x2d, cos_t, sin_t, w, tm, per_head):
    m = x2d.shape[0]
    q_width = HEAD_QK if per_head else QK_WIDTH
    key_width = MLA_HEADS * HEAD_QK if per_head else QK_WIDTH
    grid = (m // tm,)
    row = lambda i: (i, 0)
    const2 = lambda i: (0, 0)
    const3 = lambda i: (0, 0, 0)
    const3_row = lambda i: (i, 0, 0)
    pos_blocks = cos_t.shape[0] // tm
    pos_row = lambda i: (i % pos_blocks, 0)
    in_specs = [
        pl.BlockSpec((tm, D_MODEL), row),
        pl.BlockSpec((tm, LANES), pos_row),
        pl.BlockSpec((tm, LANES), pos_row),
        pl.BlockSpec((D_MODEL, IN_COLS_P), const2),
        pl.BlockSpec((1, Q_RANK), const2),
        pl.BlockSpec(w['w_uq'].shape, const2),
        pl.BlockSpec((1, KV_RANK), const2),
        pl.BlockSpec((MLA_HEADS, MLA_NOPE, KV_RANK), const3),
        pl.BlockSpec(w['w_ukv'].shape, const2),
        pl.BlockSpec((LANES, GLA_KW), const2),
        pl.BlockSpec((1, GLA_KW), const2),
    ]
    out_shape = [
        jax.ShapeDtypeStruct((MLA_HEADS, m, q_width), BF16),
        jax.ShapeDtypeStruct((m, key_width), BF16),
        jax.ShapeDtypeStruct((m, KV_RANK), F32),
        jax.ShapeDtypeStruct((m, MLA_ROPE), F32),
        jax.ShapeDtypeStruct((m, GLA_KW), F32),
        jax.ShapeDtypeStruct((m, GLA_KW), F32),
        jax.ShapeDtypeStruct((m, GLA_WIDTH), BF16 if per_head else F32),
        jax.ShapeDtypeStruct((m, GLA_KW), F32),
        jax.ShapeDtypeStruct((m, GLA_WIDTH), F32),
    ]
    out_specs = [
        pl.BlockSpec((MLA_HEADS, tm, q_width), lambda i: (0, i, 0)),
        pl.BlockSpec((tm, key_width), row),
        pl.BlockSpec((tm, KV_RANK), row),
        pl.BlockSpec((tm, MLA_ROPE), row),
        pl.BlockSpec((tm, GLA_KW), row),
        pl.BlockSpec((tm, GLA_KW), row),
        pl.BlockSpec((tm, GLA_WIDTH), row),
        pl.BlockSpec((tm, GLA_KW), row),
        pl.BlockSpec((tm, GLA_WIDTH), row),
    ]
    if per_head:
        vt_rows = MLA_HEADS * MLA_V
        out_shape.append(jax.ShapeDtypeStruct((m // KEY_TILE, vt_rows, KEY_TILE), BF16))
        out_specs.append(pl.BlockSpec((tm // KEY_TILE, vt_rows, KEY_TILE), const3_row))
    return pl.pallas_call(
        _proj_kernel, grid=grid, in_specs=in_specs, out_specs=out_specs, out_shape=out_shape,
        compiler_params=pltpu.CompilerParams(
            dimension_semantics=("arbitrary",), vmem_limit_bytes=VMEM_LIMIT),
        name="proj",
    )(x2d, cos_t, sin_t, w['w_in'], w['q_norm'], w['w_uq'], w['kv_norm'], w['w_uk'],
      w['w_ukv'], w['w_gate2'], w['b_gate'])


def _mla_prompt_kernel(q_ref, k_ref, vt_ref, o_ref, m_scr, l_scr, acc_scr, s_a, s_b, *, tq):
    i = pl.program_id(1)
    rows = MLA_HEADS * tq
    size = STEP_TILES * KEY_TILE
    hw = STREAM_HEADS * tq
    n_streams = MLA_HEADS // STREAM_HEADS
    m_scr[...] = jnp.full(m_scr.shape, -jnp.inf, F32)
    l_scr[...] = jnp.zeros(l_scr.shape, F32)
    acc_scr[...] = jnp.zeros(acc_scr.shape, F32)

    def scores_to(s_ref, j):
        start = pl.multiple_of(j * size, size)
        for h in range(MLA_HEADS):
            g, hh = divmod(h, STREAM_HEADS)
            kh = k_ref[pl.ds(start, size), h * HEAD_QK:(h + 1) * HEAD_QK]
            s_ref[g, :, hh * tq:(hh + 1) * tq] = _dot_nt(kh, q_ref[h])

    def fold(s_ref, j, masked):
        tile0 = j * STEP_TILES
        if masked:
            key = j * size + lax.broadcasted_iota(jnp.int32, (size, hw), 0)
            tok = i * tq + (lax.broadcasted_iota(jnp.int32, (size, hw), 1) & (tq - 1))
            visible = key <= tok
        m_all = m_scr[...]
        l_all = l_scr[...]
        m_out, l_out, updates = [], [], []
        ones_rows = jnp.ones((BF16_SUBLANES, KEY_TILE), BF16)
        for g in range(n_streams):
            lanes = slice(g * hw, (g + 1) * hw)
            st = s_ref[g]
            if masked:
                st = jnp.where(visible, st, -jnp.inf)
            m_old = m_all[:, lanes]
            m_new = jnp.maximum(m_old, jnp.max(st, axis=0, keepdims=True))
            p = jnp.exp2(st - m_new)
            alpha = jnp.exp2(m_old - m_new)
            m_out.append(m_new)
            pb = p.astype(BF16)
            pvs, sums = [], []
            for hh in range(STREAM_HEADS):
                h = g * STREAM_HEADS + hh
                vrows = slice(h * MLA_V, (h + 1) * MLA_V)
                cols = slice(hh * tq, (hh + 1) * tq)
                pv = None
                for t in range(STEP_TILES):
                    lhs = jnp.concatenate([vt_ref[tile0 + t, vrows, :], ones_rows], axis=0)
                    d = _dot(lhs, pb[t * KEY_TILE:(t + 1) * KEY_TILE, cols])
                    pv = d if pv is None else pv + d
                pvs.append(pv[:MLA_V])
                sums.append(pv[MLA_V:MLA_V + 1])
            l_out.append(alpha * l_all[:, lanes] + jnp.concatenate(sums, axis=1))
            updates.append((lanes, alpha, jnp.concatenate(pvs, axis=1)))
        for lanes, alpha, pv in updates:
            acc_scr[:, lanes] = alpha * acc_scr[:, lanes] + pv
        m_scr[...] = jnp.concatenate(m_out, axis=1)
        l_scr[...] = jnp.concatenate(l_out, axis=1)

    last = (i * tq) // size
    scores_to(s_a, 0)

    def two_steps(k, carry):
        scores_to(s_b, 2 * k + 1)
        fold(s_a, 2 * k, masked=False)
        scores_to(s_a, 2 * k + 2)
        fold(s_b, 2 * k + 1, masked=False)
        return carry

    lax.fori_loop(0, last // 2, two_steps, 0)

    @pl.when(last % 2 == 1)
    def _():
        scores_to(s_b, last)
        fold(s_a, last - 1, masked=False)
        fold(s_b, last, masked=True)

    @pl.when(last % 2 == 0)
    def _():
        fold(s_a, last, masked=True)

    o_t = acc_scr[...] / l_scr[...]
    for h in range(MLA_HEADS):
        o_ref[:, h * MLA_V:(h + 1) * MLA_V] = jnp.transpose(
            o_t[:, h * tq:(h + 1) * tq]).astype(o_ref.dtype)


def _mla_prompt(q, k_heads, v_t, batch, seq):
    tq = KEY_TILE
    resident = dict(pipeline_mode=pl.Buffered(1))
    nq = seq // tq
    kern = functools.partial(_mla_prompt_kernel, tq=tq)
    rows = MLA_HEADS * tq
    score_tile = (MLA_HEADS // STREAM_HEADS, STEP_TILES * KEY_TILE, STREAM_HEADS * tq)
    return pl.pallas_call(
        kern, grid=(batch, nq),
        in_specs=[
            pl.BlockSpec((MLA_HEADS, tq, HEAD_QK), lambda b, i: (0, b * nq + i, 0)),
            pl.BlockSpec((seq, MLA_HEADS * HEAD_QK), lambda b, i: (b, 0), **resident),
            pl.BlockSpec((seq // KEY_TILE, MLA_HEADS * MLA_V, KEY_TILE),
                         lambda b, i: (b, 0, 0), **resident),
        ],
        out_specs=pl.BlockSpec((tq, MLA_HEADS * MLA_V), lambda b, i: (b * nq + i, 0)),
        out_shape=jax.ShapeDtypeStruct((batch * seq, MLA_HEADS * MLA_V), BF16),
        scratch_shapes=[pltpu.VMEM((1, rows), F32), pltpu.VMEM((1, rows), F32),
                        pltpu.VMEM((MLA_V, rows), F32),
                        pltpu.VMEM(score_tile, F32), pltpu.VMEM(score_tile, F32)],
        compiler_params=pltpu.CompilerParams(
            dimension_semantics=("arbitrary", "arbitrary"), vmem_limit_bytes=VMEM_LIMIT),
        name="mla_prompt",
    )(q, k_heads, v_t)


SAMPLE_Q_ROWS = 16
PAGES_PER_CHUNK = 32
SAMPLE_SLOTS = 4
SAMPLE_PREFETCH = SAMPLE_SLOTS - 1


def _mla_sample_kernel(pt_ref, q_ref, knew_ref, wuv_ref, ckv_hbm, krt_hbm, o_ref,
                       cbuf, kbuf, cbf, sem_c, sem_k, *, n_batch, n_chunks):
    b = pl.program_id(0)
    g_pages = PAGES_PER_CHUNK

    def page_copies(bb, c, slot, g):
        page = pt_ref[bb, c * g_pages + g]
        tok = pl.ds(g * PAGE_SIZE, PAGE_SIZE)
        return (pltpu.make_async_copy(ckv_hbm.at[page], cbuf.at[slot, tok, :], sem_c.at[slot]),
                pltpu.make_async_copy(krt_hbm.at[page], kbuf.at[slot, :, tok], sem_k.at[slot]))

    def start_chunk(bb, c, slot):
        for g in range(g_pages):
            cc, ck = page_copies(bb, c, slot, g)
            cc.start()
            ck.start()

    def wait_chunk(bb, c, slot):
        for g in range(g_pages):
            cc, ck = page_copies(bb, c, slot, g)
            cc.wait()
            ck.wait()

    def start_ahead(c):
        cn = c + SAMPLE_PREFETCH
        if cn < n_chunks:
            start_chunk(b, cn, cn % SAMPLE_SLOTS)
        else:
            @pl.when(b + 1 < n_batch)
            def _():
                start_chunk(b + 1, cn - n_chunks, (cn - n_chunks) % SAMPLE_SLOTS)

    @pl.when(b == 0)
    def _():
        for c0 in range(SAMPLE_PREFETCH):
            start_chunk(0, c0, c0)

    q = q_ref[0]
    q_lat = q[:, :KV_RANK]
    q_rope = q[:, KV_RANK:KV_RANK + MLA_ROPE]
    m = jnp.full((SAMPLE_Q_ROWS, 1), -jnp.inf, F32)
    l = jnp.zeros((SAMPLE_Q_ROWS, 1), F32)
    acc = jnp.zeros((SAMPLE_Q_ROWS, KV_RANK), F32)

    n_tiles = g_pages * PAGE_SIZE // KEY_TILE

    def tiled_pv(p_b, cbf_slot):
        out = _dot(p_b[:, 0:KEY_TILE], cbf[cbf_slot, 0:KEY_TILE, :])
        for t in range(1, n_tiles):
            tok = slice(t * KEY_TILE, (t + 1) * KEY_TILE)
            out = out + _dot(p_b[:, tok], cbf[cbf_slot, tok, :])
        return out

    pending = None
    for c in range(n_chunks):
        slot = c % SAMPLE_SLOTS
        wait_chunk(b, c, slot)
        if pending is not None:
            alpha_p, p_p, slot_p = pending
            acc = alpha_p * acc + tiled_pv(p_p, slot_p)
        s_tiles = []
        for t in range(n_tiles):
            tok = slice(t * KEY_TILE, (t + 1) * KEY_TILE)
            cb = cbuf[slot, tok, :].astype(BF16)
            cbf[c % 2, tok, :] = cb
            kb = kbuf[slot, :, tok].astype(BF16)
            s_tiles.append(_dot_nt(q_lat, cb) + _dot(q_rope, kb))
        s = jnp.concatenate(s_tiles, axis=1)
        start_ahead(c)
        m_new = jnp.maximum(m, jnp.max(s, axis=-1, keepdims=True))
        p = jnp.exp(s - m_new)
        alpha = jnp.exp(m - m_new)
        l = alpha * l + jnp.sum(p, axis=-1, keepdims=True)
        m = m_new
        pending = (alpha, p.astype(BF16), c % 2)
    alpha_p, p_p, slot_p = pending
    acc = alpha_p * acc + tiled_pv(p_p, slot_p)


    knew = knew_ref[0].astype(F32)
    s_self = jnp.sum(q.astype(F32) * knew, axis=-1, keepdims=True)
    m_new = jnp.maximum(m, s_self)
    p_self = jnp.exp(s_self - m_new)
    alpha = jnp.exp(m - m_new)
    l = alpha * l + p_self
    acc = alpha * acc + p_self.astype(BF16).astype(F32) * knew[:, :KV_RANK]
    o_lat = (acc / l).astype(BF16)
    for h in range(MLA_HEADS):
        res = _dot(o_lat, wuv_ref[h])
        o_ref[0, :, h * MLA_V:(h + 1) * MLA_V] = res[h:h + 1, :]


def _mla_sample(page_table, q_s, knew, w_uv, cache_ckv, cache_krope_t):
    n_batch, n_pages = page_table.shape
    n_chunks = n_pages // PAGES_PER_CHUNK
    chunk_rows = PAGES_PER_CHUNK * PAGE_SIZE
    assert n_chunks * PAGES_PER_CHUNK == n_pages and n_chunks % SAMPLE_SLOTS == 0
    assert SAMPLE_PREFETCH <= n_chunks
    kern = functools.partial(_mla_sample_kernel, n_batch=n_batch, n_chunks=n_chunks)
    grid_spec = pltpu.PrefetchScalarGridSpec(
        num_scalar_prefetch=1,
        grid=(n_batch,),
        in_specs=[
            pl.BlockSpec((1, SAMPLE_Q_ROWS, QK_WIDTH), lambda b, pt: (b, 0, 0)),
            pl.BlockSpec((1, 1, QK_WIDTH), lambda b, pt: (b, 0, 0)),
            pl.BlockSpec((MLA_HEADS, KV_RANK, MLA_V), lambda b, pt: (0, 0, 0)),
            pl.BlockSpec(memory_space=pl.ANY),
            pl.BlockSpec(memory_space=pl.ANY),
        ],
        out_specs=pl.BlockSpec((1, 1, MLA_HEADS * MLA_V), lambda b, pt: (b, 0, 0)),
        scratch_shapes=[
            pltpu.VMEM((SAMPLE_SLOTS, chunk_rows, KV_RANK), F32),
            pltpu.VMEM((SAMPLE_SLOTS, MLA_ROPE, chunk_rows), F32),
            pltpu.VMEM((2, chunk_rows, KV_RANK), BF16),
            pltpu.SemaphoreType.DMA((SAMPLE_SLOTS,)),
            pltpu.SemaphoreType.DMA((SAMPLE_SLOTS,)),
        ],
    )
    return pl.pallas_call(
        kern, grid_spec=grid_spec,
        out_shape=jax.ShapeDtypeStruct((n_batch, 1, MLA_HEADS * MLA_V), F32),
        compiler_params=pltpu.CompilerParams(
            dimension_semantics=("arbitrary",), vmem_limit_bytes=VMEM_LIMIT),
        name="mla_sample",
    )(page_table, q_s, knew, w_uv, cache_ckv, cache_krope_t)


GLA_UNROLL = 8


def _split3(x):
    hi = x.astype(BF16)
    r1 = x - hi.astype(F32)
    mid = r1.astype(BF16)
    lo = (r1 - mid.astype(F32)).astype(BF16)
    return hi, mid, lo


def _gla_gate_out(o, gg, gnorm):
    return _rms(o, gnorm) * (gg * jax.nn.sigmoid(gg))


def _gla_prompt_kernel(gq_ref, gk_ref, gv_ref, la_ref, gg_ref, gn_ref, o_ref, s_out_ref,
                       s_scr, *, n_sub):
    t = pl.program_id(1)
    cs = GLA_CHUNK

    @pl.when(t == 0)
    def _():
        s_scr[...] = jnp.zeros(s_scr.shape, F32)

    ri = lax.broadcasted_iota(jnp.int32, (cs, cs), 0)
    ci = lax.broadcasted_iota(jnp.int32, (cs, cs), 1)
    tri = ri >= ci
    tri_b = tri.astype(BF16)
    lane_head = lax.broadcasted_iota(jnp.int32, (cs, GLA_KW), 1) // GLA_DK
    gnorm = gn_ref[...]

    def chunk(c, s_prev):
        r0 = pl.multiple_of(c * cs, cs)
        g = la_ref[pl.ds(r0, cs), :]
        g_hi, g_mid, g_lo = _split3(g)
        bcum = _dot(tri_b, g_hi) + _dot(tri_b, g_mid) + _dot(tri_b, g_lo)
        b_last = bcum[cs - 1:cs, :]
        qf = gq_ref[pl.ds(r0, cs), :]
        kf = gk_ref[pl.ds(r0, cs), :]
        v = gv_ref[pl.ds(r0, cs), :]
        q_t = qf * jnp.exp(bcum)
        k_t = (kf * jnp.exp(-bcum)).astype(BF16)
        k_h = (kf * jnp.exp(b_last - bcum)).astype(BF16)
        decay = jnp.exp(b_last)
        q_stack = jnp.concatenate(
            [jnp.where(lane_head == h, q_t, 0.0) for h in range(GLA_HEADS)],
            axis=0).astype(BF16)
        a_stack = _dot_nt(q_stack, k_t)
        o_inter = _dot(q_stack, s_prev.astype(BF16))
        kv = _dot_tn(k_h, v)
        decay_col = jnp.transpose(jnp.broadcast_to(decay, (GLA_DV, GLA_KW)))
        kv_diag = jnp.concatenate(
            [kv[h * GLA_DK:(h + 1) * GLA_DK, h * GLA_DV:(h + 1) * GLA_DV]
             for h in range(GLA_HEADS)], axis=0)
        s_new = decay_col * s_prev + kv_diag
        for h in range(GLA_HEADS):
            a_h = jnp.where(tri, a_stack[h * cs:(h + 1) * cs], 0.0).astype(BF16)
            o_h = _dot(a_h, v[:, h * GLA_DV:(h + 1) * GLA_DV]) + o_inter[h * cs:(h + 1) * cs]
            gg_h = gg_ref[pl.ds(r0, cs), h * GLA_DV:(h + 1) * GLA_DV]
            o_ref[pl.ds(r0, cs), h * GLA_DV:(h + 1) * GLA_DV] = _gla_gate_out(
                o_h, gg_h, gnorm).astype(o_ref.dtype)
        return s_new

    def trip(j, carry):
        s = s_scr[...]
        for u in range(GLA_UNROLL):
            s = chunk(j * GLA_UNROLL + u, s)
        s_scr[...] = s
        return carry

    lax.fori_loop(0, n_sub // GLA_UNROLL, trip, 0)
    s_out_ref[0] = s_scr[...]


def _gla_prompt(gq, gk, gv, la, gg, gnorm, batch, seq, tc=1024):
    nt = seq // tc
    n_sub = tc // GLA_CHUNK
    row = lambda b, t: (b * nt + t, 0)
    kern = functools.partial(_gla_prompt_kernel, n_sub=n_sub)
    return pl.pallas_call(
        kern, grid=(batch, nt),
        in_specs=[
            pl.BlockSpec((tc, GLA_KW), row),
            pl.BlockSpec((tc, GLA_KW), row),
            pl.BlockSpec((tc, GLA_WIDTH), row),
            pl.BlockSpec((tc, GLA_KW), row),
            pl.BlockSpec((tc, GLA_WIDTH), row),
            pl.BlockSpec((1, GLA_DV), lambda b, t: (0, 0)),
        ],
        out_specs=[
            pl.BlockSpec((tc, GLA_WIDTH), row),
            pl.BlockSpec((1, GLA_KW, GLA_DV), lambda b, t: (b, 0, 0)),
        ],
        out_shape=[
            jax.ShapeDtypeStruct((batch * seq, GLA_WIDTH), BF16),
            jax.ShapeDtypeStruct((batch, GLA_KW, GLA_DV), F32),
        ],
        scratch_shapes=[pltpu.VMEM((GLA_KW, GLA_DV), F32)],
        compiler_params=pltpu.CompilerParams(
            dimension_semantics=("arbitrary", "arbitrary"), vmem_limit_bytes=VMEM_LIMIT),
        name="gla_prompt",
    )(gq, gk, gv, la, gg, gnorm)


def _gla_step_kernel(s_ref, gq_ref, gk_ref, gv_ref, la_ref, gg_ref, gn_ref, o_ref, s_out_ref,
                     *, bb):
    gnorm = gn_ref[...]

    def col(x_row):
        return jnp.transpose(jnp.broadcast_to(x_row, (GLA_DV, GLA_KW)))

    for i in range(bb):
        s = s_ref[i]
        e_col = col(jnp.exp(la_ref[i:i + 1, :]))
        k_col = col(gk_ref[i:i + 1, :])
        q_col = col(gq_ref[i:i + 1, :].astype(BF16).astype(F32))
        v_row = gv_ref[i:i + 1, :].astype(F32)
        v_rows = jnp.concatenate(
            [jnp.broadcast_to(v_row[:, h * GLA_DV:(h + 1) * GLA_DV], (GLA_DK, GLA_DV))
             for h in range(GLA_HEADS)], axis=0)
        s_new = e_col * s + k_col * v_rows
        s_out_ref[i] = s_new
        prod = q_col * s_new.astype(BF16).astype(F32)
        for h in range(GLA_HEADS):
            o_h = jnp.sum(prod[h * GLA_DK:(h + 1) * GLA_DK], axis=0, keepdims=True)
            gg_h = gg_ref[i:i + 1, h * GLA_DV:(h + 1) * GLA_DV]
            o_ref[i:i + 1, h * GLA_DV:(h + 1) * GLA_DV] = _gla_gate_out(o_h, gg_h, gnorm)


def _gla_step(state, gq, gk, gv, la, gg, gnorm, bb=8):
    n = state.shape[0]
    row = lambda i: (i, 0)
    kern = functools.partial(_gla_step_kernel, bb=bb)
    return pl.pallas_call(
        kern, grid=(n // bb,),
        in_specs=[
            pl.BlockSpec((bb, GLA_KW, GLA_DV), lambda i: (i, 0, 0)),
            pl.BlockSpec((bb, GLA_KW), row),
            pl.BlockSpec((bb, GLA_KW), row),
            pl.BlockSpec((bb, GLA_WIDTH), row),
            pl.BlockSpec((bb, GLA_KW), row),
            pl.BlockSpec((bb, GLA_WIDTH), row),
            pl.BlockSpec((1, GLA_DV), lambda i: (0, 0)),
        ],
        out_specs=[
            pl.BlockSpec((bb, GLA_WIDTH), row),
            pl.BlockSpec((bb, GLA_KW, GLA_DV), lambda i: (i, 0, 0)),
        ],
        out_shape=[
            jax.ShapeDtypeStruct((n, GLA_WIDTH), F32),
            jax.ShapeDtypeStruct((n, GLA_KW, GLA_DV), F32),
        ],
        compiler_params=pltpu.CompilerParams(
            dimension_semantics=("arbitrary",), vmem_limit_bytes=VMEM_LIMIT),
        name="gla_step",
    )(state, gq, gk, gv, la, gg, gnorm)


FF_CHUNK = 1024
POST_SUBBLOCKS = 2


def _post_kernel(x_ref, mla_ref, gla_ref, wout_ref, g1_ref, b1_ref, w1_ref, w2_ref,
                 g2_ref, b2_ref, y_ref):
    half = MLA_HEADS * MLA_V
    tm = x_ref.shape[0]
    n_sub = POST_SUBBLOCKS if tm >= POST_SUBBLOCKS * KEY_TILE else 1
    sub = tm // n_sub

    def front(r):
        mix = (_dot(mla_ref[r, :].astype(BF16), wout_ref[0:half, :])
               + _dot(gla_ref[r, :].astype(BF16), wout_ref[half:, :]))
        return _layer_norm(ALPHA * x_ref[r, :] + mix, g1_ref[...], b1_ref[...])

    rows = [pl.ds(k * sub, sub) for k in range(n_sub)]
    x1s = [front(r) for r in rows]
    for r, x1 in zip(rows, x1s):
        x1b = x1.astype(BF16)
        acc = jnp.zeros(x1.shape, F32)
        for c in range(D_FF // FF_CHUNK):
            hmid = _dot(x1b, w1_ref[:, c * FF_CHUNK:(c + 1) * FF_CHUNK])
            hmid = jnp.square(jnp.maximum(hmid, 0.0)).astype(BF16)
            acc = acc + _dot(hmid, w2_ref[c * FF_CHUNK:(c + 1) * FF_CHUNK, :])
        y_ref[r, :] = _layer_norm(ALPHA * x1 + acc, g2_ref[...], b2_ref[...])


def _post(x2d, mla, gla, w, tm):
    m = x2d.shape[0]
    row = lambda i: (i, 0)
    const = lambda i: (0, 0)
    resident = dict(pipeline_mode=pl.Buffered(1))
    return pl.pallas_call(
        _post_kernel, grid=(m // tm,),
        in_specs=[
            pl.BlockSpec((tm, D_MODEL), row),
            pl.BlockSpec((tm, MLA_HEADS * MLA_V), row),
            pl.BlockSpec((tm, GLA_WIDTH), row),
            pl.BlockSpec((D_MODEL, D_MODEL), const, **resident),
            pl.BlockSpec((1, D_MODEL), const),
            pl.BlockSpec((1, D_MODEL), const),
            pl.BlockSpec((D_MODEL, D_FF), const, **resident),
            pl.BlockSpec((D_FF, D_MODEL), const, **resident),
            pl.BlockSpec((1, D_MODEL), const),
            pl.BlockSpec((1, D_MODEL), const),
        ],
        out_specs=pl.BlockSpec((tm, D_MODEL), row),
        out_shape=jax.ShapeDtypeStruct((m, D_MODEL), F32),
        compiler_params=pltpu.CompilerParams(
            dimension_semantics=("arbitrary",), vmem_limit_bytes=VMEM_LIMIT),
        name="post",
    )(x2d, mla, gla, w['w_out'], w['ln1_g'], w['ln1_b'], w['w1'], w['w2'],
      w['ln2_g'], w['ln2_b'])


def _permute_w_in(w):
    sizes = (Q_RANK, KV_RANK, MLA_ROPE, GLA_KW, GLA_KW, GLA_WIDTH, GATE_RANK, GLA_WIDTH)
    off = np.concatenate([[0], np.cumsum(sizes)]).tolist()
    cq, ckv, kr, gq, gk, gv, gr, gg = [w[:, off[i]:off[i + 1]] for i in range(8)]
    half = MLA_ROPE // 2
    pad = jnp.zeros((w.shape[0], LANES - GATE_RANK), w.dtype)
    return jnp.concatenate(
        [cq, ckv, gq, gk, gv, gg, kr, kr[:, half:], kr[:, :half], gr, pad], axis=1).astype(BF16)


def _permute_w_uq(w):
    per_head = MLA_NOPE + MLA_ROPE
    half = MLA_ROPE // 2
    nope, rope, rope_sw = [], [], []
    for h in range(MLA_HEADS):
        base = h * per_head
        nope.append(w[:, base:base + MLA_NOPE])
        r = w[:, base + MLA_NOPE:base + per_head]
        rope.append(r)
        rope_sw += [r[:, half:], r[:, :half]]
    return jnp.concatenate(nope + rope + rope_sw, axis=1).astype(BF16)


def _prep_weights(w_in, mla_q_norm, mla_w_uq, mla_kv_norm, mla_w_uk, mla_w_uv,
                  gla_w_gate2, gla_b_gate, gla_norm, w_out, ln1_g, ln1_b,
                  mlp_w1, mlp_w2, ln2_g, ln2_b, l):
    w_in_p = _permute_w_in(w_in[l])
    w_uq_p = _permute_w_uq(mla_w_uq[l])
    return dict(
        w_in=w_in_p,
        q_norm=mla_q_norm[l][None, :],
        w_uq=w_uq_p,
        kv_norm=mla_kv_norm[l][None, :],
        w_uk=jnp.transpose(mla_w_uk[l], (1, 2, 0)).astype(BF16),
        w_ukv=jnp.concatenate(
            [mla_w_uk[l].reshape(KV_RANK, MLA_HEADS * MLA_NOPE),
             mla_w_uv[l].reshape(KV_RANK, MLA_HEADS * MLA_V)], axis=1).astype(BF16),
        w_uv=jnp.transpose(mla_w_uv[l], (1, 0, 2)).astype(BF16),
        w_gate2=jnp.pad(gla_w_gate2[l], ((0, LANES - GATE_RANK), (0, 0))).astype(BF16),
        b_gate=gla_b_gate[l][None, :],
        gla_norm=gla_norm[l][None, :],
        w_out=w_out[l].astype(BF16),
        ln1_g=ln1_g[l][None, :], ln1_b=ln1_b[l][None, :],
        w1=mlp_w1[l].astype(BF16), w2=mlp_w2[l].astype(BF16),
        ln2_g=ln2_g[l][None, :], ln2_b=ln2_b[l][None, :],
    )


def _rope_tables(pos):
    half = MLA_ROPE // 2
    inv = ROPE_THETA ** (-jnp.arange(0, MLA_ROPE, 2, dtype=F32) / MLA_ROPE)
    inv4 = jnp.tile(inv, LANES // half)
    sign = jnp.tile(jnp.concatenate([-jnp.ones((half,), F32), jnp.ones((half,), F32)]),
                    LANES // MLA_ROPE)
    ang = pos.astype(F32)[:, None] * inv4[None, :]
    return jnp.cos(ang), jnp.sin(ang) * sign[None, :]


def kernel(x_prompt, x_sample, cache_ckv, cache_krope, state_gla, page_table, w_in,
           mla_q_norm, mla_w_uq, mla_kv_norm, mla_w_uk, mla_w_uv, gla_w_gate2, gla_b_gate,
           gla_norm, w_out, ln1_g, ln1_b, mlp_w1, mlp_w2, ln2_g, ln2_b):
    assert w_in.shape[0] == DEPTH == 1
    batch, seq, _ = x_prompt.shape
    n_dec, t_new, _ = x_sample.shape
    assert t_new == 1
    l = 0
    w = _prep_weights(w_in, mla_q_norm, mla_w_uq, mla_kv_norm, mla_w_uk, mla_w_uv,
                      gla_w_gate2, gla_b_gate, gla_norm, w_out, ln1_g, ln1_b,
                      mlp_w1, mlp_w2, ln2_g, ln2_b, l)

    xp = x_prompt.reshape(batch * seq, D_MODEL)
    cos_p, sin_p = _rope_tables(jnp.arange(seq, dtype=jnp.int32))
    q, k_heads, ckv_p, kr_p, gq, gk, gv, la, gg, v_t = _proj(xp, cos_p, sin_p, w, 512, True)
    mla_p = _mla_prompt(q, k_heads, v_t, batch, seq)
    gla_p, s_p = _gla_prompt(gq, gk, gv, la, gg, w['gla_norm'], batch, seq)
    y_p = _post(xp, mla_p, gla_p, w, tm=512)

    xs = x_sample.reshape(n_dec, D_MODEL)
    cos_s, sin_s = _rope_tables(jnp.full((n_dec,), PAST_LEN, dtype=jnp.int32))
    q, kcat, ckv_s, kr_s, gq, gk, gv, la, gg = _proj(xs, cos_s, sin_s, w, n_dec, False)
    q_s = jnp.pad(jnp.transpose(q, (1, 0, 2)),
                  ((0, 0), (0, SAMPLE_Q_ROWS - MLA_HEADS), (0, 0)))
    krope_t = jnp.swapaxes(cache_krope[l], 1, 2)
    mla_s = _mla_sample(page_table, q_s, kcat[:, None, :], w['w_uv'],
                        cache_ckv[l], krope_t)
    gla_s, s_s = _gla_step(state_gla[l].reshape(n_dec, GLA_KW, GLA_DV),
                           gq, gk, gv, la, gg, w['gla_norm'])
    y_s = _post(xs, mla_s.reshape(n_dec, MLA_HEADS * MLA_V), gla_s, w, tm=n_dec)

    return (y_p.reshape(batch, seq, D_MODEL),
            y_s.reshape(n_dec, 1, D_MODEL),
            ckv_p.reshape(1, batch, seq, KV_RANK),
            kr_p.reshape(1, batch, seq, MLA_ROPE),
            s_p.reshape(1, batch, GLA_HEADS, GLA_DK, GLA_DV),
            ckv_s.reshape(1, n_dec, 1, KV_RANK),
            kr_s.reshape(1, n_dec, 1, MLA_ROPE),
            s_s.reshape(1, n_dec, GLA_HEADS, GLA_DK, GLA_DV))
```

```python
import functools

import numpy as np
import jax
import jax.numpy as jnp
from jax import lax
from jax.experimental import pallas as pl
from jax.experimental.pallas import tpu as pltpu

D_MODEL = 1024
PAST_LEN = 16384
PAGE_SIZE = 128
MLA_HEADS = 4
MLA_V = 128
MLA_NOPE = 128
MLA_ROPE = 64
Q_RANK = 384
KV_RANK = 256
MLA_SCALE = (MLA_NOPE + MLA_ROPE) ** -0.5
LOG2_E = 1.4426950408889634
ROPE_THETA = 10000.0
GLA_HEADS = 4
GLA_DV = 128
GLA_DK = 64
GLA_WIDTH = GLA_HEADS * GLA_DV
GLA_KW = GLA_HEADS * GLA_DK
GATE_RANK = 16
GATE_TAU = 16.0
GLA_CHUNK = 64
D_FF = 4 * D_MODEL
DEPTH = 1
ALPHA = (2.0 * DEPTH) ** 0.25
EPS = 1e-5

LANES = 128
BF16_SUBLANES = 16
QK_WIDTH = KV_RANK + LANES
HEAD_QK = MLA_NOPE + LANES
VMEM_LIMIT = 56 * 1024 * 1024
KEY_TILE = 256
STREAM_HEADS = 4
STEP_TILES = 2

_C_CQ = 0
_C_CKV = _C_CQ + Q_RANK
_C_GQ = _C_CKV + KV_RANK
_C_GK = _C_GQ + GLA_KW
_C_GV = _C_GK + GLA_KW
_C_GG = _C_GV + GLA_WIDTH
_C_KR = _C_GG + GLA_WIDTH
_C_GR = _C_KR + 2 * MLA_ROPE
IN_COLS_P = _C_GR + LANES

BF16 = jnp.bfloat16
F32 = jnp.float32


def _dot(a, b):
    return jnp.dot(a, b, preferred_element_type=F32)


def _dot_nt(a, b):
    return lax.dot_general(a, b, (((1,), (1,)), ((), ())), preferred_element_type=F32)


def _dot_tn(a, b):
    return lax.dot_general(a, b, (((0,), (0,)), ((), ())), preferred_element_type=F32)


def _rms(x, g):
    return x * lax.rsqrt(jnp.mean(x * x, axis=-1, keepdims=True) + EPS) * g


def _layer_norm(x, g, b):
    mu = jnp.mean(x, axis=-1, keepdims=True)
    xc = x - mu
    var = jnp.mean(xc * xc, axis=-1, keepdims=True)
    return xc * lax.rsqrt(var + EPS) * g + b


def _proj_kernel(x_ref, cos_ref, sin_ref, win_ref, qn_ref, wuq_ref, kvn_ref, wuk_ref,
                 wukv_ref, wg2_ref, bg_ref,
                 q_ref, key_ref, ckv_ref, kr_ref, gq_ref, gk_ref, gv_ref, la_ref, gg_ref,
                 vt_ref=None):
    tm = x_ref.shape[0]
    sub = KEY_TILE if tm % KEY_TILE == 0 else tm
    for k in range(tm // sub):
        _proj_rows(pl.ds(k * sub, sub), k, x_ref, cos_ref, sin_ref, win_ref, qn_ref, wuq_ref,
                   kvn_ref, wuk_ref, wukv_ref, wg2_ref, bg_ref, q_ref, key_ref, ckv_ref,
                   kr_ref, gq_ref, gk_ref, gv_ref, la_ref, gg_ref, vt_ref)


def _proj_rows(r, k, x_ref, cos_ref, sin_ref, win_ref, qn_ref, wuq_ref, kvn_ref, wuk_ref,
               wukv_ref, wg2_ref, bg_ref, q_ref, key_ref, ckv_ref, kr_ref, gq_ref, gk_ref,
               gv_ref, la_ref, gg_ref, vt_ref):
    xb = x_ref[r, :].astype(BF16)
    cos = cos_ref[r, :]
    sin = sin_ref[r, :]

    cq = _dot(xb, win_ref[:, _C_CQ:_C_CQ + Q_RANK])
    cqn = _rms(cq, qn_ref[...]).astype(BF16)
    q = _dot(cqn, wuq_ref[...])
    nh = MLA_HEADS * MLA_NOPE
    nr = MLA_HEADS * MLA_ROPE
    low_half = lax.broadcasted_iota(jnp.int32, (1, LANES), 1) < MLA_ROPE
    per_head = vt_ref is not None
    q_main = MLA_NOPE if per_head else KV_RANK
    q_scale = MLA_SCALE * LOG2_E if per_head else MLA_SCALE
    for h in range(MLA_HEADS):
        nope = q[:, h * MLA_NOPE:(h + 1) * MLA_NOPE]
        if not per_head:
            nope = _dot(nope.astype(BF16), wuk_ref[h])
        q_ref[h, r, 0:q_main] = (nope * q_scale).astype(BF16)
    for pair in range(MLA_HEADS // 2):
        lanes = slice(nh + pair * LANES, nh + (pair + 1) * LANES)
        lanes_sw = slice(nh + nr + pair * LANES, nh + nr + (pair + 1) * LANES)
        rope2 = (q[:, lanes] * cos + q[:, lanes_sw] * sin) * q_scale
        for j, blk in enumerate((rope2, pltpu.roll(rope2, MLA_ROPE, 1))):
            q_ref[2 * pair + j, r, q_main:q_main + LANES] = jnp.where(
                low_half, blk, 0.0).astype(BF16)

    ckv = _rms(_dot(xb, win_ref[:, _C_CKV:_C_CKV + KV_RANK]), kvn_ref[...])
    ckv_ref[r, :] = ckv
    kr_gr = _dot(xb, win_ref[:, _C_KR:_C_GR + LANES])
    krr = kr_gr[:, :2 * MLA_ROPE]
    k_rope = (krr[:, :MLA_ROPE] * cos[:, :MLA_ROPE]
              + krr[:, MLA_ROPE:] * sin[:, :MLA_ROPE])
    kr_ref[r, :] = k_rope
    ckv_b = ckv.astype(BF16)
    kr_pad = jnp.concatenate([k_rope, jnp.zeros_like(k_rope)], axis=-1).astype(BF16)
    if per_head:
        kv = _dot(ckv_b, wukv_ref[...])
        hk = MLA_HEADS * MLA_NOPE
        for h in range(MLA_HEADS):
            base = h * HEAD_QK
            key_ref[r, base:base + MLA_NOPE] = kv[:, h * MLA_NOPE:(h + 1) * MLA_NOPE].astype(BF16)
            key_ref[r, base + MLA_NOPE:base + HEAD_QK] = kr_pad
        vt_ref[k] = jnp.transpose(kv[:, hk:]).astype(BF16)
    else:
        key_ref[r, 0:KV_RANK] = ckv_b
        key_ref[r, KV_RANK:QK_WIDTH] = kr_pad

    gq_ref[r, :] = _dot(xb, win_ref[:, _C_GQ:_C_GQ + GLA_KW]) * (GLA_DK ** -0.5)
    gk_ref[r, :] = _dot(xb, win_ref[:, _C_GK:_C_GK + GLA_KW])
    gv_ref[r, :] = _dot(xb, win_ref[:, _C_GV:_C_GV + GLA_WIDTH]).astype(gv_ref.dtype)
    gg_ref[r, :] = _dot(xb, win_ref[:, _C_GG:_C_GG + GLA_WIDTH])
    gr = kr_gr[:, 2 * MLA_ROPE:].astype(BF16)
    z = _dot(gr, wg2_ref[...]) + bg_ref[...]
    la_ref[r, :] = (jnp.minimum(z, 0.0) - jnp.log1p(jnp.exp(-jnp.abs(z)))) / GATE_TAU


def _proj(x2d, cos_t, sin_t, w, tm, per_head):
    m = x2d.shape[0]
    q_width = HEAD_QK if per_head else QK_WIDTH
    key_width = MLA_HEADS * HEAD_QK if per_head else QK_WIDTH
    grid = (m // tm,)
    row = lambda i: (i, 0)
    const2 = lambda i: (0, 0)
    const3 = lambda i: (0, 0, 0)
    const3_row = lambda i: (i, 0, 0)
    pos_blocks = cos_t.shape[0] // tm
    pos_row = lambda i: (i % pos_blocks, 0)
    in_specs = [
        pl.BlockSpec((tm, D_MODEL), row),
        pl.BlockSpec((tm, LANES), pos_row),
        pl.BlockSpec((tm, LANES), pos_row),
        pl.BlockSpec((D_MODEL, IN_COLS_P), const2),
        pl.BlockSpec((1, Q_RANK), const2),
        pl.BlockSpec(w['w_uq'].shape, const2),
        pl.BlockSpec((1, KV_RANK), const2),
        pl.BlockSpec((MLA_HEADS, MLA_NOPE, KV_RANK), const3),
        pl.BlockSpec(w['w_ukv'].shape, const2),
        pl.BlockSpec((LANES, GLA_KW), const2),
        pl.BlockSpec((1, GLA_KW), const2),
    ]
    out_shape = [
        jax.ShapeDtypeStruct((MLA_HEADS, m, q_width), BF16),
        jax.ShapeDtypeStruct((m, key_width), BF16),
        jax.ShapeDtypeStruct((m, KV_RANK), F32),
        jax.ShapeDtypeStruct((m, MLA_ROPE), F32),
        jax.ShapeDtypeStruct((m, GLA_KW), F32),
        jax.ShapeDtypeStruct((m, GLA_KW), F32),
        jax.ShapeDtypeStruct((m, GLA_WIDTH), BF16 if per_head else F32),
        jax.ShapeDtypeStruct((m, GLA_KW), F32),
        jax.ShapeDtypeStruct((m, GLA_WIDTH), F32),
    ]
    out_specs = [
        pl.BlockSpec((MLA_HEADS, tm, q_width), lambda i: (0, i, 0)),
        pl.BlockSpec((tm, key_width), row),
        pl.BlockSpec((tm, KV_RANK), row),
        pl.BlockSpec((tm, MLA_ROPE), row),
        pl.BlockSpec((tm, GLA_KW), row),
        pl.BlockSpec((tm, GLA_KW), row),
        pl.BlockSpec((tm, GLA_WIDTH), row),
        pl.BlockSpec((tm, GLA_KW), row),
        pl.BlockSpec((tm, GLA_WIDTH), row),
    ]
    if per_head:
        vt_rows = MLA_HEADS * MLA_V
        out_shape.append(jax.ShapeDtypeStruct((m // KEY_TILE, vt_rows, KEY_TILE), BF16))
        out_specs.append(pl.BlockSpec((tm // KEY_TILE, vt_rows, KEY_TILE), const3_row))
    return pl.pallas_call(
        _proj_kernel, grid=grid, in_specs=in_specs, out_specs=out_specs, out_shape=out_shape,
        compiler_params=pltpu.CompilerParams(
            dimension_semantics=("arbitrary",), vmem_limit_bytes=VMEM_LIMIT),
        name="proj",
    )(x2d, cos_t, sin_t, w['w_in'], w['q_norm'], w['w_uq'], w['kv_norm'], w['w_uk'],
      w['w_ukv'], w['w_gate2'], w['b_gate'])


def _mla_prompt_kernel(q_ref, k_ref, vt_ref, o_ref, m_scr, l_scr, acc_scr, s_a, s_b, *, tq):
    i = pl.program_id(1)
    rows = MLA_HEADS * tq
    size = STEP_TILES * KEY_TILE
    hw = STREAM_HEADS * tq
    n_streams = MLA_HEADS // STREAM_HEADS
    m_scr[...] = jnp.full(m_scr.shape, -jnp.inf, F32)
    l_scr[...] = jnp.zeros(l_scr.shape, F32)
    acc_scr[...] = jnp.zeros(acc_scr.shape, F32)

    def scores_to(s_ref, j):
        start = pl.multiple_of(j * size, size)
        for h in range(MLA_HEADS):
            g, hh = divmod(h, STREAM_HEADS)
            kh = k_ref[pl.ds(start, size), h * HEAD_QK:(h + 1) * HEAD_QK]
            s_ref[g, :, hh * tq:(hh + 1) * tq] = _dot_nt(kh, q_ref[h])

    def fold(s_ref, j, masked):
        tile0 = j * STEP_TILES
        if masked:
            key = j * size + lax.broadcasted_iota(jnp.int32, (size, hw), 0)
            tok = i * tq + (lax.broadcasted_iota(jnp.int32, (size, hw), 1) & (tq - 1))
            visible = key <= tok
        m_all = m_scr[...]
        l_all = l_scr[...]
        m_out, l_out, updates = [], [], []
        ones_rows = jnp.ones((BF16_SUBLANES, KEY_TILE), BF16)
        for g in range(n_streams):
            lanes = slice(g * hw, (g + 1) * hw)
            st = s_ref[g]
            if masked:
                st = jnp.where(visible, st, -jnp.inf)
            m_old = m_all[:, lanes]
            m_new = jnp.maximum(m_old, jnp.max(st, axis=0, keepdims=True))
            p = jnp.exp2(st - m_new)
            alpha = jnp.exp2(m_old - m_new)
            m_out.append(m_new)
            pb = p.astype(BF16)
            pvs, sums = [], []
            for hh in range(STREAM_HEADS):
                h = g * STREAM_HEADS + hh
                vrows = slice(h * MLA_V, (h + 1) * MLA_V)
                cols = slice(hh * tq, (hh + 1) * tq)
                pv = None
                for t in range(STEP_TILES):
                    lhs = jnp.concatenate([vt_ref[tile0 + t, vrows, :], ones_rows], axis=0)
                    d = _dot(lhs, pb[t * KEY_TILE:(t + 1) * KEY_TILE, cols])
                    pv = d if pv is None else pv + d
                pvs.append(pv[:MLA_V])
                sums.append(pv[MLA_V:MLA_V + 1])
            l_out.append(alpha * l_all[:, lanes] + jnp.concatenate(sums, axis=1))
            updates.append((lanes, alpha, jnp.concatenate(pvs, axis=1)))
        for lanes, alpha, pv in updates:
            acc_scr[:, lanes] = alpha * acc_scr[:, lanes] + pv
        m_scr[...] = jnp.concatenate(m_out, axis=1)
        l_scr[...] = jnp.concatenate(l_out, axis=1)

    last = (i * tq) // size
    scores_to(s_a, 0)

    def two_steps(k, carry):
        scores_to(s_b, 2 * k + 1)
        fold(s_a, 2 * k, masked=False)
        scores_to(s_a, 2 * k + 2)
        fold(s_b, 2 * k + 1, masked=False)
        return carry

    lax.fori_loop(0, last // 2, two_steps, 0)

    @pl.when(last % 2 == 1)
    def _():
        scores_to(s_b, last)
        fold(s_a, last - 1, masked=False)
        fold(s_b, last, masked=True)

    @pl.when(last % 2 == 0)
    def _():
        fold(s_a, last, masked=True)

    o_t = acc_scr[...] / l_scr[...]
    for h in range(MLA_HEADS):
        o_ref[:, h * MLA_V:(h + 1) * MLA_V] = jnp.transpose(
            o_t[:, h * tq:(h + 1) * tq]).astype(o_ref.dtype)


def _mla_prompt(q, k_heads, v_t, batch, seq):
    tq = KEY_TILE
    resident = dict(pipeline_mode=pl.Buffered(1))
    nq = seq // tq
    kern = functools.partial(_mla_prompt_kernel, tq=tq)
    rows = MLA_HEADS * tq
    score_tile = (MLA_HEADS // STREAM_HEADS, STEP_TILES * KEY_TILE, STREAM_HEADS * tq)
    return pl.pallas_call(
        kern, grid=(batch, nq),
        in_specs=[
            pl.BlockSpec((MLA_HEADS, tq, HEAD_QK), lambda b, i: (0, b * nq + i, 0)),
            pl.BlockSpec((seq, MLA_HEADS * HEAD_QK), lambda b, i: (b, 0), **resident),
            pl.BlockSpec((seq // KEY_TILE, MLA_HEADS * MLA_V, KEY_TILE),
                         lambda b, i: (b, 0, 0), **resident),
        ],
        out_specs=pl.BlockSpec((tq, MLA_HEADS * MLA_V), lambda b, i: (b * nq + i, 0)),
        out_shape=jax.ShapeDtypeStruct((batch * seq, MLA_HEADS * MLA_V), BF16),
        scratch_shapes=[pltpu.VMEM((1, rows), F32), pltpu.VMEM((1, rows), F32),
                        pltpu.VMEM((MLA_V, rows), F32),
                        pltpu.VMEM(score_tile, F32), pltpu.VMEM(score_tile, F32)],
        compiler_params=pltpu.CompilerParams(
            dimension_semantics=("arbitrary", "arbitrary"), vmem_limit_bytes=VMEM_LIMIT),
        name="mla_prompt",
    )(q, k_heads, v_t)


SAMPLE_Q_ROWS = 16
PAGES_PER_CHUNK = 32
SAMPLE_SLOTS = 4
SAMPLE_PREFETCH = SAMPLE_SLOTS - 1


def _mla_sample_kernel(pt_ref, q_ref, knew_ref, wuv_ref, ckv_hbm, krt_hbm, o_ref,
                       cbuf, kbuf, cbf, sem_c, sem_k, *, n_batch, n_chunks):
    b = pl.program_id(0)
    g_pages = PAGES_PER_CHUNK

    def page_copies(bb, c, slot, g):
        page = pt_ref[bb, c * g_pages + g]
        tok = pl.ds(g * PAGE_SIZE, PAGE_SIZE)
        return (pltpu.make_async_copy(ckv_hbm.at[page], cbuf.at[slot, tok, :], sem_c.at[slot]),
                pltpu.make_async_copy(krt_hbm.at[page], kbuf.at[slot, :, tok], sem_k.at[slot]))

    def start_chunk(bb, c, slot):
        for g in range(g_pages):
            cc, ck = page_copies(bb, c, slot, g)
            cc.start(priority=g % 2)
            ck.start(priority=g % 2)

    def wait_chunk(bb, c, slot):
        for g in range(g_pages):
            cc, ck = page_copies(bb, c, slot, g)
            cc.wait()
            ck.wait()

    def start_ahead(c):
        cn = c + SAMPLE_PREFETCH
        if cn < n_chunks:
            start_chunk(b, cn, cn % SAMPLE_SLOTS)
        else:
            @pl.when(b + 1 < n_batch)
            def _():
                start_chunk(b + 1, cn - n_chunks, (cn - n_chunks) % SAMPLE_SLOTS)

    @pl.when(b == 0)
    def _():
        for c0 in range(SAMPLE_PREFETCH):
            start_chunk(0, c0, c0)

    q = q_ref[0]
    q_lat = q[:, :KV_RANK]
    q_rope = q[:, KV_RANK:KV_RANK + MLA_ROPE]
    m = jnp.full((SAMPLE_Q_ROWS, 1), -jnp.inf, F32)
    l = jnp.zeros((SAMPLE_Q_ROWS, 1), F32)
    acc = jnp.zeros((SAMPLE_Q_ROWS, KV_RANK), F32)

    n_tiles = g_pages * PAGE_SIZE // KEY_TILE

    def tiled_pv(p_b, cbf_slot):
        out = _dot(p_b[:, 0:KEY_TILE], cbf[cbf_slot, 0:KEY_TILE, :])
        for t in range(1, n_tiles):
            tok = slice(t * KEY_TILE, (t + 1) * KEY_TILE)
            out = out + _dot(p_b[:, tok], cbf[cbf_slot, tok, :])
        return out

    pending = None
    for c in range(n_chunks):
        slot = c % SAMPLE_SLOTS
        wait_chunk(b, c, slot)
        if pending is not None:
            alpha_p, p_p, slot_p = pending
            acc = alpha_p * acc + tiled_pv(p_p, slot_p)
        s_tiles = []
        for t in range(n_tiles):
            tok = slice(t * KEY_TILE, (t + 1) * KEY_TILE)
            cb = cbuf[slot, tok, :].astype(BF16)
            cbf[c % 2, tok, :] = cb
            kb = kbuf[slot, :, tok].astype(BF16)
            s_tiles.append(_dot_nt(q_lat, cb) + _dot(q_rope, kb))
        s = jnp.concatenate(s_tiles, axis=1)
        start_ahead(c)
        m_new = jnp.maximum(m, jnp.max(s, axis=-1, keepdims=True))
        p = jnp.exp(s - m_new)
        alpha = jnp.exp(m - m_new)
        l = alpha * l + jnp.sum(p, axis=-1, keepdims=True)
        m = m_new
        pending = (alpha, p.astype(BF16), c % 2)
    alpha_p, p_p, slot_p = pending
    acc = alpha_p * acc + tiled_pv(p_p, slot_p)


    knew = knew_ref[0].astype(F32)
    s_self = jnp.sum(q.astype(F32) * knew, axis=-1, keepdims=True)
    m_new = jnp.maximum(m, s_self)
    p_self = jnp.exp(s_self - m_new)
    alpha = jnp.exp(m - m_new)
    l = alpha * l + p_self
    acc = alpha * acc + p_self.astype(BF16).astype(F32) * knew[:, :KV_RANK]
    o_lat = (acc / l).astype(BF16)
    for h in range(MLA_HEADS):
        res = _dot(o_lat, wuv_ref[h])
        o_ref[0, :, h * MLA_V:(h + 1) * MLA_V] = res[h:h + 1, :]


def _mla_sample(page_table, q_s, knew, w_uv, cache_ckv, cache_krope_t):
    n_batch, n_pages = page_table.shape
    n_chunks = n_pages // PAGES_PER_CHUNK
    chunk_rows = PAGES_PER_CHUNK * PAGE_SIZE
    assert n_chunks * PAGES_PER_CHUNK == n_pages and n_chunks % SAMPLE_SLOTS == 0
    assert SAMPLE_PREFETCH <= n_chunks
    kern = functools.partial(_mla_sample_kernel, n_batch=n_batch, n_chunks=n_chunks)
    grid_spec = pltpu.PrefetchScalarGridSpec(
        num_scalar_prefetch=1,
        grid=(n_batch,),
        in_specs=[
            pl.BlockSpec((1, SAMPLE_Q_ROWS, QK_WIDTH), lambda b, pt: (b, 0, 0)),
            pl.BlockSpec((1, 1, QK_WIDTH), lambda b, pt: (b, 0, 0)),
            pl.BlockSpec((MLA_HEADS, KV_RANK, MLA_V), lambda b, pt: (0, 0, 0)),
            pl.BlockSpec(memory_space=pl.ANY),
            pl.BlockSpec(memory_space=pl.ANY),
        ],
        out_specs=pl.BlockSpec((1, 1, MLA_HEADS * MLA_V), lambda b, pt: (b, 0, 0)),
        scratch_shapes=[
            pltpu.VMEM((SAMPLE_SLOTS, chunk_rows, KV_RANK), F32),
            pltpu.VMEM((SAMPLE_SLOTS, MLA_ROPE, chunk_rows), F32),
            pltpu.VMEM((2, chunk_rows, KV_RANK), BF16),
            pltpu.SemaphoreType.DMA((SAMPLE_SLOTS,)),
            pltpu.SemaphoreType.DMA((SAMPLE_SLOTS,)),
        ],
    )
    return pl.pallas_call(
        kern, grid_spec=grid_spec,
        out_shape=jax.ShapeDtypeStruct((n_batch, 1, MLA_HEADS * MLA_V), F32),
        compiler_params=pltpu.CompilerParams(
            dimension_semantics=("arbitrary",), vmem_limit_bytes=VMEM_LIMIT),
        name="mla_sample",
    )(page_table, q_s, knew, w_uv, cache_ckv, cache_krope_t)


GLA_UNROLL = 8


def _split3(x):
    hi = x.astype(BF16)
    r1 = x - hi.astype(F32)
    mid = r1.astype(BF16)
    lo = (r1 - mid.astype(F32)).astype(BF16)
    return hi, mid, lo


def _gla_gate_out(o, gg, gnorm):
    return _rms(o, gnorm) * (gg * jax.nn.sigmoid(gg))


def _gla_prompt_kernel(gq_ref, gk_ref, gv_ref, la_ref, gg_ref, gn_ref, o_ref, s_out_ref,
                       s_scr, *, n_sub):
    t = pl.program_id(1)
    cs = GLA_CHUNK

    @pl.when(t == 0)
    def _():
        s_scr[...] = jnp.zeros(s_scr.shape, F32)

    ri = lax.broadcasted_iota(jnp.int32, (cs, cs), 0)
    ci = lax.broadcasted_iota(jnp.int32, (cs, cs), 1)
    tri = ri >= ci
    tri_b = tri.astype(BF16)
    lane_head = lax.broadcasted_iota(jnp.int32, (cs, GLA_KW), 1) // GLA_DK
    gnorm = gn_ref[...]

    def chunk(c, s_prev):
        r0 = pl.multiple_of(c * cs, cs)
        g = la_ref[pl.ds(r0, cs), :]
        g_hi, g_mid, g_lo = _split3(g)
        bcum = _dot(tri_b, g_hi) + _dot(tri_b, g_mid) + _dot(tri_b, g_lo)
        b_last = bcum[cs - 1:cs, :]
        qf = gq_ref[pl.ds(r0, cs), :]
        kf = gk_ref[pl.ds(r0, cs), :]
        v = gv_ref[pl.ds(r0, cs), :]
        q_t = qf * jnp.exp(bcum)
        k_t = (kf * jnp.exp(-bcum)).astype(BF16)
        k_h = (kf * jnp.exp(b_last - bcum)).astype(BF16)
        decay = jnp.exp(b_last)
        q_stack = jnp.concatenate(
            [jnp.where(lane_head == h, q_t, 0.0) for h in range(GLA_HEADS)],
            axis=0).astype(BF16)
        a_stack = _dot_nt(q_stack, k_t)
        o_inter = _dot(q_stack, s_prev.astype(BF16))
        kv = _dot_tn(k_h, v)
        decay_col = jnp.transpose(jnp.broadcast_to(decay, (GLA_DV, GLA_KW)))
        kv_diag = jnp.concatenate(
            [kv[h * GLA_DK:(h + 1) * GLA_DK, h * GLA_DV:(h + 1) * GLA_DV]
             for h in range(GLA_HEADS)], axis=0)
        s_new = decay_col * s_prev + kv_diag
        for h in range(GLA_HEADS):
            a_h = jnp.where(tri, a_stack[h * cs:(h + 1) * cs], 0.0).astype(BF16)
            o_h = _dot(a_h, v[:, h * GLA_DV:(h + 1) * GLA_DV]) + o_inter[h * cs:(h + 1) * cs]
            gg_h = gg_ref[pl.ds(r0, cs), h * GLA_DV:(h + 1) * GLA_DV]
            o_ref[pl.ds(r0, cs), h * GLA_DV:(h + 1) * GLA_DV] = _gla_gate_out(
                o_h, gg_h, gnorm).astype(o_ref.dtype)
        return s_new

    def trip(j, carry):
        s = s_scr[...]
        for u in range(GLA_UNROLL):
            s = chunk(j * GLA_UNROLL + u, s)
        s_scr[...] = s
        return carry

    lax.fori_loop(0, n_sub // GLA_UNROLL, trip, 0)
    s_out_ref[0] = s_scr[...]


def _gla_prompt(gq, gk, gv, la, gg, gnorm, batch, seq, tc=1024):
    nt = seq // tc
    n_sub = tc // GLA_CHUNK
    row = lambda b, t: (b * nt + t, 0)
    kern = functools.partial(_gla_prompt_kernel, n_sub=n_sub)
    return pl.pallas_call(
        kern, grid=(batch, nt),
        in_specs=[
            pl.BlockSpec((tc, GLA_KW), row),
            pl.BlockSpec((tc, GLA_KW), row),
            pl.BlockSpec((tc, GLA_WIDTH), row),
            pl.BlockSpec((tc, GLA_KW), row),
            pl.BlockSpec((tc, GLA_WIDTH), row),
            pl.BlockSpec((1, GLA_DV), lambda b, t: (0, 0)),
        ],
        out_specs=[
            pl.BlockSpec((tc, GLA_WIDTH), row),
            pl.BlockSpec((1, GLA_KW, GLA_DV), lambda b, t: (b, 0, 0)),
        ],
        out_shape=[
            jax.ShapeDtypeStruct((batch * seq, GLA_WIDTH), BF16),
            jax.ShapeDtypeStruct((batch, GLA_KW, GLA_DV), F32),
        ],
        scratch_shapes=[pltpu.VMEM((GLA_KW, GLA_DV), F32)],
        compiler_params=pltpu.CompilerParams(
            dimension_semantics=("arbitrary", "arbitrary"), vmem_limit_bytes=VMEM_LIMIT),
        name="gla_prompt",
    )(gq, gk, gv, la, gg, gnorm)


def _gla_step_kernel(s_ref, gq_ref, gk_ref, gv_ref, la_ref, gg_ref, gn_ref, o_ref, s_out_ref,
                     *, bb):
    gnorm = gn_ref[...]

    def col(x_row):
        return jnp.transpose(jnp.broadcast_to(x_row, (GLA_DV, GLA_KW)))

    for i in range(bb):
        s = s_ref[i]
        e_col = col(jnp.exp(la_ref[i:i + 1, :]))
        k_col = col(gk_ref[i:i + 1, :])
        q_col = col(gq_ref[i:i + 1, :].astype(BF16).astype(F32))
        v_row = gv_ref[i:i + 1, :].astype(F32)
        v_rows = jnp.concatenate(
            [jnp.broadcast_to(v_row[:, h * GLA_DV:(h + 1) * GLA_DV], (GLA_DK, GLA_DV))
             for h in range(GLA_HEADS)], axis=0)
        s_new = e_col * s + k_col * v_rows
        s_out_ref[i] = s_new
        prod = q_col * s_new.astype(BF16).astype(F32)
        for h in range(GLA_HEADS):
            o_h = jnp.sum(prod[h * GLA_DK:(h + 1) * GLA_DK], axis=0, keepdims=True)
            gg_h = gg_ref[i:i + 1, h * GLA_DV:(h + 1) * GLA_DV]
            o_ref[i:i + 1, h * GLA_DV:(h + 1) * GLA_DV] = _gla_gate_out(o_h, gg_h, gnorm)


def _gla_step(state, gq, gk, gv, la, gg, gnorm, bb=8):
    n = state.shape[0]
    row = lambda i: (i, 0)
    kern = functools.partial(_gla_step_kernel, bb=bb)
    return pl.pallas_call(
        kern, grid=(n // bb,),
        in_specs=[
            pl.BlockSpec((bb, GLA_KW, GLA_DV), lambda i: (i, 0, 0)),
            pl.BlockSpec((bb, GLA_KW), row),
            pl.BlockSpec((bb, GLA_KW), row),
            pl.BlockSpec((bb, GLA_WIDTH), row),
            pl.BlockSpec((bb, GLA_KW), row),
            pl.BlockSpec((bb, GLA_WIDTH), row),
            pl.BlockSpec((1, GLA_DV), lambda i: (0, 0)),
        ],
        out_specs=[
            pl.BlockSpec((bb, GLA_WIDTH), row),
            pl.BlockSpec((bb, GLA_KW, GLA_DV), lambda i: (i, 0, 0)),
        ],
        out_shape=[
            jax.ShapeDtypeStruct((n, GLA_WIDTH), F32),
            jax.ShapeDtypeStruct((n, GLA_KW, GLA_DV), F32),
        ],
        compiler_params=pltpu.CompilerParams(
            dimension_semantics=("arbitrary",), vmem_limit_bytes=VMEM_LIMIT),
        name="gla_step",
    )(state, gq, gk, gv, la, gg, gnorm)


FF_CHUNK = 1024
POST_SUBBLOCKS = 2


def _post_kernel(x_ref, mla_ref, gla_ref, wout_ref, g1_ref, b1_ref, w1_ref, w2_ref,
                 g2_ref, b2_ref, y_ref):
    half = MLA_HEADS * MLA_V
    tm = x_ref.shape[0]
    n_sub = POST_SUBBLOCKS if tm >= POST_SUBBLOCKS * KEY_TILE else 1
    sub = tm // n_sub

    def front(r):
        mix = (_dot(mla_ref[r, :].astype(BF16), wout_ref[0:half, :])
               + _dot(gla_ref[r, :].astype(BF16), wout_ref[half:, :]))
        return _layer_norm(ALPHA * x_ref[r, :] + mix, g1_ref[...], b1_ref[...])

    rows = [pl.ds(k * sub, sub) for k in range(n_sub)]
    x1s = [front(r) for r in rows]
    for r, x1 in zip(rows, x1s):
        x1b = x1.astype(BF16)
        acc = jnp.zeros(x1.shape, F32)
        for c in range(D_FF // FF_CHUNK):
            hmid = _dot(x1b, w1_ref[:, c * FF_CHUNK:(c + 1) * FF_CHUNK])
            hmid = jnp.square(jnp.maximum(hmid, 0.0)).astype(BF16)
            acc = acc + _dot(hmid, w2_ref[c * FF_CHUNK:(c + 1) * FF_CHUNK, :])
        y_ref[r, :] = _layer_norm(ALPHA * x1 + acc, g2_ref[...], b2_ref[...])


def _post(x2d, mla, gla, w, tm):
    m = x2d.shape[0]
    row = lambda i: (i, 0)
    const = lambda i: (0, 0)
    resident = dict(pipeline_mode=pl.Buffered(1))
    return pl.pallas_call(
        _post_kernel, grid=(m // tm,),
        in_specs=[
            pl.BlockSpec((tm, D_MODEL), row),
            pl.BlockSpec((tm, MLA_HEADS * MLA_V), row),
            pl.BlockSpec((tm, GLA_WIDTH), row),
            pl.BlockSpec((D_MODEL, D_MODEL), const, **resident),
            pl.BlockSpec((1, D_MODEL), const),
            pl.BlockSpec((1, D_MODEL), const),
            pl.BlockSpec((D_MODEL, D_FF), const, **resident),
            pl.BlockSpec((D_FF, D_MODEL), const, **resident),
            pl.BlockSpec((1, D_MODEL), const),
            pl.BlockSpec((1, D_MODEL), const),
        ],
        out_specs=pl.BlockSpec((tm, D_MODEL), row),
        out_shape=jax.ShapeDtypeStruct((m, D_MODEL), F32),
        compiler_params=pltpu.CompilerParams(
            dimension_semantics=("arbitrary",), vmem_limit_bytes=VMEM_LIMIT),
        name="post",
    )(x2d, mla, gla, w['w_out'], w['ln1_g'], w['ln1_b'], w['w1'], w['w2'],
      w['ln2_g'], w['ln2_b'])


def _permute_w_in(w):
    sizes = (Q_RANK, KV_RANK, MLA_ROPE, GLA_KW, GLA_KW, GLA_WIDTH, GATE_RANK, GLA_WIDTH)
    off = np.concatenate([[0], np.cumsum(sizes)]).tolist()
    cq, ckv, kr, gq, gk, gv, gr, gg = [w[:, off[i]:off[i + 1]] for i in range(8)]
    half = MLA_ROPE // 2
    pad = jnp.zeros((w.shape[0], LANES - GATE_RANK), w.dtype)
    return jnp.concatenate(
        [cq, ckv, gq, gk, gv, gg, kr, kr[:, half:], kr[:, :half], gr, pad], axis=1).astype(BF16)


def _permute_w_uq(w):
    per_head = MLA_NOPE + MLA_ROPE
    half = MLA_ROPE // 2
    nope, rope, rope_sw = [], [], []
    for h in range(MLA_HEADS):
        base = h * per_head
        nope.append(w[:, base:base + MLA_NOPE])
        r = w[:, base + MLA_NOPE:base + per_head]
        rope.append(r)
        rope_sw += [r[:, half:], r[:, :half]]
    return jnp.concatenate(nope + rope + rope_sw, axis=1).astype(BF16)


def _prep_weights(w_in, mla_q_norm, mla_w_uq, mla_kv_norm, mla_w_uk, mla_w_uv,
                  gla_w_gate2, gla_b_gate, gla_norm, w_out, ln1_g, ln1_b,
                  mlp_w1, mlp_w2, ln2_g, ln2_b, l):
    w_in_p = _permute_w_in(w_in[l])
    w_uq_p = _permute_w_uq(mla_w_uq[l])
    return dict(
        w_in=w_in_p,
        q_norm=mla_q_norm[l][None, :],
        w_uq=w_uq_p,
        kv_norm=mla_kv_norm[l][None, :],
        w_uk=jnp.transpose(mla_w_uk[l], (1, 2, 0)).astype(BF16),
        w_ukv=jnp.concatenate(
            [mla_w_uk[l].reshape(KV_RANK, MLA_HEADS * MLA_NOPE),
             mla_w_uv[l].reshape(KV_RANK, MLA_HEADS * MLA_V)], axis=1).astype(BF16),
        w_uv=jnp.transpose(mla_w_uv[l], (1, 0, 2)).astype(BF16),
        w_gate2=jnp.pad(gla_w_gate2[l], ((0, LANES - GATE_RANK), (0, 0))).astype(BF16),
        b_gate=gla_b_gate[l][None, :],
        gla_norm=gla_norm[l][None, :],
        w_out=w_out[l].astype(BF16),
        ln1_g=ln1_g[l][None, :], ln1_b=ln1_b[l][None, :],
        w1=mlp_w1[l].astype(BF16), w2=mlp_w2[l].astype(BF16),
        ln2_g=ln2_g[l][None, :], ln2_b=ln2_b[l][None, :],
    )


def _rope_tables(pos):
    half = MLA_ROPE // 2
    inv = ROPE_THETA ** (-jnp.arange(0, MLA_ROPE, 2, dtype=F32) / MLA_ROPE)
    inv4 = jnp.tile(inv, LANES // half)
    sign = jnp.tile(jnp.concatenate([-jnp.ones((half,), F32), jnp.ones((half,), F32)]),
                    LANES // MLA_ROPE)
    ang = pos.astype(F32)[:, None] * inv4[None, :]
    return jnp.cos(ang), jnp.sin(ang) * sign[None, :]


def kernel(x_prompt, x_sample, cache_ckv, cache_krope, state_gla, page_table, w_in,
           mla_q_norm, mla_w_uq, mla_kv_norm, mla_w_uk, mla_w_uv, gla_w_gate2, gla_b_gate,
           gla_norm, w_out, ln1_g, ln1_b, mlp_w1, mlp_w2, ln2_g, ln2_b):
    assert w_in.shape[0] == DEPTH == 1
    batch, seq, _ = x_prompt.shape
    n_dec, t_new, _ = x_sample.shape
    assert t_new == 1
    l = 0
    w = _prep_weights(w_in, mla_q_norm, mla_w_uq, mla_kv_norm, mla_w_uk, mla_w_uv,
                      gla_w_gate2, gla_b_gate, gla_norm, w_out, ln1_g, ln1_b,
                      mlp_w1, mlp_w2, ln2_g, ln2_b, l)

    xp = x_prompt.reshape(batch * seq, D_MODEL)
    cos_p, sin_p = _rope_tables(jnp.arange(seq, dtype=jnp.int32))
    q, k_heads, ckv_p, kr_p, gq, gk, gv, la, gg, v_t = _proj(xp, cos_p, sin_p, w, 512, True)
    mla_p = _mla_prompt(q, k_heads, v_t, batch, seq)
    gla_p, s_p = _gla_prompt(gq, gk, gv, la, gg, w['gla_norm'], batch, seq)
    y_p = _post(xp, mla_p, gla_p, w, tm=512)

    xs = x_sample.reshape(n_dec, D_MODEL)
    cos_s, sin_s = _rope_tables(jnp.full((n_dec,), PAST_LEN, dtype=jnp.int32))
    q, kcat, ckv_s, kr_s, gq, gk, gv, la, gg = _proj(xs, cos_s, sin_s, w, n_dec, False)
    q_s = jnp.pad(jnp.transpose(q, (1, 0, 2)),
                  ((0, 0), (0, SAMPLE_Q_ROWS - MLA_HEADS), (0, 0)))
    krope_t = jnp.swapaxes(cache_krope[l], 1, 2)
    mla_s = _mla_sample(page_table, q_s, kcat[:, None, :], w['w_uv'],
                        cache_ckv[l], krope_t)
    gla_s, s_s = _gla_step(state_gla[l].reshape(n_dec, GLA_KW, GLA_DV),
                           gq, gk, gv, la, gg, w['gla_norm'])
    y_s = _post(xs, mla_s.reshape(n_dec, MLA_HEADS * MLA_V), gla_s, w, tm=n_dec)

    return (y_p.reshape(batch, seq, D_MODEL),
            y_s.reshape(n_dec, 1, D_MODEL),
            ckv_p.reshape(1, batch, seq, KV_RANK),
            kr_p.reshape(1, batch, seq, MLA_ROPE),
            s_p.reshape(1, batch, GLA_HEADS, GLA_DK, GLA_DV),
            ckv_s.reshape(1, n_dec, 1, KV_RANK),
            kr_s.reshape(1, n_dec, 1, MLA_ROPE),
            s_s.reshape(1, n_dec, GLA_HEADS, GLA_DK, GLA_DV))
```
